```python
import jax, jax.numpy as jnp
from jax import lax
import numpy as np

D_MODEL = 2048
BATCH = 16
SEQ = 2048
DEPTH = 1
DEC_BATCH = 32
DEC_SEQ = 1
PAST_LEN = 16384
PAGE_SIZE = 128

HEAD_DIM = 128
N_HEADS = D_MODEL // HEAD_DIM
H_NSA = N_HEADS // 2
H_FOX = N_HEADS - H_NSA
W_NSA = H_NSA * HEAD_DIM
W_FOX = H_FOX * HEAD_DIM
MIX_WIDTH = W_NSA + W_FOX
G_NSA = 2
HPG_NSA = H_NSA // G_NSA
KV_FOX = 4
HPG_FOX = H_FOX // KV_FOX
CMP_LEN = 32
CMP_STRIDE = 16
CMP_HID = 2 * HEAD_DIM
SEL_BLOCK = 64
N_SEL = 16
N_LOCAL_SEL = 2
WINDOW = 512
Q_BLOCK = 128
N_EXPERTS = 32
TOP_K = 4
D_FF = D_MODEL
SWIGLU_ALPHA = 1.702
SWIGLU_LIMIT = 7.0
MOE_BLOCK = 128
RMS_EPS = 1e-6
ATTN_SCALE = HEAD_DIM ** -0.5
NEG_INF = -1e30
FORCE_SCORE = 1e9
PROJ_SIZES = (W_NSA, 3 * 2 * G_NSA * HEAD_DIM, 3 * H_NSA, W_FOX, 2 * KV_FOX * HEAD_DIM, H_FOX)
N_IN = sum(PROJ_SIZES)

kernel_name = 'hybrid_nsa_fox_moe_step'

F32 = jnp.float32


def rmsnorm(x, g):
    xf = x.astype(F32)
    y = xf * lax.rsqrt(jnp.mean(xf * xf, axis=-1, keepdims=True) + RMS_EPS)
    return (y * g.astype(F32)).astype(x.dtype)


def masked_softmax(s, mask):
    s = jnp.where(mask, s, NEG_INF)
    e = jnp.where(mask, jnp.exp(s - jnp.max(s, axis=-1, keepdims=True)), 0.0)
    return e / jnp.maximum(jnp.sum(e, axis=-1, keepdims=True), 1e-30)


def alibi_slopes():
    h = jnp.arange(1, H_NSA + 1, dtype=F32)
    return jnp.exp2(-8.0 * h / H_NSA).reshape(G_NSA, HPG_NSA)


def gather_pages(pool, page_table):
    g = pool[page_table]
    return g.reshape((g.shape[0], g.shape[1] * g.shape[2]) + g.shape[3:])


def project(h, P):
    B, T = h.shape[:2]
    p = jnp.einsum('btd,dn->btn', h, P['w_in'])
    cuts = np.cumsum(PROJ_SIZES)[:-1].tolist()
    qn, kvn, gn, qf, kvf, fl = jnp.split(p, cuts, axis=-1)
    qn = qn.reshape(B, T, G_NSA, HPG_NSA, HEAD_DIM)
    kvn = kvn.reshape(B, T, 3, 2, G_NSA, HEAD_DIM)
    gates = jax.nn.sigmoid((gn + P['b_nsa_gate']).astype(F32)).reshape(B, T, 3, H_NSA)
    qf = qf.reshape(B, T, KV_FOX, HPG_FOX, HEAD_DIM)
    kvf = kvf.reshape(B, T, 2, KV_FOX, HEAD_DIM)
    logf = jax.nn.log_sigmoid((fl + P['b_fox_forget']).astype(F32))
    return qn, kvn, gates, qf, kvf, logf


def compress_blocks(kv, P):
    B, L = kv.shape[:2]
    n_chunk = L // CMP_STRIDE
    ch = kv[:, :n_chunk * CMP_STRIDE].reshape(B, n_chunk, CMP_STRIDE, 2, G_NSA, HEAD_DIM)
    pe = jnp.swapaxes(P['pe_cmp'], 0, 1)[None, None, :, :, None, :]
    w1 = P['w_cmp1']
    first = jnp.einsum('bcsjgd,jsdh->bcjgh', ch + pe[:, :, :CMP_STRIDE], w1[:, :CMP_STRIDE])
    second = jnp.einsum('bcsjgd,jsdh->bcjgh', ch + pe[:, :, CMP_STRIDE:], w1[:, CMP_STRIDE:])
    hid = jax.nn.gelu(first[:, :-1] + second[:, 1:] + P['b_cmp1'][None, None, :, None, :])
    return jnp.einsum('bcjgh,jhd->bcjgd', hid, P['w_cmp2']) + P['b_cmp2'][None, None, :, None, :]


def nsa_compressed(q, qpos, kv_c, slopes):
    n_cmp = kv_c.shape[1]
    blk = jnp.arange(n_cmp)
    end = blk * CMP_STRIDE + CMP_LEN - 1
    center = blk.astype(F32) * CMP_STRIDE + 0.5 * (CMP_LEN - 1)
    s = jnp.einsum('btghd,bcgd->btghc', q, kv_c[:, :, 0], preferred_element_type=F32) * ATTN_SCALE
    dist = qpos[:, None].astype(F32) - center[None]
    s = s - slopes[None, None, :, :, None] * dist[None, :, None, None, :]
    p = masked_softmax(s, (end[None] <= qpos[:, None])[None, :, None, None, :])
    o = jnp.einsum('btghc,bcgd->btghd', p.astype(kv_c.dtype), kv_c[:, :, 1])
    return o, jnp.sum(p, axis=3)


def nsa_select_blocks(imp, qpos, L):
    n_cmp = imp.shape[-1]
    n_sel = -(-L // SEL_BLOCK)
    ci = jnp.arange(n_cmp)[:, None] * CMP_STRIDE
    sj = jnp.arange(n_sel)
    overlap = ((ci < (sj[None] + 1) * SEL_BLOCK) & (ci + CMP_LEN > sj[None] * SEL_BLOCK)).astype(F32)
    score = jnp.einsum('btgc,cj->btgj', imp, overlap)
    lag = (qpos // SEL_BLOCK)[:, None] - sj[None]
    forced = (sj[None] == 0) | ((lag >= 0) & (lag < N_LOCAL_SEL))
    score = jnp.where(forced[:, None], FORCE_SCORE, jnp.where((lag >= 0)[:, None], score, -FORCE_SCORE))
    return lax.top_k(score, min(N_SEL, n_sel))[1]


def nsa_selected(q, qpos, kv_s, idx, slopes):
    B, T = q.shape[:2]
    L = kv_s.shape[1]
    n_blk = -(-L // SEL_BLOCK)
    kv_s = jnp.pad(kv_s, ((0, 0), (0, n_blk * SEL_BLOCK - L), (0, 0), (0, 0), (0, 0)))
    kvb = kv_s.reshape(B, n_blk, SEL_BLOCK, 2, G_NSA, HEAD_DIM).transpose(0, 3, 4, 1, 2, 5)
    qb = Q_BLOCK if T % Q_BLOCK == 0 else T
    nqb = T // qb
    n_k = idx.shape[-1]
    garr = jnp.arange(G_NSA)[None, :, None]
    offs = jnp.arange(SEL_BLOCK)

    def one(args):
        qq, ii, pp, b = args
        kvq = kvb[b]
        kg = kvq[0][garr, ii]
        vg = kvq[1][garr, ii]
        kpos = ii[..., None] * SEL_BLOCK + offs
        s = jnp.einsum('qghd,qgksd->qghks', qq, kg, preferred_element_type=F32) * ATTN_SCALE
        dist = (pp[:, None, None, None] - kpos).astype(F32)
        s = s - slopes[None, :, :, None, None] * dist[:, :, None]
        mask = jnp.broadcast_to((kpos <= pp[:, None, None, None])[:, :, None], s.shape)
        p = masked_softmax(s.reshape(qb, G_NSA, HPG_NSA, -1), mask.reshape(qb, G_NSA, HPG_NSA, -1))
        p = p.reshape(qb, G_NSA, HPG_NSA, n_k, SEL_BLOCK)
        return jnp.einsum('qghks,qgksd->qghd', p.astype(vg.dtype), vg)

    xs = (q.reshape(B * nqb, qb, G_NSA, HPG_NSA, HEAD_DIM),
          idx.reshape(B * nqb, qb, G_NSA, n_k),
          jnp.tile(qpos.reshape(nqb, qb), (B, 1)),
          jnp.repeat(jnp.arange(B), nqb))
    return lax.map(one, xs).reshape(B, T, G_NSA, HPG_NSA, HEAD_DIM)


def nsa_cmp_sel(q, qpos, kv_cs, P, slopes):
    L = kv_cs.shape[1]
    kv_c = compress_blocks(kv_cs[:, :, 0], P)
    o_cmp, imp = nsa_compressed(q, qpos, kv_c, slopes)
    idx = nsa_select_blocks(imp, qpos, L)
    o_sel = nsa_selected(q, qpos, kv_cs[:, :, 1], idx, slopes)
    return o_cmp, o_sel


def nsa_window(q, qpos, kv, kpos, slopes):
    s = jnp.einsum('bnqghd,bnkgd->bnghqk', q, kv[:, :, :, 0], preferred_element_type=F32) * ATTN_SCALE
    dist = qpos[:, :, None] - kpos[:, None, :]
    mask = (dist >= 0) & (dist < WINDOW) & (kpos[:, None, :] >= 0)
    s = s - slopes[None, None, :, :, None, None] * dist[None, :, None, None].astype(F32)
    p = masked_softmax(s, mask[None, :, None, None])
    return jnp.einsum('bnghqk,bnkgd->bnqghd', p.astype(kv.dtype), kv[:, :, :, 1])


def window_prompt(q, kv_w, slopes):
    B, T = q.shape[:2]
    nqb = T // Q_BLOCK
    nprev = WINDOW // Q_BLOCK
    kvp = jnp.pad(kv_w, ((0, 0), (WINDOW, 0), (0, 0), (0, 0), (0, 0)))
    kvp = kvp.reshape(B, nqb + nprev, Q_BLOCK, 2, G_NSA, HEAD_DIM)
    band = jnp.concatenate([kvp[:, i:i + nqb] for i in range(nprev + 1)], axis=2)
    start = jnp.arange(nqb)[:, None] * Q_BLOCK
    kpos = start - WINDOW + jnp.arange((nprev + 1) * Q_BLOCK)[None]
    qpos = start + jnp.arange(Q_BLOCK)[None]
    o = nsa_window(q.reshape(B, nqb, Q_BLOCK, G_NSA, HPG_NSA, HEAD_DIM), qpos, band, kpos, slopes)
    return o.reshape(B, T, G_NSA, HPG_NSA, HEAD_DIM)


def fox_attend(q, qpos, cq, k, v, kpos, ck):
    B, Tq = q.shape[:2]
    Tk = k.shape[1]
    s = jnp.einsum('bqghd,bkgd->bghqk', q, k, preferred_element_type=F32) * ATTN_SCALE
    cqh = cq.reshape(B, Tq, KV_FOX, HPG_FOX).transpose(0, 2, 3, 1)
    ckh = ck.reshape(B, Tk, KV_FOX, HPG_FOX).transpose(0, 2, 3, 1)
    s = s + cqh[..., :, None] - ckh[..., None, :]
    p = masked_softmax(s, kpos[None, :] <= qpos[:, None])
    return jnp.einsum('bghqk,bkgd->bqghd', p.astype(v.dtype), v)


def fox_prompt(q, kv, logf):
    B, T = q.shape[:2]
    C = jnp.cumsum(logf, axis=1)
    nqb = T // Q_BLOCK
    qb = jnp.swapaxes(q.reshape(B, nqb, Q_BLOCK, KV_FOX, HPG_FOX, HEAD_DIM), 0, 1)
    cb = jnp.swapaxes(C.reshape(B, nqb, Q_BLOCK, H_FOX), 0, 1)
    pb = jnp.arange(T).reshape(nqb, Q_BLOCK)
    kpos = jnp.arange(T)
    k, v = kv[:, :, 0], kv[:, :, 1]
    o = lax.map(lambda a: fox_attend(a[0], a[1], a[2], k, v, kpos, C), (qb, pb, cb))
    return jnp.swapaxes(o, 0, 1).reshape(B, T, KV_FOX, HPG_FOX, HEAD_DIM)


def merge_heads(o_cmp, o_sel, o_win, gates, o_fox, P):
    B, T = gates.shape[:2]
    g = gates.astype(o_cmp.dtype)[..., None]
    shp = (B, T, H_NSA, HEAD_DIM)
    o_nsa = (g[:, :, 0] * o_cmp.reshape(shp) + g[:, :, 1] * o_sel.reshape(shp)
             + g[:, :, 2] * o_win.reshape(shp)).reshape(B, T, W_NSA)
    y = jnp.concatenate([rmsnorm(o_nsa, P['g_grp'][:W_NSA]),
                         rmsnorm(o_fox.reshape(B, T, W_FOX), P['g_grp'][W_NSA:])], axis=-1)
    return jnp.einsum('btm,md->btd', y, P['w_out'])


def mixer_prompt(h, P, slopes):
    T = h.shape[1]
    q_n, kv_n, gates, q_f, kv_f, logf = project(h, P)
    kv_cs = kv_n[:, :, :2]
    o_cmp, o_sel = nsa_cmp_sel(q_n, jnp.arange(T), kv_cs, P, slopes)
    o_win = window_prompt(q_n, kv_n[:, :, 2], slopes)
    o_fox = fox_prompt(q_f, kv_f, logf)
    y = merge_heads(o_cmp, o_sel, o_win, gates, o_fox, P)
    return y, (kv_cs, kv_n[:, T - min(WINDOW, T):, 2], kv_f, logf)


def mixer_sample(h, cache_nsa, win_buf, cache_fox, cache_logf, page_table, P, slopes):
    T = h.shape[1]
    past = page_table.shape[1] * cache_nsa.shape[1]
    q_n, kv_n, gates, q_f, kv_f, logf = project(h, P)
    qpos = past + jnp.arange(T)
    kv_cs = jnp.concatenate([gather_pages(cache_nsa, page_table), kv_n[:, :, :2].astype(cache_nsa.dtype)], axis=1)
    o_cmp, o_sel = nsa_cmp_sel(q_n, qpos, kv_cs, P, slopes)
    wb = win_buf.shape[1]
    kw = jnp.concatenate([win_buf, kv_n[:, :, 2].astype(win_buf.dtype)], axis=1)
    kpos_w = past - wb + jnp.arange(wb + T)
    o_win = nsa_window(q_n[:, None], qpos[None], kw[:, None], kpos_w[None], slopes)[:, 0]
    kv_fall = jnp.concatenate([gather_pages(cache_fox, page_table), kv_f.astype(cache_fox.dtype)], axis=1)
    C = jnp.cumsum(jnp.concatenate([gather_pages(cache_logf, page_table).astype(F32), logf], axis=1), axis=1)
    o_fox = fox_attend(q_f, qpos, C[:, past:], kv_fall[:, :, 0], kv_fall[:, :, 1], jnp.arange(past + T), C)
    y = merge_heads(o_cmp, o_sel, o_win, gates, o_fox, P)
    return y, (kv_n[:, :, :2], kw[:, -wb:], kv_f, logf)


def expert_ffn(x, w_up, b_up, w_down, b_down):
    u = x @ w_up + b_up
    glu = jnp.minimum(u[:, :D_FF], SWIGLU_LIMIT)
    lin = jnp.clip(u[:, D_FF:], -SWIGLU_LIMIT, SWIGLU_LIMIT)
    a = glu * jax.nn.sigmoid(SWIGLU_ALPHA * glu) * (lin + 1.0)
    return a @ w_down + b_down


def moe(h, P):
    B, T, D = h.shape
    n = B * T
    xt = h.reshape(n, D)
    logits = (xt @ P['w_router'] + P['b_router']).astype(F32)
    top_v, top_e = lax.top_k(logits, TOP_K)
    gate = jax.nn.softmax(top_v, axis=-1)
    rows = min(MOE_BLOCK, max(8, (n * TOP_K) // N_EXPERTS))
    n_asg = n * TOP_K
    n_blk = -(-(n_asg + N_EXPERTS * (rows - 1)) // rows)
    e_flat = top_e.reshape(-1)
    order = jnp.argsort(e_flat)
    e_sorted = e_flat[order]
    counts = jax.ops.segment_sum(jnp.ones_like(e_flat), e_flat, num_segments=N_EXPERTS)
    padded = (counts + rows - 1) // rows * rows
    pad_end = jnp.cumsum(padded)
    pad_start = pad_end - padded
    grp_start = jnp.cumsum(counts) - counts
    dest = pad_start[e_sorted] + jnp.arange(n_asg) - grp_start[e_sorted]
    slot_tok = jnp.zeros((n_blk * rows,), jnp.int32).at[dest].set((order // TOP_K).astype(jnp.int32))
    slot_w = jnp.zeros((n_blk * rows,), F32).at[dest].set(gate.reshape(-1)[order])
    blk_e = jnp.minimum(jnp.searchsorted(pad_end, jnp.arange(n_blk) * rows, side='right'), N_EXPERTS - 1)

    def run(args):
        e, tok, w = args
        y = expert_ffn(xt[tok], P['w_up'][e], P['b_up'][e], P['w_down'][e], P['b_down'][e])
        return y * w[:, None].astype(y.dtype)

    ys = lax.map(run, (blk_e, slot_tok.reshape(n_blk, rows), slot_w.reshape(n_blk, rows)))
    out = jax.ops.segment_sum(ys.reshape(-1, D), slot_tok, num_segments=n)
    return out.reshape(B, T, D)


def sublayers(x, c, P, mixer):
    ada = jnp.einsum('bd,dn->bn', jax.nn.silu(c), P['w_ada']) + P['b_ada']
    sh_m, sc_m, gt_m, sh_f, sc_f, gt_f = jnp.split(ada[:, None, :], 6, axis=-1)
    h = rmsnorm(x, P['g_pre_mix']) * (1.0 + sc_m) + sh_m
    m, state = mixer(h)
    x = x + gt_m * rmsnorm(m, P['g_post_mix'])
    h = rmsnorm(x, P['g_pre_ffn']) * (1.0 + sc_f) + sh_f
    x = x + gt_f * rmsnorm(moe(h, P), P['g_post_ffn'])
    return x, state


def setup_inputs(seed: int = 0) -> dict:
    key = jax.random.key(seed)
    keys = iter(jax.random.split(key, 40))

    def nrm(shape, scale=1.0):
        return scale * jax.random.normal(next(keys), shape, F32)

    def gain(shape):
        return 1.0 + 0.1 * nrm(shape)

    n_pages = PAST_LEN // PAGE_SIZE
    n_used = DEC_BATCH * n_pages
    n_phys = n_used + max(1, n_used // 4)
    L = (DEPTH,)
    d = D_MODEL
    return {
        'x_prompt': nrm((BATCH, SEQ, d)),
        'x_sample': nrm((DEC_BATCH, DEC_SEQ, d)),
        'c_prompt': nrm((BATCH, d)),
        'c_sample': nrm((DEC_BATCH, d)),
        'cache_nsa_kv': nrm(L + (n_phys, PAGE_SIZE, 2, 2, G_NSA, HEAD_DIM)),
        'state_nsa_win': nrm(L + (DEC_BATCH, min(WINDOW, PAST_LEN), 2, G_NSA, HEAD_DIM)),
        'cache_fox_kv': nrm(L + (n_phys, PAGE_SIZE, 2, KV_FOX, HEAD_DIM)),
        'cache_fox_logf': jax.nn.log_sigmoid(3.0 + nrm(L + (n_phys, PAGE_SIZE, H_FOX))),
        'page_table': jax.random.permutation(next(keys), n_phys)[:n_used].reshape(DEC_BATCH, n_pages).astype(jnp.int32),
        'w_ada': nrm(L + (d, 6 * d), 0.5 * d ** -0.5),
        'b_ada': nrm(L + (6 * d,), 0.02),
        'g_pre_mix': gain(L + (d,)),
        'g_post_mix': gain(L + (d,)),
        'g_pre_ffn': gain(L + (d,)),
        'g_post_ffn': gain(L + (d,)),
        'w_in': nrm(L + (d, N_IN), d ** -0.5),
        'b_nsa_gate': nrm(L + (3 * H_NSA,), 0.1),
        'b_fox_forget': 1.0 + nrm(L + (H_FOX,), 0.1),
        'w_cmp1': nrm(L + (2, CMP_LEN, HEAD_DIM, CMP_HID), (CMP_LEN * HEAD_DIM) ** -0.5),
        'b_cmp1': nrm(L + (2, CMP_HID), 0.02),
        'w_cmp2': nrm(L + (2, CMP_HID, HEAD_DIM), CMP_HID ** -0.5),
        'b_cmp2': nrm(L + (2, HEAD_DIM), 0.02),
        'pe_cmp': nrm(L + (2, CMP_LEN, HEAD_DIM), 0.1),
        'g_grp': gain(L + (MIX_WIDTH,)),
        'w_out': nrm(L + (MIX_WIDTH, d), MIX_WIDTH ** -0.5),
        'w_router': nrm(L + (d, N_EXPERTS), d ** -0.5),
        'b_router': nrm(L + (N_EXPERTS,), 0.01),
        'w_up': nrm(L + (N_EXPERTS, d, 2 * D_FF), d ** -0.5),
        'b_up': nrm(L + (N_EXPERTS, 2 * D_FF), 0.01),
        'w_down': nrm(L + (N_EXPERTS, D_FF, d), D_FF ** -0.5),
        'b_down': nrm(L + (N_EXPERTS, d), 0.01),
    }


def reference(x_prompt, x_sample, c_prompt, c_sample, cache_nsa_kv, state_nsa_win, cache_fox_kv,
              cache_fox_logf, page_table, w_ada, b_ada, g_pre_mix, g_post_mix, g_pre_ffn, g_post_ffn,
              w_in, b_nsa_gate, b_fox_forget, w_cmp1, b_cmp1, w_cmp2, b_cmp2, pe_cmp, g_grp, w_out,
              w_router, b_router, w_up, b_up, w_down, b_down):
    slopes = alibi_slopes()
    y_p, y_s = x_prompt, x_sample
    st_p, st_s = [], []
    for l in range(DEPTH):
        P = {'w_ada': w_ada[l], 'b_ada': b_ada[l], 'g_pre_mix': g_pre_mix[l], 'g_post_mix': g_post_mix[l],
             'g_pre_ffn': g_pre_ffn[l], 'g_post_ffn': g_post_ffn[l], 'w_in': w_in[l],
             'b_nsa_gate': b_nsa_gate[l], 'b_fox_forget': b_fox_forget[l], 'w_cmp1': w_cmp1[l],
             'b_cmp1': b_cmp1[l], 'w_cmp2': w_cmp2[l], 'b_cmp2': b_cmp2[l], 'pe_cmp': pe_cmp[l],
             'g_grp': g_grp[l], 'w_out': w_out[l], 'w_router': w_router[l], 'b_router': b_router[l],
             'w_up': w_up[l], 'b_up': b_up[l], 'w_down': w_down[l], 'b_down': b_down[l]}
        y_p, s_p = sublayers(y_p, c_prompt, P, lambda h: mixer_prompt(h, P, slopes))
        y_s, s_s = sublayers(y_s, c_sample, P, lambda h: mixer_sample(
            h, cache_nsa_kv[l], state_nsa_win[l], cache_fox_kv[l], cache_fox_logf[l], page_table, P, slopes))
        st_p.append(s_p)
        st_s.append(s_s)
    nsa_kv_p, win_p, fox_kv_p, fox_logf_p = [jnp.stack(a) for a in zip(*st_p)]
    nsa_kv_s, win_s, fox_kv_s, fox_logf_s = [jnp.stack(a) for a in zip(*st_s)]
    return (y_p, y_s, nsa_kv_p, win_p, fox_kv_p, fox_logf_p, nsa_kv_s, win_s, fox_kv_s, fox_logf_s)
```

```python
import functools

import jax
import jax.numpy as jnp
from jax import lax
from jax.experimental import pallas as pl
from jax.experimental.pallas import tpu as pltpu

F32 = jnp.float32
BF16 = jnp.bfloat16
I32 = jnp.int32

D_MODEL = 2048
HEAD_DIM = 128
H_NSA = 8
H_FOX = 8
G_NSA = 2
HPG_NSA = 4
KV_FOX = 4
HPG_FOX = 2
W_NSA = H_NSA * HEAD_DIM
W_FOX = H_FOX * HEAD_DIM
CMP_LEN = 32
CMP_STRIDE = 16
CMP_HID = 256
SEL_BLOCK = 64
N_SEL = 16
N_LOCAL_SEL = 2
WINDOW = 512
N_EXPERTS = 32
TOP_K = 4
D_FF = 2048
SWIGLU_ALPHA = 1.702
SWIGLU_LIMIT = 7.0
RMS_EPS = 1e-6
ATTN_SCALE = HEAD_DIM ** -0.5
FORCE_SCORE = 1e9
MASKED = -2e30
M_FLOOR = -1e30

LANES = 128
VMEM_LIMIT = 52 * 1024 * 1024

N_MAIN = 4608
PROJ_TN = 512
COL_KVCS, COL_WIN, COL_KVF, COL_QN, COL_QF = 0, 1024, 1536, 2560, 3584


def _cparams(sem, vmem=VMEM_LIMIT):
    return pltpu.CompilerParams(dimension_semantics=sem, vmem_limit_bytes=vmem)


def _rms(x, g):
    return x * lax.rsqrt(jnp.mean(x * x, axis=-1, keepdims=True) + RMS_EPS) * g


def _nt_dot(a, b):
    return lax.dot_general(a, b, (((1,), (1,)), ((), ())), preferred_element_type=F32)


def _ada_kernel(c_ref, w_ref, b_ref, o_ref):
    c = c_ref[...]
    a = (c * jax.nn.sigmoid(c)).astype(BF16)
    o_ref[...] = jnp.dot(a, w_ref[...].astype(BF16), preferred_element_type=F32) + b_ref[...]


def ada_call(c, w, b):
    bc, d = c.shape
    n = w.shape[1]
    tn = 1024
    return pl.pallas_call(
        _ada_kernel,
        grid=(n // tn,),
        in_specs=[pl.BlockSpec((bc, d), lambda j: (0, 0)),
                  pl.BlockSpec((d, tn), lambda j: (0, j)),
                  pl.BlockSpec((1, tn), lambda j: (0, j))],
        out_specs=pl.BlockSpec((bc, tn), lambda j: (0, j)),
        out_shape=jax.ShapeDtypeStruct((bc, n), F32),
        compiler_params=_cparams(("arbitrary",)),
        name="ada",
    )(c, w, b)


def _inproj_kernel(x_ref, g_ref, sc_ref, sh_ref, wm_ref, ws_ref, bs_ref,
                   pb_ref, kvcs_ref, win_ref, kvf_ref, small_ref, h_scr):
    j = pl.program_id(2)

    @pl.when(j == 0)
    def _():
        h = _rms(x_ref[0], g_ref[...]) * (1.0 + sc_ref[0]) + sh_ref[0]
        hb = h.astype(BF16)
        h_scr[...] = hb
        z = jnp.dot(hb, ws_ref[...], preferred_element_type=F32) + bs_ref[...]
        lane = lax.broadcasted_iota(I32, z.shape, 1)
        small_ref[0] = jnp.where(lane < 3 * H_NSA, jax.nn.sigmoid(z), jax.nn.log_sigmoid(z))

    r = jnp.dot(h_scr[...], wm_ref[...], preferred_element_type=F32)
    pb_ref[0] = r.astype(BF16)

    @pl.when(j < 2)
    def _():
        kvcs_ref[0] = r

    @pl.when(j == 2)
    def _():
        win_ref[0] = r

    @pl.when((j == 3) | (j == 4))
    def _():
        kvf_ref[0] = r


def inproj_call(x, g, sc, sh, wm, ws, bs, tm):
    b, t, d = x.shape
    per_row = sc.shape[1] != 1
    nj = N_MAIN // PROJ_TN
    mod_spec = (pl.BlockSpec((1, tm, d), lambda bb, i, j: (bb, i, 0)) if per_row
                else pl.BlockSpec((1, 1, d), lambda bb, i, j: (bb, 0, 0)))
    return pl.pallas_call(
        _inproj_kernel,
        grid=(b, t // tm, nj),
        in_specs=[pl.BlockSpec((1, tm, d), lambda bb, i, j: (bb, i, 0)),
                  pl.BlockSpec((1, d), lambda bb, i, j: (0, 0)),
                  mod_spec, mod_spec,
                  pl.BlockSpec((d, PROJ_TN), lambda bb, i, j: (0, j)),
                  pl.BlockSpec((d, LANES), lambda bb, i, j: (0, 0)),
                  pl.BlockSpec((1, LANES), lambda bb, i, j: (0, 0))],
        out_specs=[pl.BlockSpec((1, tm, PROJ_TN), lambda bb, i, j: (bb, i, j)),
                   pl.BlockSpec((1, tm, PROJ_TN), lambda bb, i, j: (bb, i, jnp.minimum(j, 1))),
                   pl.BlockSpec((1, tm, PROJ_TN), lambda bb, i, j: (bb, i, 0)),
                   pl.BlockSpec((1, tm, PROJ_TN), lambda bb, i, j: (bb, i, jnp.clip(j - 3, 0, 1))),
                   pl.BlockSpec((1, tm, LANES), lambda bb, i, j: (bb, i, 0))],
        out_shape=[jax.ShapeDtypeStruct((b, t, N_MAIN), BF16),
                   jax.ShapeDtypeStruct((b, t, 1024), F32),
                   jax.ShapeDtypeStruct((b, t, 512), F32),
                   jax.ShapeDtypeStruct((b, t, 1024), F32),
                   jax.ShapeDtypeStruct((b, t, LANES), F32)],
        scratch_shapes=[pltpu.VMEM((tm, d), BF16)],
        compiler_params=_cparams(("arbitrary", "arbitrary", "arbitrary")),
        name="inproj",
    )(x, g, sc, sh, wm, ws, bs)


def _lane_cumsum(x):
    lane = lax.broadcasted_iota(I32, x.shape, 1)
    d = 1
    while d < LANES:
        x = x + jnp.where(lane >= d, pltpu.roll(x, d, axis=1), 0.0)
        d *= 2
    return x


def _cumsum_kernel(x_ref, o_ref):
    t = x_ref.shape[2]
    carry = jnp.zeros((x_ref.shape[1], 1), F32)
    for c in range(t // LANES):
        sl = slice(c * LANES, (c + 1) * LANES)
        y = _lane_cumsum(x_ref[0, :, sl]) + carry
        o_ref[0, :, sl] = y
        carry = y[:, LANES - 1:LANES]


def cumsum_call(x):
    b, h, t = x.shape
    return pl.pallas_call(
        _cumsum_kernel,
        grid=(b,),
        in_specs=[pl.BlockSpec((1, h, t), lambda bb: (bb, 0, 0))],
        out_specs=pl.BlockSpec((1, h, t), lambda bb: (bb, 0, 0)),
        out_shape=jax.ShapeDtypeStruct((b, h, t), F32),
        compiler_params=_cparams(("arbitrary",)),
        name="logf_cumsum",
    )(x)


def _nsa_slope(g, h):
    return jnp.where(g == 0, 2.0 ** -(h + 1), 2.0 ** -(HPG_NSA + h + 1)).astype(F32)


def _flash_kernel(*refs, mode, hpg, tq, tk, nk):
    if mode == "fox":
        q_ref, k_ref, v_ref, cq_ref, ck_ref, o_ref, m_scr, l_scr, acc_scr = refs
    elif mode == "sel":
        q_ref, k_ref, v_ref, sm_ref, o_ref, m_scr, l_scr, acc_scr = refs
    else:
        q_ref, k_ref, v_ref, o_ref, m_scr, l_scr, acc_scr = refs
    g = pl.program_id(1)
    i = pl.program_id(2)
    kk = pl.program_id(3)

    @pl.when(kk == 0)
    def _():
        m_scr[...] = jnp.full(m_scr.shape, M_FLOOR, F32)
        l_scr[...] = jnp.zeros(l_scr.shape, F32)
        acc_scr[...] = jnp.zeros(acc_scr.shape, F32)

    q0 = i * tq
    if mode == "win":
        kb = (q0 - WINDOW) // tk + kk
        valid = kb >= 0
    else:
        kb = kk
        valid = kk * tk <= q0 + tq - 1

    @pl.when(valid)
    def _():
        k = k_ref[0]
        v = v_ref[0]
        k0 = kb * tk
        dist = (q0 + lax.broadcasted_iota(I32, (tq, tk), 0)) - (k0 + lax.broadcasted_iota(I32, (tq, tk), 1))
        if mode == "win":
            mask = lax.bitcast_convert_type(dist, jnp.uint32) < WINDOW
        elif mode == "fox":
            mask = dist >= 0
        else:
            blk = (k0 + lax.broadcasted_iota(I32, (sm_ref.shape[3], tk), 1)) // SEL_BLOCK
            expand = jnp.where(blk == lax.broadcasted_iota(I32, blk.shape, 0), 1.0, 0.0).astype(BF16)
            chosen = jnp.dot(sm_ref[0, 0], expand, preferred_element_type=F32)
            mask = jnp.where(dist >= 0, chosen, 0.0) > 0.5
        krel = (k0 - q0 + lax.broadcasted_iota(I32, (1, tk), 1)).astype(F32)
        for h in range(hpg):
            q = q_ref[0, :, h * HEAD_DIM:(h + 1) * HEAD_DIM]
            if mode == "fox":
                bias = cq_ref[0, 0, h:h + 1, 0:1] - ck_ref[0, 0, h:h + 1, :]
            else:
                bias = _nsa_slope(g, h) * krel
            s = _nt_dot(q, k) * ATTN_SCALE + bias
            s = jnp.where(mask, s, MASKED)
            m_prev = m_scr[h]
            m_new = jnp.maximum(m_prev, jnp.max(s, axis=1, keepdims=True))
            alpha = jnp.exp(m_prev - m_new)
            p = jnp.exp(s - m_new)
            l_scr[h] = alpha * l_scr[h] + jnp.sum(p, axis=1, keepdims=True)
            acc_scr[h] = alpha * acc_scr[h] + jnp.dot(p.astype(BF16), v, preferred_element_type=F32)
            m_scr[h] = m_new

    @pl.when(kk == nk - 1)
    def _():
        for h in range(hpg):
            o = acc_scr[h] / jnp.maximum(l_scr[h], 1e-30)
            o_ref[0, :, h * HEAD_DIM:(h + 1) * HEAD_DIM] = o.astype(o_ref.dtype)


def flash_call(mode, pb, *, tq, tk, extra=()):
    b, t, _ = pb.shape
    nq = t // tq
    if mode == "fox":
        hpg, ngrp = HPG_FOX, KV_FOX
        qcol, kcol, vcol = COL_QF // (hpg * HEAD_DIM), COL_KVF // HEAD_DIM, COL_KVF // HEAD_DIM + KV_FOX
    else:
        hpg, ngrp = HPG_NSA, G_NSA
        base = (COL_KVCS + 512) if mode == "sel" else COL_WIN
        qcol, kcol, vcol = COL_QN // (hpg * HEAD_DIM), base // HEAD_DIM, base // HEAD_DIM + G_NSA
    if mode == "win":
        assert WINDOW % tk == 0 and tq % tk == 0
        nk = (WINDOW + tq) // tk

        def kv_blk(i, kk):
            return jnp.maximum((i * tq - WINDOW) // tk + kk, 0)
    else:
        nk = t // tk

        def kv_blk(i, kk):
            return jnp.minimum(kk, (i * tq + tq - 1) // tk)

    in_specs = [pl.BlockSpec((1, tq, hpg * HEAD_DIM), lambda bb, g, i, kk: (bb, i, qcol + g)),
                pl.BlockSpec((1, tk, HEAD_DIM), lambda bb, g, i, kk: (bb, kv_blk(i, kk), kcol + g)),
                pl.BlockSpec((1, tk, HEAD_DIM), lambda bb, g, i, kk: (bb, kv_blk(i, kk), vcol + g))]
    args = [pb, pb, pb]
    if mode == "fox":
        c8 = extra[0]
        in_specs += [pl.BlockSpec((1, 1, 8, tq), lambda bb, g, i, kk: (bb, g, 0, i)),
                     pl.BlockSpec((1, 1, 8, tk), lambda bb, g, i, kk: (bb, g, 0, kv_blk(i, kk)))]
        args += [c8, c8]
    elif mode == "sel":
        sm = extra[0]
        in_specs += [pl.BlockSpec((1, 1, tq, sm.shape[3]), lambda bb, g, i, kk: (bb, g, i, 0))]
        args += [sm]
    return pl.pallas_call(
        functools.partial(_flash_kernel, mode=mode, hpg=hpg, tq=tq, tk=tk, nk=nk),
        grid=(b, ngrp, nq, nk),
        in_specs=in_specs,
        out_specs=pl.BlockSpec((1, tq, hpg * HEAD_DIM), lambda bb, g, i, kk: (bb, i, g)),
        out_shape=jax.ShapeDtypeStruct((b, t, ngrp * hpg * HEAD_DIM), BF16),
        scratch_shapes=[pltpu.VMEM((hpg, tq, 1), F32), pltpu.VMEM((hpg, tq, 1), F32),
                        pltpu.VMEM((hpg, tq, HEAD_DIM), F32)],
        compiler_params=_cparams(("arbitrary",) * 4),
        name="flash_" + mode,
    )(*args)


def _compress_rows(x_ref, n, w1_ref, pe_ref, b1_ref, w2_ref, b2_ref):
    outs = []
    for j in range(2):
        first = jnp.zeros((2 * n, CMP_HID), F32)
        second = jnp.zeros((2 * n, CMP_HID), F32)
        for sp in range(CMP_STRIDE // 2):
            def rows(off):
                parts = []
                for g in range(G_NSA):
                    cb = j * G_NSA + g
                    halves = [x_ref[cb, pl.ds(2 * sp + u, n, stride=CMP_STRIDE), :]
                              + pe_ref[j, off + 2 * sp + u:off + 2 * sp + u + 1, :] for u in range(2)]
                    parts.append(jnp.concatenate(halves, axis=1))
                return jnp.concatenate(parts, axis=0).astype(BF16)
            first = first + jnp.dot(rows(0), w1_ref[j, sp], preferred_element_type=F32)
            second = second + jnp.dot(rows(CMP_STRIDE), w1_ref[j, CMP_STRIDE // 2 + sp],
                                      preferred_element_type=F32)
        for g in range(G_NSA):
            f = first[g * n:(g + 1) * n]
            s = pltpu.roll(second[g * n:(g + 1) * n], n - 1, axis=0)
            hid = jax.nn.gelu(f + s + b1_ref[j:j + 1, :])
            outs.append(jnp.dot(hid.astype(BF16), w2_ref[j], preferred_element_type=F32) + b2_ref[j:j + 1, :])
    return jnp.concatenate(outs, axis=1)


def _compress_prompt_kernel(x_ref, w1_ref, pe_ref, b1_ref, w2_ref, b2_ref, o_ref, x_scr):
    n = o_ref.shape[1]
    for cb in range(4):
        x_scr[cb] = x_ref[0, :, cb * HEAD_DIM:(cb + 1) * HEAD_DIM]
    o_ref[0] = _compress_rows(x_scr, n, w1_ref, pe_ref, b1_ref, w2_ref, b2_ref).astype(o_ref.dtype)


def _cmp_weight_specs():
    def const(shape):
        return pl.BlockSpec(shape, lambda *a: (0,) * len(shape))
    return [const((2, CMP_STRIDE, 2 * HEAD_DIM, CMP_HID)), const((2, CMP_LEN, HEAD_DIM)),
            const((2, CMP_HID)), const((2, CMP_HID, HEAD_DIM)), const((2, HEAD_DIM))]


def compress_prompt_call(kvcs, w1p, pe, b1, w2, b2):
    b, t, _ = kvcs.shape
    n = t // CMP_STRIDE
    return pl.pallas_call(
        _compress_prompt_kernel,
        grid=(b,),
        in_specs=[pl.BlockSpec((1, t, 512), lambda bb: (bb, 0, 0))] + _cmp_weight_specs(),
        out_specs=pl.BlockSpec((1, n, 512), lambda bb: (bb, 0, 0)),
        out_shape=jax.ShapeDtypeStruct((b, n, 512), BF16),
        scratch_shapes=[pltpu.VMEM((4, t, HEAD_DIM), F32)],
        compiler_params=_cparams(("arbitrary",)),
        name="compress_prompt",
    )(kvcs, w1p, pe, b1, w2, b2)


def _rank_select(score, n_keep):
    ns = score.shape[1]
    lane = lax.broadcasted_iota(I32, score.shape, 1)
    rank = jnp.zeros(score.shape, F32)
    for c in range(ns):
        col = score[:, c:c + 1]
        rank = rank + jnp.where(lane > c, jnp.where(col >= score, 1.0, 0.0), jnp.where(col > score, 1.0, 0.0))
    return jnp.where(rank < n_keep, 1.0, 0.0)


def _split_dot(a, b_bf16):
    hi = a.astype(BF16)
    r1 = a - hi.astype(F32)
    mid = r1.astype(BF16)
    lo = (r1 - mid.astype(F32)).astype(BF16)
    return (jnp.dot(hi, b_bf16, preferred_element_type=F32) + jnp.dot(mid, b_bf16, preferred_element_type=F32)
            + jnp.dot(lo, b_bf16, preferred_element_type=F32))


def _overlap_matrix(nc, ns):
    ci = lax.broadcasted_iota(I32, (nc, ns), 0) * CMP_STRIDE
    sj = lax.broadcasted_iota(I32, (nc, ns), 1)
    return jnp.where(ci < (sj + 1) * SEL_BLOCK, jnp.where(ci + CMP_LEN > sj * SEL_BLOCK, 1.0, 0.0), 0.0).astype(BF16)


def _force_scores(score, blk, lag):
    recent = lax.bitcast_convert_type(lag, jnp.uint32) < N_LOCAL_SEL
    score = jnp.where(recent, FORCE_SCORE, jnp.where(lag >= 0, score, -FORCE_SCORE))
    return jnp.where(blk == 0, FORCE_SCORE, score)


def _cmpsel_prompt_kernel(q_ref, k_ref, v_ref, o_ref, sm_ref, *, tq, n_cmp, n_sel, n_keep):
    g = pl.program_id(1)
    i = pl.program_id(2)
    nc = k_ref.shape[1]
    k = k_ref[0]
    v = v_ref[0]
    qpos = i * tq + lax.broadcasted_iota(I32, (tq, nc), 0)
    cidx = lax.broadcasted_iota(I32, (tq, nc), 1)
    mask = jnp.where(cidx < n_cmp, cidx * CMP_STRIDE + CMP_LEN - 1, 2 ** 30) <= qpos
    center = (lax.broadcasted_iota(I32, (1, nc), 1) * CMP_STRIDE - i * tq).astype(F32) + 0.5 * (CMP_LEN - 1)
    imp = jnp.zeros((tq, nc), F32)
    for h in range(HPG_NSA):
        q = q_ref[0, :, h * HEAD_DIM:(h + 1) * HEAD_DIM]
        s = _nt_dot(q, k) * ATTN_SCALE + _nsa_slope(g, h) * center
        s = jnp.where(mask, s, -1e30)
        e = jnp.where(mask, jnp.exp(s - jnp.max(s, axis=1, keepdims=True)), 0.0)
        p = e / jnp.maximum(jnp.sum(e, axis=1, keepdims=True), 1e-30)
        o_ref[0, :, h * HEAD_DIM:(h + 1) * HEAD_DIM] = jnp.dot(
            p.astype(BF16), v, preferred_element_type=F32).astype(o_ref.dtype)
        imp = imp + p
    score = _split_dot(imp, _overlap_matrix(nc, n_sel))
    blk = lax.broadcasted_iota(I32, (tq, n_sel), 1)
    lag = (i * tq + lax.broadcasted_iota(I32, (tq, n_sel), 0)) // SEL_BLOCK - blk
    sm_ref[0, 0] = _rank_select(_force_scores(score, blk, lag), n_keep).astype(sm_ref.dtype)


def cmpsel_prompt_call(pb, kvc, *, tq):
    b, t, _ = pb.shape
    nc = kvc.shape[1]
    n_cmp = nc - 1
    n_sel = -(-t // SEL_BLOCK)
    n_keep = min(N_SEL, n_sel)
    qcol = COL_QN // (HPG_NSA * HEAD_DIM)
    return pl.pallas_call(
        functools.partial(_cmpsel_prompt_kernel, tq=tq, n_cmp=n_cmp, n_sel=n_sel, n_keep=n_keep),
        grid=(b, G_NSA, t // tq),
        in_specs=[pl.BlockSpec((1, tq, HPG_NSA * HEAD_DIM), lambda bb, g, i: (bb, i, qcol + g)),
                  pl.BlockSpec((1, nc, HEAD_DIM), lambda bb, g, i: (bb, 0, g)),
                  pl.BlockSpec((1, nc, HEAD_DIM), lambda bb, g, i: (bb, 0, G_NSA + g))],
        out_specs=[pl.BlockSpec((1, tq, HPG_NSA * HEAD_DIM), lambda bb, g, i: (bb, i, g)),
                   pl.BlockSpec((1, 1, tq, n_sel), lambda bb, g, i: (bb, g, i, 0))],
        out_shape=[jax.ShapeDtypeStruct((b, t, W_NSA), BF16),
                   jax.ShapeDtypeStruct((b, G_NSA, t, n_sel), BF16)],
        compiler_params=_cparams(("arbitrary",) * 3),
        name="cmpsel_prompt",
    )(pb, kvc, kvc)


def _merge_kernel(ocmp_ref, osel_ref, owin_ref, ofox_ref, small_ref, x_ref, ggrp_ref, wout_ref, gpost_ref,
                  gt_ref, gpre_ref, sc_ref, sh_ref, wrh_ref, wrl_ref, br_ref, x1_ref, h2_ref, route_ref):
    gs = small_ref[0]
    parts = []
    for h in range(H_NSA):
        sl = slice(h * HEAD_DIM, (h + 1) * HEAD_DIM)
        parts.append(gs[:, h:h + 1] * ocmp_ref[0, :, sl].astype(F32)
                     + gs[:, H_NSA + h:H_NSA + h + 1] * osel_ref[0, :, sl].astype(F32)
                     + gs[:, 2 * H_NSA + h:2 * H_NSA + h + 1] * owin_ref[0, :, sl].astype(F32))
    o_nsa = jnp.concatenate(parts, axis=1)
    y = jnp.concatenate([_rms(o_nsa, ggrp_ref[:, :W_NSA]),
                         _rms(ofox_ref[0].astype(F32), ggrp_ref[:, W_NSA:])], axis=1).astype(BF16)
    m = jnp.dot(y, wout_ref[...], preferred_element_type=F32)
    x1 = x_ref[0] + gt_ref[0] * _rms(m, gpost_ref[...])
    x1_ref[0] = x1
    h2 = _rms(x1, gpre_ref[...]) * (1.0 + sc_ref[0]) + sh_ref[0]
    hi = h2.astype(BF16)
    h2_ref[0] = hi
    lo = (h2 - hi.astype(F32)).astype(BF16)
    logits = (jnp.dot(hi, wrh_ref[...], preferred_element_type=F32) + jnp.dot(hi, wrl_ref[...], preferred_element_type=F32)
              + jnp.dot(lo, wrh_ref[...], preferred_element_type=F32) + br_ref[...])
    lane = lax.broadcasted_iota(I32, logits.shape, 1)
    vals = jnp.where(lane < N_EXPERTS, logits, -jnp.inf)
    top_v, top_e = [], []
    for _ in range(TOP_K):
        mx = jnp.max(vals, axis=1, keepdims=True)
        idx = jnp.min(jnp.where(vals == mx, lane, LANES), axis=1, keepdims=True)
        top_v.append(mx)
        top_e.append(idx)
        vals = jnp.where(lane == idx, -jnp.inf, vals)
    ex = [jnp.exp(v - top_v[0]) for v in top_v]
    den = ex[0] + ex[1] + ex[2] + ex[3]
    route = jnp.zeros(logits.shape, F32)
    for kx in range(TOP_K):
        route = jnp.where(lane == kx, top_e[kx].astype(F32), route)
        route = jnp.where(lane == TOP_K + kx, ex[kx] / den, route)
    route_ref[0] = route


def merge_call(ocmp, osel, owin, ofox, small, x, ggrp, wout, gpost, gt, gpre, sc, sh, wrh, wrl, br, tm):
    b, t, d = x.shape
    per_row = sc.shape[1] != 1
    row = lambda w: pl.BlockSpec((1, tm, w), lambda bb, i: (bb, i, 0))
    const = lambda shape: pl.BlockSpec(shape, lambda bb, i: (0,) * len(shape))
    mod = row(d) if per_row else pl.BlockSpec((1, 1, d), lambda bb, i: (bb, 0, 0))
    return pl.pallas_call(
        _merge_kernel,
        grid=(b, t // tm),
        in_specs=[row(W_NSA), row(W_NSA), row(W_NSA), row(W_FOX), row(LANES), row(d),
                  const((1, d)), const((d, d)), const((1, d)), mod, const((1, d)), mod, mod,
                  const((d, LANES)), const((d, LANES)), const((1, LANES))],
        out_specs=[row(d), row(d), row(LANES)],
        out_shape=[jax.ShapeDtypeStruct((b, t, d), F32), jax.ShapeDtypeStruct((b, t, d), BF16),
                   jax.ShapeDtypeStruct((b, t, LANES), F32)],
        compiler_params=_cparams(("arbitrary", "arbitrary")),
        name="merge",
    )(ocmp, osel, owin, ofox, small, x, ggrp, wout, gpost, gt, gpre, sc, sh, wrh, wrl, br)


MOE_TM = 512
MOE_TF = 512
MOE_TN = 512


def _expert_changed(blk_e_ref, i):
    return (i == 0) | (blk_e_ref[i] != blk_e_ref[jnp.maximum(i - 1, 0)])


def _moe_up_kernel(blk_e_ref, nused_ref, x_ref, wg_ref, wl_ref, bg_ref, bl_ref, a_ref, wg_bf, wl_bf):
    i = pl.program_id(1)

    @pl.when(_expert_changed(blk_e_ref, i))
    def _():
        wg_bf[...] = wg_ref[0].astype(BF16)
        wl_bf[...] = wl_ref[0].astype(BF16)

    @pl.when(i < nused_ref[0])
    def _():
        x = x_ref[...]
        ug = jnp.dot(x, wg_bf[...], preferred_element_type=F32) + bg_ref[0]
        ul = jnp.dot(x, wl_bf[...], preferred_element_type=F32) + bl_ref[0]
        glu = jnp.minimum(ug, SWIGLU_LIMIT)
        lin = jnp.clip(ul, -SWIGLU_LIMIT, SWIGLU_LIMIT)
        a_ref[...] = (glu * jax.nn.sigmoid(SWIGLU_ALPHA * glu) * (lin + 1.0)).astype(a_ref.dtype)

    @pl.when(i >= nused_ref[0])
    def _():
        a_ref[...] = jnp.zeros(a_ref.shape, a_ref.dtype)


def moe_up_call(blk_e, n_used, xs, w_up, b_up):
    n_slots, d = xs.shape
    n_blk = n_slots // MOE_TM
    nf = D_FF // MOE_TF
    grid_spec = pltpu.PrefetchScalarGridSpec(
        num_scalar_prefetch=2,
        grid=(nf, n_blk),
        in_specs=[pl.BlockSpec((MOE_TM, d), lambda f, i, be, nu: (i, 0)),
                  pl.BlockSpec((1, d, MOE_TF), lambda f, i, be, nu: (be[i], 0, f)),
                  pl.BlockSpec((1, d, MOE_TF), lambda f, i, be, nu: (be[i], 0, nf + f)),
                  pl.BlockSpec((1, 1, MOE_TF), lambda f, i, be, nu: (be[i], 0, f)),
                  pl.BlockSpec((1, 1, MOE_TF), lambda f, i, be, nu: (be[i], 0, nf + f))],
        out_specs=pl.BlockSpec((MOE_TM, MOE_TF), lambda f, i, be, nu: (i, f)),
        scratch_shapes=[pltpu.VMEM((d, MOE_TF), BF16), pltpu.VMEM((d, MOE_TF), BF16)])
    return pl.pallas_call(
        _moe_up_kernel,
        grid_spec=grid_spec,
        out_shape=jax.ShapeDtypeStruct((n_slots, D_FF), BF16),
        compiler_params=_cparams(("arbitrary", "arbitrary")),
        name="moe_up",
    )(blk_e, n_used, xs, w_up, w_up, b_up, b_up)


def _moe_down_kernel(blk_e_ref, nused_ref, a_ref, wd_ref, bd_ref, sw_ref, y_ref, wd_bf):
    i = pl.program_id(1)

    @pl.when(_expert_changed(blk_e_ref, i))
    def _():
        wd_bf[...] = wd_ref[0].astype(BF16)

    @pl.when(i < nused_ref[0])
    def _():
        y = jnp.dot(a_ref[...], wd_bf[...], preferred_element_type=F32) + bd_ref[0]
        y_ref[...] = (y * sw_ref[...]).astype(y_ref.dtype)

    @pl.when(i >= nused_ref[0])
    def _():
        y_ref[...] = jnp.zeros(y_ref.shape, y_ref.dtype)


def moe_down_call(blk_e, n_used, a, w_down, b_down, slot_w):
    n_slots, dff = a.shape
    d = w_down.shape[2]
    n_blk = n_slots // MOE_TM
    grid_spec = pltpu.PrefetchScalarGridSpec(
        num_scalar_prefetch=2,
        grid=(d // MOE_TN, n_blk),
        in_specs=[pl.BlockSpec((MOE_TM, dff), lambda c, i, be, nu: (i, 0)),
                  pl.BlockSpec((1, dff, MOE_TN), lambda c, i, be, nu: (be[i], 0, c)),
                  pl.BlockSpec((1, 1, MOE_TN), lambda c, i, be, nu: (be[i], 0, c)),
                  pl.BlockSpec((MOE_TM, 1), lambda c, i, be, nu: (i, 0))],
        out_specs=pl.BlockSpec((MOE_TM, MOE_TN), lambda c, i, be, nu: (i, c)),
        scratch_shapes=[pltpu.VMEM((dff, MOE_TN), BF16)])
    return pl.pallas_call(
        _moe_down_kernel,
        grid_spec=grid_spec,
        out_shape=jax.ShapeDtypeStruct((n_slots, d), BF16),
        compiler_params=_cparams(("arbitrary", "arbitrary")),
        name="moe_down",
    )(blk_e, n_used, a, w_down, b_down, slot_w)


def moe_dispatch(route):
    n = route.shape[0]
    n_asg = n * TOP_K
    n_blk = -(-(n_asg + N_EXPERTS * (MOE_TM - 1)) // MOE_TM)
    e_flat = route[:, :TOP_K].astype(I32).reshape(-1)
    w_flat = route[:, TOP_K:2 * TOP_K].reshape(-1)
    order = jnp.argsort(e_flat).astype(I32)
    e_sorted = e_flat[order]
    counts = jnp.zeros((N_EXPERTS,), I32).at[e_flat].add(1)
    padded = (counts + MOE_TM - 1) // MOE_TM * MOE_TM
    pad_end = jnp.cumsum(padded)
    pad_start = pad_end - padded
    grp_start = jnp.cumsum(counts) - counts
    dest_sorted = pad_start[e_sorted] + jnp.arange(n_asg, dtype=I32) - grp_start[e_sorted]
    slot_tok = jnp.zeros((n_blk * MOE_TM,), I32).at[dest_sorted].set(order // TOP_K)
    slot_w = jnp.zeros((n_blk * MOE_TM,), F32).at[dest_sorted].set(w_flat[order])
    dest = jnp.zeros((n_asg,), I32).at[order].set(dest_sorted).reshape(n, TOP_K).T
    blk_e = jnp.minimum(jnp.searchsorted(pad_end, jnp.arange(n_blk, dtype=I32) * MOE_TM, side="right"),
                        N_EXPERTS - 1).astype(I32)
    n_used = (pad_end[-1:] // MOE_TM).astype(I32)
    return slot_tok, slot_w[:, None], dest, blk_e, n_used


def _final_kernel(yg_ref, x1_ref, gt_ref, gpost_ref, o_ref):
    s = yg_ref[0, 0].astype(F32)
    for kx in range(1, TOP_K):
        s = s + yg_ref[kx, 0].astype(F32)
    o_ref[0] = x1_ref[0] + gt_ref[0] * _rms(s, gpost_ref[...])


def final_call(yg, x1, gt, gpost, tm):
    b, t, d = x1.shape
    per_row = gt.shape[1] != 1
    mod = (pl.BlockSpec((1, tm, d), lambda bb, i: (bb, i, 0)) if per_row
           else pl.BlockSpec((1, 1, d), lambda bb, i: (bb, 0, 0)))
    return pl.pallas_call(
        _final_kernel,
        grid=(b, t // tm),
        in_specs=[pl.BlockSpec((TOP_K, 1, tm, d), lambda bb, i: (0, bb, i, 0)),
                  pl.BlockSpec((1, tm, d), lambda bb, i: (bb, i, 0)),
                  mod, pl.BlockSpec((1, d), lambda bb, i: (0, 0))],
        out_specs=pl.BlockSpec((1, tm, d), lambda bb, i: (bb, i, 0)),
        out_shape=jax.ShapeDtypeStruct((b, t, d), F32),
        compiler_params=_cparams(("arbitrary", "arbitrary")),
        name="final",
    )(yg, x1, gt, gpost)


def prep_weights(w_in, b_nsa_gate, b_fox_forget, w_cmp1, w_out, w_router, b_router):
    d = w_in.shape[0]
    o_qn, o_kvn, o_gn, o_qf, o_kvf, o_fl = 0, 1024, 2560, 2584, 3608, 4632
    wm = jnp.concatenate([w_in[:, o_kvn:o_kvn + 1536], w_in[:, o_kvf:o_kvf + 1024],
                          w_in[:, o_qn:o_qn + 1024], w_in[:, o_qf:o_qf + 1024]], axis=1).astype(BF16)
    pad = LANES - 3 * H_NSA - H_FOX
    ws = jnp.concatenate([w_in[:, o_gn:o_gn + 3 * H_NSA], w_in[:, o_fl:o_fl + H_FOX],
                          jnp.zeros((d, pad), F32)], axis=1).astype(BF16)
    bs = jnp.concatenate([b_nsa_gate, b_fox_forget, jnp.zeros((pad,), F32)])[None, :]
    w1p = w_cmp1.reshape(2, CMP_STRIDE, 2 * HEAD_DIM, CMP_HID).astype(BF16)
    wr = jnp.concatenate([w_router, jnp.zeros((d, LANES - N_EXPERTS), F32)], axis=1)
    wrh = wr.astype(BF16)
    wrl = (wr - wrh.astype(F32)).astype(BF16)
    br = jnp.concatenate([b_router, jnp.zeros((LANES - N_EXPERTS,), F32)])[None, :]
    return dict(wm=wm, ws=ws, bs=bs, w1p=w1p, wout=w_out.astype(BF16), wrh=wrh, wrl=wrl, br=br)


def prompt_mixer(x, sc, sh, g_pre, W, pe, b1, w2b, b2, *, tm, tq, tk):
    b, t, _ = x.shape
    pb, kvcs, win, kvf, small = inproj_call(x, g_pre, sc, sh, W["wm"], W["ws"], W["bs"], tm)
    kvc = compress_prompt_call(kvcs, W["w1p"], pe, b1, w2b, b2)
    ocmp, selmask = cmpsel_prompt_call(pb, kvc, tq=tq)
    osel = flash_call("sel", pb, tq=tq, tk=tk, extra=(selmask,))
    owin = flash_call("win", pb, tq=tq, tk=tk)
    logf = small[:, :, 3 * H_NSA:3 * H_NSA + H_FOX]
    cum = cumsum_call(jnp.swapaxes(logf, 1, 2))
    c8 = jnp.pad(cum.reshape(b, KV_FOX, HPG_FOX, t), ((0, 0), (0, 0), (0, 8 - HPG_FOX), (0, 0)))
    ofox = flash_call("fox", pb, tq=tq, tk=tk, extra=(c8,))
    return ocmp, osel, owin, ofox, small, kvcs, win, kvf, logf


PAGE = 128
CMP_PAGES = 16
FOX_PAGES = 8
NS_PAD = 384


def _head_rows(qf, g):
    rows = [qf[:, (g * HPG_NSA + h) * HEAD_DIM:(g * HPG_NSA + h + 1) * HEAD_DIM] for h in range(HPG_NSA)]
    return jnp.concatenate(rows + [jnp.zeros((8 - HPG_NSA, HEAD_DIM), F32)], axis=0).astype(BF16)


def _slope_col(g):
    row = lax.broadcasted_iota(I32, (8, 1), 0)
    col = jnp.zeros((8, 1), F32)
    for h in range(HPG_NSA):
        col = jnp.where(row == h, 2.0 ** -(g * HPG_NSA + h + 1), col)
    return col


def _masked_softmax(s, mask):
    s = jnp.where(mask, s, -1e30)
    e = jnp.where(mask, jnp.exp(s - jnp.max(s, axis=1, keepdims=True)), 0.0)
    return e / jnp.maximum(jnp.sum(e, axis=1, keepdims=True), 1e-30)


def _store_heads(o_ref, o, g):
    for h in range(HPG_NSA):
        c = (g * HPG_NSA + h) * HEAD_DIM
        o_ref[0, :, c:c + HEAD_DIM] = o[h:h + 1].astype(o_ref.dtype)


def _compress_sample_kernel(pt_ref, *refs, npg):
    pages = refs[:npg + 1]
    w1_ref, pe_ref, b1_ref, w2_ref, b2_ref, o_ref, x_scr = refs[npg + 1:]
    for p in range(npg + 1):
        for cb in range(4):
            x_scr[cb, p * PAGE:(p + 1) * PAGE, :] = pages[p][0, :, cb * HEAD_DIM:(cb + 1) * HEAD_DIM]
    n = (npg + 1) * PAGE // CMP_STRIDE
    r = _compress_rows(x_scr, n, w1_ref, pe_ref, b1_ref, w2_ref, b2_ref)
    o_ref[0] = r[:npg * PAGE // CMP_STRIDE].astype(o_ref.dtype)


def compress_sample_call(page_table, cache3, w1p, pe, b1, w2, b2):
    bsz, n_pages = page_table.shape
    npg = CMP_PAGES
    assert n_pages % npg == 0
    page_specs = [pl.BlockSpec((1, PAGE, 512),
                               lambda bb, gi, pt, p=p: (pt[bb, jnp.minimum(gi * npg + p, n_pages - 1)], 0, 0))
                  for p in range(npg + 1)]
    rows_out = npg * PAGE // CMP_STRIDE
    grid_spec = pltpu.PrefetchScalarGridSpec(
        num_scalar_prefetch=1,
        grid=(bsz, n_pages // npg),
        in_specs=page_specs + _cmp_weight_specs(),
        out_specs=pl.BlockSpec((1, rows_out, 512), lambda bb, gi, pt: (bb, gi, 0)),
        scratch_shapes=[pltpu.VMEM((4, (npg + 1) * PAGE, HEAD_DIM), F32)])
    return pl.pallas_call(
        functools.partial(_compress_sample_kernel, npg=npg),
        grid_spec=grid_spec,
        out_shape=jax.ShapeDtypeStruct((bsz, n_pages * PAGE // CMP_STRIDE, 512), BF16),
        compiler_params=_cparams(("arbitrary", "arbitrary")),
        name="compress_sample",
    )(page_table, *([cache3] * (npg + 1)), w1p, pe, b1, w2, b2)


def _cmpsel_sample_kernel(q_ref, kv_ref, o_ref, idx_ref, *, past, n_cmp, n_keep):
    nc = kv_ref.shape[1]
    qf = q_ref[0].astype(F32)
    cidx = lax.broadcasted_iota(I32, (1, nc), 1)
    mask = jnp.where(cidx < n_cmp, cidx * CMP_STRIDE + CMP_LEN - 1, 2 ** 30) <= past
    center = (cidx * CMP_STRIDE - past).astype(F32) + 0.5 * (CMP_LEN - 1)
    overlap = _overlap_matrix(nc, NS_PAD)
    blk = lax.broadcasted_iota(I32, (8, NS_PAD), 1)
    lag = past // SEL_BLOCK - blk
    ii =lax.broadcasted_iota(I32, (NS_PAD, NS_PAD), 0)
    jj = lax.broadcasted_iota(I32, (NS_PAD, NS_PAD), 1)
    slot = lax.broadcasted_iota(I32, (NS_PAD, LANES), 1).astype(F32)
    rows = []
    for g in range(G_NSA):
        k = kv_ref[0, :, g * HEAD_DIM:(g + 1) * HEAD_DIM]
        v = kv_ref[0, :, (G_NSA + g) * HEAD_DIM:(G_NSA + g + 1) * HEAD_DIM]
        s = _nt_dot(_head_rows(qf, g), k) * ATTN_SCALE + _slope_col(g) * center
        p = _masked_softmax(s, jnp.broadcast_to(mask, s.shape))
        _store_heads(o_ref, jnp.dot(p.astype(BF16), v, preferred_element_type=F32), g)
        imp = jnp.sum(p[0:HPG_NSA], axis=0, keepdims=True)
        score = _split_dot(jnp.broadcast_to(imp, (8, nc)), overlap)
        score = _force_scores(score, blk, lag)
        row = score[0:1, :]
        col = score.T[:, 0:1]
        ge = jnp.where(col >= row, 1.0, 0.0)
        gt = jnp.where(col > row, 1.0, 0.0)
        rank_row = jnp.sum(jnp.where(ii < jj, ge, gt), axis=0, keepdims=True)
        rank_col = jnp.sum(jnp.where(jj < ii, 1.0 - gt, 1.0 - ge), axis=1, keepdims=True)
        sel_row = jnp.where(rank_row < n_keep, 1.0, 0.0)
        sel_col = jnp.where(rank_col < n_keep, 1.0, 0.0)
        before = jnp.sum(jnp.where(jj < ii, sel_row, 0.0), axis=1, keepdims=True)
        pick = jnp.where(before == slot, sel_col * ii[:, 0:1].astype(F32), 0.0)
        rows.append(jnp.sum(pick, axis=0, keepdims=True))
    idx_ref[0] = jnp.concatenate(rows + [jnp.zeros((8 - G_NSA, LANES), F32)], axis=0).astype(I32)


def cmpsel_sample_call(qn, kvc, *, past):
    bsz = qn.shape[0]
    nc = kvc.shape[1]
    n_cmp = (past + 1) // CMP_STRIDE - 1
    n_sel = -(-(past + 1) // SEL_BLOCK)
    assert n_sel <= NS_PAD and n_cmp <= nc
    n_keep = min(N_SEL, n_sel)
    return pl.pallas_call(
        functools.partial(_cmpsel_sample_kernel, past=past, n_cmp=n_cmp, n_keep=n_keep),
        grid=(bsz,),
        in_specs=[pl.BlockSpec((1, 1, W_NSA), lambda bb: (bb, 0, 0)),
                  pl.BlockSpec((1, nc, 512), lambda bb: (bb, 0, 0))],
        out_specs=[pl.BlockSpec((1, 1, W_NSA), lambda bb: (bb, 0, 0)),
                   pl.BlockSpec((1, 8, LANES), lambda bb: (bb, 0, 0))],
        out_shape=[jax.ShapeDtypeStruct((bsz, 1, W_NSA), BF16), jax.ShapeDtypeStruct((bsz, 8, LANES), I32)],
        compiler_params=_cparams(("arbitrary",)),
        name="cmpsel_sample",
    )(qn, kvc)


def _sel_sample_kernel(idx_ref, pt_ref, q_ref, new_ref, cache_ref, o_ref, kbuf, vbuf, sem, *, past, n_pages, n_keep):
    bb = pl.program_id(0)
    per_page = PAGE // SEL_BLOCK
    copies = []
    for g in range(G_NSA):
        for kx in range(n_keep):
            j = idx_ref[(bb * G_NSA + g) * n_keep + kx]
            page = pt_ref[bb, jnp.minimum(j // per_page, n_pages - 1)]
            r0 = pl.multiple_of((j % per_page) * SEL_BLOCK, SEL_BLOCK)
            for buf, col, s in ((kbuf, 512 + g * HEAD_DIM, 0), (vbuf, 768 + g * HEAD_DIM, 1)):
                cp = pltpu.make_async_copy(cache_ref.at[page, pl.ds(r0, SEL_BLOCK), pl.ds(col, HEAD_DIM)],
                                           buf.at[g, kx], sem.at[s])
                cp.start()
                copies.append(cp)
    for cp in copies:
        cp.wait()
    qf = q_ref[0].astype(F32)
    nkeys = n_keep * SEL_BLOCK
    lane = lax.broadcasted_iota(I32, (1, nkeys), 1)
    row0 = lax.broadcasted_iota(I32, (SEL_BLOCK, HEAD_DIM), 0) == 0
    for g in range(G_NSA):
        kpos = lane % SEL_BLOCK
        ks, vs = [], []
        for kx in range(n_keep):
            j = idx_ref[(bb * G_NSA + g) * n_keep + kx]
            is_new = j * SEL_BLOCK >= past
            fresh = jnp.logical_and(is_new, row0)
            ks.append(jnp.where(fresh, new_ref[0, :, 512 + g * HEAD_DIM:512 + (g + 1) * HEAD_DIM], kbuf[g, kx]))
            vs.append(jnp.where(fresh, new_ref[0, :, 768 + g * HEAD_DIM:768 + (g + 1) * HEAD_DIM], vbuf[g, kx]))
            kpos = kpos + jnp.where(lane // SEL_BLOCK == kx, j * SEL_BLOCK, 0)
        k = jnp.concatenate(ks, axis=0).astype(BF16)
        v = jnp.concatenate(vs, axis=0).astype(BF16)
        s = _nt_dot(_head_rows(qf, g), k) * ATTN_SCALE + _slope_col(g) * (kpos - past).astype(F32)
        p = _masked_softmax(s, jnp.broadcast_to(kpos <= past, s.shape))
        _store_heads(o_ref, jnp.dot(p.astype(BF16), v, preferred_element_type=F32), g)


def sel_sample_call(idx_flat, page_table, qn, kvcs_new, cache3, *, past, n_keep):
    bsz, n_pages = page_table.shape
    assert past % SEL_BLOCK == 0 and past == n_pages * PAGE
    grid_spec = pltpu.PrefetchScalarGridSpec(
        num_scalar_prefetch=2,
        grid=(bsz,),
        in_specs=[pl.BlockSpec((1, 1, W_NSA), lambda bb, ix, pt: (bb, 0, 0)),
                  pl.BlockSpec((1, 1, 1024), lambda bb, ix, pt: (bb, 0, 0)),
                  pl.BlockSpec(memory_space=pl.ANY)],
        out_specs=pl.BlockSpec((1, 1, W_NSA), lambda bb, ix, pt: (bb, 0, 0)),
        scratch_shapes=[pltpu.VMEM((G_NSA, n_keep, SEL_BLOCK, HEAD_DIM), F32),
                        pltpu.VMEM((G_NSA, n_keep, SEL_BLOCK, HEAD_DIM), F32),
                        pltpu.SemaphoreType.DMA((2,))])
    return pl.pallas_call(
        functools.partial(_sel_sample_kernel, past=past, n_pages=n_pages, n_keep=n_keep),
        grid_spec=grid_spec,
        out_shape=jax.ShapeDtypeStruct((bsz, 1, W_NSA), BF16),
        compiler_params=_cparams(("arbitrary",)),
        name="sel_sample",
    )(idx_flat, page_table, qn, kvcs_new, cache3)


def _win_sample_kernel(q_ref, buf_ref, new_ref, o_ref, nb_ref):
    wb = buf_ref.shape[1]
    rowi = lax.broadcasted_iota(I32, (wb, 512), 0)
    nb = jnp.where(rowi == wb - 1, new_ref[0], pltpu.roll(buf_ref[0], wb - 1, axis=0))
    nb_ref[0] = nb
    qf = q_ref[0].astype(F32)
    krel = (lax.broadcasted_iota(I32, (1, wb), 1) - (wb - 1)).astype(F32)
    for g in range(G_NSA):
        k = nb[:, g * HEAD_DIM:(g + 1) * HEAD_DIM].astype(BF16)
        v = nb[:, (G_NSA + g) * HEAD_DIM:(G_NSA + g + 1) * HEAD_DIM].astype(BF16)
        s = _nt_dot(_head_rows(qf, g), k) * ATTN_SCALE + _slope_col(g) * krel
        p = _masked_softmax(s, jnp.full(s.shape, True))
        _store_heads(o_ref, jnp.dot(p.astype(BF16), v, preferred_element_type=F32), g)


def win_sample_call(qn, win_buf, win_new):
    bsz, wb, _ = win_buf.shape
    assert wb == WINDOW
    return pl.pallas_call(
        _win_sample_kernel,
        grid=(bsz,),
        in_specs=[pl.BlockSpec((1, 1, W_NSA), lambda bb: (bb, 0, 0)),
                  pl.BlockSpec((1, wb, 512), lambda bb: (bb, 0, 0)),
                  pl.BlockSpec((1, 1, 512), lambda bb: (bb, 0, 0))],
        out_specs=[pl.BlockSpec((1, 1, W_NSA), lambda bb: (bb, 0, 0)),
                   pl.BlockSpec((1, wb, 512), lambda bb: (bb, 0, 0))],
        out_shape=[jax.ShapeDtypeStruct((bsz, 1, W_NSA), BF16), jax.ShapeDtypeStruct((bsz, wb, 512), F32)],
        compiler_params=_cparams(("arbitrary",)),
        name="win_sample",
    )(qn, win_buf, win_new)


def _fox_sample_kernel(pt_ref, *refs, npg, n_steps):
    kv_pages = refs[:npg]
    lf_pages = refs[npg:2 * npg]
    q_ref, new_ref, lfn_ref, o_ref, qbd_scr, m_scr, l_scr, acc_scr, carry_scr = refs[2 * npg:]
    st = pl.program_id(1)
    kvw = KV_FOX * HEAD_DIM

    @pl.when(st == 0)
    def _():
        qf = q_ref[0].astype(F32)
        zero = jnp.zeros((1, HEAD_DIM), F32)
        rows = [jnp.concatenate([qf[:, h * HEAD_DIM:(h + 1) * HEAD_DIM] if c == h // HPG_FOX else zero
                                 for c in range(KV_FOX)], axis=1) for h in range(H_FOX)]
        qbd_scr[...] = jnp.concatenate(rows, axis=0)
        m_scr[...] = jnp.full(m_scr.shape, M_FLOOR, F32)
        l_scr[...] = jnp.zeros(l_scr.shape, F32)
        acc_scr[...] = jnp.zeros(acc_scr.shape, F32)
        carry_scr[...] = lfn_ref[0]

    qbd = qbd_scr[...].astype(BF16)
    tri = jnp.where(lax.broadcasted_iota(I32, (PAGE, PAGE), 0) <= lax.broadcasted_iota(I32, (PAGE, PAGE), 1),
                    1.0, 0.0).astype(BF16)
    carry = carry_scr[...]
    parts = []
    for p in range(npg):
        k = kv_pages[p][0, :, :kvw].astype(BF16)
        incl = _split_dot(lf_pages[p][0], tri)
        tot = incl[:, PAGE - 1:PAGE]
        parts.append(_nt_dot(qbd, k) * ATTN_SCALE + (carry + (tot - incl)))
        carry = carry + tot
    carry_scr[...] = carry
    s = jnp.concatenate(parts, axis=1)
    m_prev = m_scr[...]
    m_new = jnp.maximum(m_prev, jnp.max(s, axis=1, keepdims=True))
    alpha = jnp.exp(m_prev - m_new)
    pr = jnp.exp(s - m_new)
    l_scr[...] = alpha * l_scr[...] + jnp.sum(pr, axis=1, keepdims=True)
    acc = alpha * acc_scr[...]
    for p in range(npg):
        v = kv_pages[p][0, :, kvw:].astype(BF16)
        acc = acc + jnp.dot(pr[:, p * PAGE:(p + 1) * PAGE].astype(BF16), v, preferred_element_type=F32)
    acc_scr[...] = acc
    m_scr[...] = m_new

    @pl.when(st == n_steps - 1)
    def _():
        kn = new_ref[0, :, :kvw]
        vn = new_ref[0, :, kvw:]
        s_new = jnp.sum(qbd_scr[...] * kn, axis=1, keepdims=True) * ATTN_SCALE
        m_fin = jnp.maximum(m_scr[...], s_new)
        a2 = jnp.exp(m_scr[...] - m_fin)
        p_new = jnp.exp(s_new - m_fin)
        o = (a2 * acc_scr[...] + p_new * vn) / jnp.maximum(a2 * l_scr[...] + p_new, 1e-30)
        for h in range(H_FOX):
            c = (h // HPG_FOX) * HEAD_DIM
            o_ref[0, :, h * HEAD_DIM:(h + 1) * HEAD_DIM] = o[h:h + 1, c:c + HEAD_DIM].astype(o_ref.dtype)


def fox_sample_call(page_table, qf, kvf_new, lf_new, cache3, logf_t):
    bsz, n_pages = page_table.shape
    npg = FOX_PAGES
    assert n_pages % npg == 0
    n_steps = n_pages // npg
    page_of = lambda bb, st, pt, p: pt[bb, n_pages - 1 - (st * npg + p)]
    kv_specs = [pl.BlockSpec((1, PAGE, 2 * KV_FOX * HEAD_DIM), lambda bb, st, pt, p=p: (page_of(bb, st, pt, p), 0, 0))
                for p in range(npg)]
    lf_specs = [pl.BlockSpec((1, H_FOX, PAGE), lambda bb, st, pt, p=p: (page_of(bb, st, pt, p), 0, 0))
                for p in range(npg)]
    grid_spec = pltpu.PrefetchScalarGridSpec(
        num_scalar_prefetch=1,
        grid=(bsz, n_steps),
        in_specs=kv_specs + lf_specs + [pl.BlockSpec((1, 1, W_FOX), lambda bb, st, pt: (bb, 0, 0)),
                                        pl.BlockSpec((1, 1, 2 * KV_FOX * HEAD_DIM), lambda bb, st, pt: (bb, 0, 0)),
                                        pl.BlockSpec((1, H_FOX, 1), lambda bb, st, pt: (bb, 0, 0))],
        out_specs=pl.BlockSpec((1, 1, W_FOX), lambda bb, st, pt: (bb, 0, 0)),
        scratch_shapes=[pltpu.VMEM((H_FOX, KV_FOX * HEAD_DIM), F32), pltpu.VMEM((H_FOX, 1), F32),
                        pltpu.VMEM((H_FOX, 1), F32), pltpu.VMEM((H_FOX, KV_FOX * HEAD_DIM), F32),
                        pltpu.VMEM((H_FOX, 1), F32)])
    return pl.pallas_call(
        functools.partial(_fox_sample_kernel, npg=npg, n_steps=n_steps),
        grid_spec=grid_spec,
        out_shape=jax.ShapeDtypeStruct((bsz, 1, W_FOX), BF16),
        compiler_params=_cparams(("arbitrary", "arbitrary")),
        name="fox_sample",
    )(page_table, *([cache3] * npg), *([logf_t] * npg), qf, kvf_new, lf_new)


def sample_mixer(x, sc, sh, g_pre, W, pe, b1, w2b, b2, cache_nsa3, win_buf, cache_fox3, logf_t, page_table):
    bsz = x.shape[1]
    n_pages = page_table.shape[1]
    past = n_pages * PAGE
    pb, kvcs, win, kvf, small = inproj_call(x, g_pre, sc, sh, W["wm"], W["ws"], W["bs"], bsz)
    per_seq = lambda a: a.reshape(bsz, 1, a.shape[-1])
    pb, kvcs, win, kvf, small = map(per_seq, (pb, kvcs, win, kvf, small))
    qn = pb[:, :, COL_QN:COL_QN + W_NSA]
    qf = pb[:, :, COL_QF:COL_QF + W_FOX]
    kvc = compress_sample_call(page_table, cache_nsa3, W["w1p"], pe, b1, w2b, b2)
    ocmp, idx = cmpsel_sample_call(qn, kvc, past=past)
    n_keep = min(N_SEL, -(-(past + 1) // SEL_BLOCK))
    idx_flat = idx[:, :G_NSA, :n_keep].reshape(-1)
    osel = sel_sample_call(idx_flat, page_table, qn, kvcs, cache_nsa3, past=past, n_keep=n_keep)
    owin, win_new = win_sample_call(qn, win_buf, win)
    logf = small[:, :, 3 * H_NSA:3 * H_NSA + H_FOX]
    ofox = fox_sample_call(page_table, qf, kvf, logf.reshape(bsz, H_FOX, 1), cache_fox3, logf_t)
    to_rows = lambda a: a.reshape(1, bsz, a.shape[-1])
    return tuple(map(to_rows, (ocmp, osel, owin, ofox, small))) + (kvcs, win_new, kvf, logf)


PROMPT_TM = 512
ATTN_TQ = 256
ATTN_TK = 256


def kernel(x_prompt, x_sample, c_prompt, c_sample, cache_nsa_kv, state_nsa_win, cache_fox_kv, cache_fox_logf,
           page_table, w_ada, b_ada, g_pre_mix, g_post_mix, g_pre_ffn, g_post_ffn, w_in, b_nsa_gate, b_fox_forget,
           w_cmp1, b_cmp1, w_cmp2, b_cmp2, pe_cmp, g_grp, w_out, w_router, b_router, w_up, b_up, w_down, b_down):
    assert w_ada.shape[0] == 1 and x_sample.shape[1] == 1
    bp, t, d = x_prompt.shape
    bs = x_sample.shape[0]
    n_phys = cache_nsa_kv.shape[1]
    W = prep_weights(w_in[0], b_nsa_gate[0], b_fox_forget[0], w_cmp1[0], w_out[0], w_router[0], b_router[0])
    w2b = w_cmp2[0].astype(BF16)
    row = lambda a: a[0][None, :]

    ada = ada_call(jnp.concatenate([c_prompt, c_sample], axis=0), w_ada[0], row(b_ada))
    mods_p = [m[:, None, :] for m in jnp.split(ada[:bp], 6, axis=-1)]
    mods_s = [m[None] for m in jnp.split(ada[bp:], 6, axis=-1)]
    xs = x_sample.reshape(1, bs, d)

    ocmp, osel, owin, ofox, small, kvcs_p, win_p, kvf_p, logf_p = prompt_mixer(
        x_prompt, mods_p[1], mods_p[0], row(g_pre_mix), W, pe_cmp[0], b_cmp1[0], w2b, b_cmp2[0],
        tm=PROMPT_TM, tq=ATTN_TQ, tk=ATTN_TK)
    merge_args = (row(g_grp), W["wout"], row(g_post_mix))
    ffn_args = (W["wrh"], W["wrl"], W["br"])
    x1_p, h2_p, route_p = merge_call(ocmp, osel, owin, ofox, small, x_prompt, *merge_args, mods_p[2],
                                     row(g_pre_ffn), mods_p[4], mods_p[3], *ffn_args, 256)

    cache_nsa3 = cache_nsa_kv[0].reshape(n_phys, PAGE, 1024)
    cache_fox3 = cache_fox_kv[0].reshape(n_phys, PAGE, 1024)
    logf_t = jnp.swapaxes(cache_fox_logf[0], 1, 2)
    ocmp_s, osel_s, owin_s, ofox_s, small_s, kvcs_s, win_s, kvf_s, logf_s = sample_mixer(
        xs, mods_s[1], mods_s[0], row(g_pre_mix), W, pe_cmp[0], b_cmp1[0], w2b, b_cmp2[0],
        cache_nsa3, state_nsa_win[0].reshape(bs, WINDOW, 512), cache_fox3, logf_t, page_table)
    x1_s, h2_s, route_s = merge_call(ocmp_s, osel_s, owin_s, ofox_s, small_s, xs, *merge_args, mods_s[2],
                                     row(g_pre_ffn), mods_s[4], mods_s[3], *ffn_args, bs)

    n_p = bp * t
    h2_all = jnp.concatenate([h2_p.reshape(n_p, d), h2_s.reshape(bs, d)], axis=0)
    route_all = jnp.concatenate([route_p.reshape(n_p, LANES), route_s.reshape(bs, LANES)], axis=0)[:, :2 * TOP_K]
    slot_tok, slot_w, dest, blk_e, n_used = moe_dispatch(route_all)
    a = moe_up_call(blk_e, n_used, h2_all[slot_tok], w_up[0], b_up[0][:, None, :])
    y = moe_down_call(blk_e, n_used, a, w_down[0], b_down[0][:, None, :], slot_w)
    yg = y[dest]
    y_p = final_call(yg[:, :n_p].reshape(TOP_K, bp, t, d), x1_p, mods_p[5], row(g_post_ffn), PROMPT_TM)
    y_s = final_call(yg[:, n_p:].reshape(TOP_K, 1, bs, d), x1_s, mods_s[5], row(g_post_ffn), bs)

    return (y_p, y_s.reshape(bs, 1, d),
            kvcs_p.reshape(1, bp, t, 2, 2, G_NSA, HEAD_DIM),
            win_p[:, t - min(WINDOW, t):].reshape(1, bp, min(WINDOW, t), 2, G_NSA, HEAD_DIM),
            kvf_p.reshape(1, bp, t, 2, KV_FOX, HEAD_DIM),
            logf_p[None],
            kvcs_s.reshape(1, bs, 1, 2, 2, G_NSA, HEAD_DIM),
            win_s.reshape(1, bs, WINDOW, 2, G_NSA, HEAD_DIM),
            kvf_s.reshape(1, bs, 1, 2, KV_FOX, HEAD_DIM),
            logf_s.reshape(1, bs, 1, H_FOX))
```

```python
import functools

import jax
import jax.numpy as jnp
from jax import lax
from jax.experimental import pallas as pl
from jax.experimental.pallas import tpu as pltpu

F32 = jnp.float32
BF16 = jnp.bfloat16
I32 = jnp.int32

D_MODEL = 2048
HEAD_DIM = 128
H_NSA = 8
H_FOX = 8
G_NSA = 2
HPG_NSA = 4
KV_FOX = 4
HPG_FOX = 2
W_NSA = H_NSA * HEAD_DIM
W_FOX = H_FOX * HEAD_DIM
CMP_LEN = 32
CMP_STRIDE = 16
CMP_HID = 256
SEL_BLOCK = 64
N_SEL = 16
N_LOCAL_SEL = 2
WINDOW = 512
N_EXPERTS = 32
TOP_K = 4
D_FF = 2048
SWIGLU_ALPHA = 1.702
SWIGLU_LIMIT = 7.0
RMS_EPS = 1e-6
ATTN_SCALE = HEAD_DIM ** -0.5
LOG2E = 1.4426950408889634
LN2 = 0.6931471805599453
Q_PRESCALE = ATTN_SCALE * LOG2E
FORCE_SCORE = 1e9
MASKED = -2e30
M_FLOOR = -1e30

LANES = 128
VMEM_LIMIT = 52 * 1024 * 1024

N_MAIN = 4608
PROJ_TN = 512
COL_KVCS, COL_WIN, COL_KVF, COL_QN, COL_QF = 0, 1024, 1536, 2560, 3584


def _cparams(sem, vmem=VMEM_LIMIT):
    return pltpu.CompilerParams(dimension_semantics=sem, vmem_limit_bytes=vmem)


def _rms(x, g):
    return x * lax.rsqrt(jnp.mean(x * x, axis=-1, keepdims=True) + RMS_EPS) * g


def _nt_dot(a, b):
    return lax.dot_general(a, b, (((1,), (1,)), ((), ())), preferred_element_type=F32)


def _ada_kernel(c_ref, w_ref, b_ref, o_ref):
    c = c_ref[...]
    a = (c * jax.nn.sigmoid(c)).astype(BF16)
    o_ref[...] = jnp.dot(a, w_ref[...].astype(BF16), preferred_element_type=F32) + b_ref[...]


def ada_call(c, w, b):
    bc, d = c.shape
    n = w.shape[1]
    tn = 1024
    return pl.pallas_call(
        _ada_kernel,
        grid=(n // tn,),
        in_specs=[pl.BlockSpec((bc, d), lambda j: (0, 0)),
                  pl.BlockSpec((d, tn), lambda j: (0, j)),
                  pl.BlockSpec((1, tn), lambda j: (0, j))],
        out_specs=pl.BlockSpec((bc, tn), lambda j: (0, j)),
        out_shape=jax.ShapeDtypeStruct((bc, n), F32),
        compiler_params=_cparams(("arbitrary",)),
        name="ada",
    )(c, w, b)


def _inproj_kernel(x_ref, g_ref, sc_ref, sh_ref, wm_ref, ws_ref, bs_ref,
                   pb_ref, kvcs_ref, win_ref, kvf_ref, small_ref, h_scr):
    j = pl.program_id(2)

    @pl.when(j == 0)
    def _():
        h = _rms(x_ref[0], g_ref[...]) * (1.0 + sc_ref[0]) + sh_ref[0]
        hb = h.astype(BF16)
        h_scr[...] = hb
        z = jnp.dot(hb, ws_ref[...], preferred_element_type=F32) + bs_ref[...]
        lane = lax.broadcasted_iota(I32, z.shape, 1)
        small_ref[0] = jnp.where(lane < 3 * H_NSA, jax.nn.sigmoid(z), jax.nn.log_sigmoid(z))

    r = jnp.dot(h_scr[...], wm_ref[...], preferred_element_type=F32)
    pb_ref[0] = (r * jnp.where(j >= COL_QN // PROJ_TN, Q_PRESCALE, 1.0)).astype(BF16)
    tm = r.shape[0]

    def scatter_rows(ref, base, per_tok):
        for c in range(PROJ_TN // HEAD_DIM):
            ref[0, pl.ds(base + c, tm, stride=per_tok), :] = r[:, c * HEAD_DIM:(c + 1) * HEAD_DIM]

    for step, (ref, base, per_tok) in enumerate(((kvcs_ref, 0, 8), (kvcs_ref, 4, 8), (win_ref, 0, 4),
                                                 (kvf_ref, 0, 8), (kvf_ref, 4, 8))):
        pl.when(j == step)(functools.partial(scatter_rows, ref, base, per_tok))


def inproj_call(x, g, sc, sh, wm, ws, bs, tm):
    b, t, d = x.shape
    per_row = sc.shape[1] != 1
    nj = N_MAIN // PROJ_TN
    mod_spec = (pl.BlockSpec((1, tm, d), lambda bb, i, j: (bb, i, 0)) if per_row
                else pl.BlockSpec((1, 1, d), lambda bb, i, j: (bb, 0, 0)))
    return pl.pallas_call(
        _inproj_kernel,
        grid=(b, t // tm, nj),
        in_specs=[pl.BlockSpec((1, tm, d), lambda bb, i, j: (bb, i, 0)),
                  pl.BlockSpec((1, d), lambda bb, i, j: (0, 0)),
                  mod_spec, mod_spec,
                  pl.BlockSpec((d, PROJ_TN), lambda bb, i, j: (0, j)),
                  pl.BlockSpec((d, LANES), lambda bb, i, j: (0, 0)),
                  pl.BlockSpec((1, LANES), lambda bb, i, j: (0, 0))],
        out_specs=[pl.BlockSpec((1, tm, PROJ_TN), lambda bb, i, j: (bb, i, j)),
                   pl.BlockSpec((1, tm * 8, HEAD_DIM), lambda bb, i, j: (bb, i, 0)),
                   pl.BlockSpec((1, tm * 4, HEAD_DIM), lambda bb, i, j: (bb, i, 0)),
                   pl.BlockSpec((1, tm * 8, HEAD_DIM), lambda bb, i, j: (bb, i, 0)),
                   pl.BlockSpec((1, tm, LANES), lambda bb, i, j: (bb, i, 0))],
        out_shape=[jax.ShapeDtypeStruct((b, t, N_MAIN), BF16),
                   jax.ShapeDtypeStruct((b, t * 8, HEAD_DIM), F32),
                   jax.ShapeDtypeStruct((b, t * 4, HEAD_DIM), F32),
                   jax.ShapeDtypeStruct((b, t * 8, HEAD_DIM), F32),
                   jax.ShapeDtypeStruct((b, t, LANES), F32)],
        scratch_shapes=[pltpu.VMEM((tm, d), BF16)],
        compiler_params=_cparams(("arbitrary", "arbitrary", "arbitrary")),
        name="inproj",
    )(x, g, sc, sh, wm, ws, bs)


def _lane_cumsum(x):
    lane = lax.broadcasted_iota(I32, x.shape, 1)
    d = 1
    while d < LANES:
        x = x + jnp.where(lane >= d, pltpu.roll(x, d, axis=1), 0.0)
        d *= 2
    return x


def _cumsum_kernel(x_ref, o_ref):
    t = x_ref.shape[2]
    carry = jnp.zeros((x_ref.shape[1], 1), F32)
    for c in range(t // LANES):
        sl = slice(c * LANES, (c + 1) * LANES)
        y = _lane_cumsum(x_ref[0, :, sl]) + carry
        o_ref[0, :, sl] = y
        carry = y[:, LANES - 1:LANES]


def cumsum_call(x):
    b, h, t = x.shape
    return pl.pallas_call(
        _cumsum_kernel,
        grid=(b,),
        in_specs=[pl.BlockSpec((1, h, t), lambda bb: (bb, 0, 0))],
        out_specs=pl.BlockSpec((1, h, t), lambda bb: (bb, 0, 0)),
        out_shape=jax.ShapeDtypeStruct((b, h, t), F32),
        compiler_params=_cparams(("arbitrary",)),
        name="logf_cumsum",
    )(x)


def _nsa_slope(g, h):
    return jnp.where(g == 0, 2.0 ** -(h + 1), 2.0 ** -(HPG_NSA + h + 1)).astype(F32)


def _flash_schedule(mode, t, tq, tk):
    qi, ki, fl = [], [], []
    for i in range(t // tq):
        lo = max(0, (i * tq - WINDOW + 1) // tk) if mode == "win" else 0
        hi = (i * tq + tq - 1) // tk
        for kb in range(lo, hi + 1):
            below_diag = kb * tk + tk - 1 <= i * tq
            in_window = mode != "win" or (i * tq + tq - 1) - kb * tk < WINDOW
            qi.append(i)
            ki.append(kb)
            fl.append((1 if kb == lo else 0) | (2 if kb == hi else 0) | (0 if below_diag and in_window else 4))
    return tuple(jnp.asarray(a, I32) for a in (qi, ki, fl))


def _flash_kernel(qi_ref, ki_ref, fl_ref, *refs, mode, hpg, tq, tk):
    if mode == "fox":
        q_ref, k_ref, v_ref, cq_ref, ck_ref, o_ref, m_scr, acc_scr = refs
    elif mode == "sel":
        q_ref, k_ref, v_ref, sm_ref, o_ref, m_scr, acc_scr = refs
    else:
        q_ref, k_ref, v_ref, o_ref, m_scr, acc_scr = refs
    g = pl.program_id(1)
    p_idx = pl.program_id(2)
    q0 = qi_ref[p_idx] * tq
    k0 = ki_ref[p_idx] * tk
    flags = fl_ref[p_idx]

    @pl.when((flags & 1) != 0)
    def _():
        m_scr[...] = jnp.full(m_scr.shape, M_FLOOR, F32)
        acc_scr[...] = jnp.zeros(acc_scr.shape, F32)

    def step(positional):
        k = k_ref[0]
        v = v_ref[0]
        mask = None
        if positional:
            dist = (q0 + lax.broadcasted_iota(I32, (tq, tk), 0)) - (k0 + lax.broadcasted_iota(I32, (tq, tk), 1))
        if mode == "win":
            mask = (lax.bitcast_convert_type(dist, jnp.uint32) < WINDOW) if positional else None
        elif mode == "fox":
            mask = (dist >= 0) if positional else None
        else:
            blk = (k0 + lax.broadcasted_iota(I32, (sm_ref.shape[3], tk), 1)) // SEL_BLOCK
            expand = jnp.where(blk == lax.broadcasted_iota(I32, blk.shape, 0), 1.0, 0.0).astype(BF16)
            chosen = jnp.dot(sm_ref[0, 0], expand, preferred_element_type=F32)
            mask = (jnp.where(dist >= 0, chosen, 0.0) if positional else chosen) > 0.5
        krel = (k0 - q0 + lax.broadcasted_iota(I32, (1, tk), 1)).astype(F32)
        v_ones = jnp.concatenate([v, jnp.ones((tk, HEAD_DIM), BF16)], axis=1)
        for h in range(hpg):
            q = q_ref[0, :, h * HEAD_DIM:(h + 1) * HEAD_DIM]
            if mode == "fox":
                bias = (cq_ref[0, 0, h:h + 1, 0:1] - ck_ref[0, 0, h:h + 1, :]) * LOG2E
            else:
                bias = (_nsa_slope(g, h) * LOG2E) * krel
            s = _nt_dot(q, k) + bias
            if mask is not None:
                s = jnp.where(mask, s, MASKED)
            m_prev = m_scr[h]
            m_new = jnp.maximum(m_prev, jnp.max(s, axis=1, keepdims=True))
            alpha = jnp.exp2(m_prev - m_new)
            p = jnp.exp2(s - jnp.concatenate([m_new] * (tk // LANES), axis=1))
            pv = jnp.dot(p.astype(BF16), v_ones, preferred_element_type=F32)
            acc_scr[h] = jnp.concatenate([alpha, alpha], axis=1) * acc_scr[h] + pv
            m_scr[h] = m_new

    pl.when((flags & 4) != 0)(functools.partial(step, True))
    pl.when((flags & 4) == 0)(functools.partial(step, False))

    @pl.when((flags & 2) != 0)
    def _():
        for h in range(hpg):
            o = acc_scr[h, :, :HEAD_DIM] / jnp.maximum(acc_scr[h, :, HEAD_DIM:], 1e-30)
            o_ref[0, :, h * HEAD_DIM:(h + 1) * HEAD_DIM] = o.astype(o_ref.dtype)


def flash_call(mode, pb, *, tq, tk, extra=()):
    b, t, _ = pb.shape
    if mode == "fox":
        hpg, ngrp = HPG_FOX, KV_FOX
        qcol, kcol, vcol = COL_QF // (hpg * HEAD_DIM), COL_KVF // HEAD_DIM, COL_KVF // HEAD_DIM + KV_FOX
    else:
        hpg, ngrp = HPG_NSA, G_NSA
        base = (COL_KVCS + 512) if mode == "sel" else COL_WIN
        qcol, kcol, vcol = COL_QN // (hpg * HEAD_DIM), base // HEAD_DIM, base // HEAD_DIM + G_NSA
    qi, ki, fl = _flash_schedule(mode, t, tq, tk)
    in_specs = [pl.BlockSpec((1, tq, hpg * HEAD_DIM), lambda bb, g, p, qi, ki, fl: (bb, qi[p], qcol + g)),
                pl.BlockSpec((1, tk, HEAD_DIM), lambda bb, g, p, qi, ki, fl: (bb, ki[p], kcol + g)),
                pl.BlockSpec((1, tk, HEAD_DIM), lambda bb, g, p, qi, ki, fl: (bb, ki[p], vcol + g))]
    args = [pb, pb, pb]
    if mode == "fox":
        c8 = extra[0]
        in_specs += [pl.BlockSpec((1, 1, 8, tq), lambda bb, g, p, qi, ki, fl: (bb, g, 0, qi[p])),
                     pl.BlockSpec((1, 1, 8, tk), lambda bb, g, p, qi, ki, fl: (bb, g, 0, ki[p]))]
        args += [c8, c8]
    elif mode == "sel":
        sm = extra[0]
        in_specs += [pl.BlockSpec((1, 1, tq, sm.shape[3]), lambda bb, g, p, qi, ki, fl: (bb, g, qi[p], 0))]
        args += [sm]
    grid_spec = pltpu.PrefetchScalarGridSpec(
        num_scalar_prefetch=3,
        grid=(b, ngrp, int(qi.shape[0])),
        in_specs=in_specs,
        out_specs=pl.BlockSpec((1, tq, hpg * HEAD_DIM), lambda bb, g, p, qi, ki, fl: (bb, qi[p], g)),
        scratch_shapes=[pltpu.VMEM((hpg, tq, LANES), F32), pltpu.VMEM((hpg, tq, 2 * HEAD_DIM), F32)])
    return pl.pallas_call(
        functools.partial(_flash_kernel, mode=mode, hpg=hpg, tq=tq, tk=tk),
        grid_spec=grid_spec,
        out_shape=jax.ShapeDtypeStruct((b, t, ngrp * hpg * HEAD_DIM), BF16),
        compiler_params=_cparams(("arbitrary",) * 3),
        name="flash_" + mode,
    )(qi, ki, fl, *args)


def _compress_rows(load, n, w1_ref, pe_ref, b1_ref, w2_ref, b2_ref):
    outs = []
    for j in range(2):
        first = jnp.zeros((2 * n, CMP_HID), F32)
        second = jnp.zeros((2 * n, CMP_HID), F32)
        for sp in range(CMP_STRIDE // 2):
            def rows(off):
                parts = []
                for g in range(G_NSA):
                    halves = [load(j * G_NSA + g, 2 * sp + u)
                              + pe_ref[j, off + 2 * sp + u:off + 2 * sp + u + 1, :] for u in range(2)]
                    parts.append(jnp.concatenate(halves, axis=1))
                return jnp.concatenate(parts, axis=0).astype(BF16)
            first = first + jnp.dot(rows(0), w1_ref[j, sp], preferred_element_type=F32)
            second = second + jnp.dot(rows(CMP_STRIDE), w1_ref[j, CMP_STRIDE // 2 + sp],
                                      preferred_element_type=F32)
        for g in range(G_NSA):
            f = first[g * n:(g + 1) * n]
            s = pltpu.roll(second[g * n:(g + 1) * n], n - 1, axis=0)
            hid = jax.nn.gelu(f + s + b1_ref[j:j + 1, :])
            outs.append(jnp.dot(hid.astype(BF16), w2_ref[j], preferred_element_type=F32) + b2_ref[j:j + 1, :])
    return jnp.concatenate(outs, axis=1)


def _compress_prompt_kernel(x_ref, w1_ref, pe_ref, b1_ref, w2_ref, b2_ref, o_ref):
    n = o_ref.shape[1]
    load = lambda cb, s: x_ref[0, pl.ds(s * 8 + cb, n, stride=CMP_STRIDE * 8), :]
    o_ref[0] = _compress_rows(load, n, w1_ref, pe_ref, b1_ref, w2_ref, b2_ref).astype(o_ref.dtype)


def _cmp_weight_specs():
    def const(shape):
        return pl.BlockSpec(shape, lambda *a: (0,) * len(shape))
    return [const((2, CMP_STRIDE, 2 * HEAD_DIM, CMP_HID)), const((2, CMP_LEN, HEAD_DIM)),
            const((2, CMP_HID)), const((2, CMP_HID, HEAD_DIM)), const((2, HEAD_DIM))]


def compress_prompt_call(kvcs, w1p, pe, b1, w2, b2):
    b, t8, _ = kvcs.shape
    n = t8 // 8 // CMP_STRIDE
    return pl.pallas_call(
        _compress_prompt_kernel,
        grid=(b,),
        in_specs=[pl.BlockSpec((1, t8, HEAD_DIM), lambda bb: (bb, 0, 0))] + _cmp_weight_specs(),
        out_specs=pl.BlockSpec((1, n, 512), lambda bb: (bb, 0, 0)),
        out_shape=jax.ShapeDtypeStruct((b, n, 512), BF16),
        compiler_params=_cparams(("arbitrary",)),
        name="compress_prompt",
    )(kvcs, w1p, pe, b1, w2, b2)


def _rank_select(score, n_keep):
    ns = score.shape[1]
    lane = lax.broadcasted_iota(I32, score.shape, 1)
    rank = jnp.zeros(score.shape, F32)
    for c in range(ns):
        col = score[:, c:c + 1]
        rank = rank + jnp.where(lane > c, jnp.where(col >= score, 1.0, 0.0), jnp.where(col > score, 1.0, 0.0))
    return jnp.where(rank < n_keep, 1.0, 0.0)


def _split_dot(a, b_bf16):
    hi = a.astype(BF16)
    r1 = a - hi.astype(F32)
    mid = r1.astype(BF16)
    lo = (r1 - mid.astype(F32)).astype(BF16)
    return (jnp.dot(hi, b_bf16, preferred_element_type=F32) + jnp.dot(mid, b_bf16, preferred_element_type=F32)
            + jnp.dot(lo, b_bf16, preferred_element_type=F32))


def _overlap_matrix(nc, ns):
    ci = lax.broadcasted_iota(I32, (nc, ns), 0) * CMP_STRIDE
    sj = lax.broadcasted_iota(I32, (nc, ns), 1)
    return jnp.where(ci < (sj + 1) * SEL_BLOCK, jnp.where(ci + CMP_LEN > sj * SEL_BLOCK, 1.0, 0.0), 0.0).astype(BF16)


def _force_scores(score, blk, lag):
    recent = lax.bitcast_convert_type(lag, jnp.uint32) < N_LOCAL_SEL
    score = jnp.where(recent, FORCE_SCORE, jnp.where(lag >= 0, score, -FORCE_SCORE))
    return jnp.where(blk == 0, FORCE_SCORE, score)


def _cmpsel_prompt_kernel(q_ref, k_ref, v_ref, o_ref, sm_ref, *, tq, n_cmp, n_sel, n_keep):
    g = pl.program_id(1)
    i = pl.program_id(2)
    nc = k_ref.shape[1]
    k = k_ref[0]
    v = v_ref[0]
    qpos = i * tq + lax.broadcasted_iota(I32, (tq, nc), 0)
    cidx = lax.broadcasted_iota(I32, (tq, nc), 1)
    mask = jnp.where(cidx < n_cmp, cidx * CMP_STRIDE + CMP_LEN - 1, 2 ** 30) <= qpos
    center = (lax.broadcasted_iota(I32, (1, nc), 1) * CMP_STRIDE - i * tq).astype(F32) + 0.5 * (CMP_LEN - 1)
    imp = jnp.zeros((tq, nc), F32)
    for h in range(HPG_NSA):
        q = q_ref[0, :, h * HEAD_DIM:(h + 1) * HEAD_DIM]
        s = _nt_dot(q, k) * LN2 + _nsa_slope(g, h) * center
        s = jnp.where(mask, s, -1e30)
        e = jnp.where(mask, jnp.exp(s - jnp.max(s, axis=1, keepdims=True)), 0.0)
        p = e / jnp.maximum(jnp.sum(e, axis=1, keepdims=True), 1e-30)
        o_ref[0, :, h * HEAD_DIM:(h + 1) * HEAD_DIM] = jnp.dot(
            p.astype(BF16), v, preferred_element_type=F32).astype(o_ref.dtype)
        imp = imp + p
    score = _split_dot(imp, _overlap_matrix(nc, n_sel))
    blk = lax.broadcasted_iota(I32, (tq, n_sel), 1)
    lag = (i * tq + lax.broadcasted_iota(I32, (tq, n_sel), 0)) // SEL_BLOCK - blk
    sm_ref[0, 0] = _rank_select(_force_scores(score, blk, lag), n_keep).astype(sm_ref.dtype)


def cmpsel_prompt_call(pb, kvc, *, tq):
    b, t, _ = pb.shape
    nc = kvc.shape[1]
    n_cmp = nc - 1
    n_sel = -(-t // SEL_BLOCK)
    n_keep = min(N_SEL, n_sel)
    qcol = COL_QN // (HPG_NSA * HEAD_DIM)
    return pl.pallas_call(
        functools.partial(_cmpsel_prompt_kernel, tq=tq, n_cmp=n_cmp, n_sel=n_sel, n_keep=n_keep),
        grid=(b, G_NSA, t // tq),
        in_specs=[pl.BlockSpec((1, tq, HPG_NSA * HEAD_DIM), lambda bb, g, i: (bb, i, qcol + g)),
                  pl.BlockSpec((1, nc, HEAD_DIM), lambda bb, g, i: (bb, 0, g)),
                  pl.BlockSpec((1, nc, HEAD_DIM), lambda bb, g, i: (bb, 0, G_NSA + g))],
        out_specs=[pl.BlockSpec((1, tq, HPG_NSA * HEAD_DIM), lambda bb, g, i: (bb, i, g)),
                   pl.BlockSpec((1, 1, tq, n_sel), lambda bb, g, i: (bb, g, i, 0))],
        out_shape=[jax.ShapeDtypeStruct((b, t, W_NSA), BF16),
                   jax.ShapeDtypeStruct((b, G_NSA, t, n_sel), BF16)],
        compiler_params=_cparams(("arbitrary",) * 3),
        name="cmpsel_prompt",
    )(pb, kvc, kvc)


def _merge_kernel(ocmp_ref, osel_ref, owin_ref, ofox_ref, small_ref, x_ref, ggrp_ref, wout_ref, gpost_ref,
                  gt_ref, gpre_ref, sc_ref, sh_ref, wrh_ref, wrl_ref, br_ref, x1_ref, h2_ref, route_ref):
    gs = small_ref[0]
    parts = []
    for h in range(H_NSA):
        sl = slice(h * HEAD_DIM, (h + 1) * HEAD_DIM)
        parts.append(gs[:, h:h + 1] * ocmp_ref[0, :, sl].astype(F32)
                     + gs[:, H_NSA + h:H_NSA + h + 1] * osel_ref[0, :, sl].astype(F32)
                     + gs[:, 2 * H_NSA + h:2 * H_NSA + h + 1] * owin_ref[0, :, sl].astype(F32))
    o_nsa = jnp.concatenate(parts, axis=1)
    y = jnp.concatenate([_rms(o_nsa, ggrp_ref[:, :W_NSA]),
                         _rms(ofox_ref[0].astype(F32), ggrp_ref[:, W_NSA:])], axis=1).astype(BF16)
    m = jnp.dot(y, wout_ref[...], preferred_element_type=F32)
    x1 = x_ref[0] + gt_ref[0] * _rms(m, gpost_ref[...])
    x1_ref[0] = x1
    h2 = _rms(x1, gpre_ref[...]) * (1.0 + sc_ref[0]) + sh_ref[0]
    hi = h2.astype(BF16)
    h2_ref[0] = hi
    lo = (h2 - hi.astype(F32)).astype(BF16)
    logits = (jnp.dot(hi, wrh_ref[...], preferred_element_type=F32) + jnp.dot(hi, wrl_ref[...], preferred_element_type=F32)
              + jnp.dot(lo, wrh_ref[...], preferred_element_type=F32) + br_ref[...])
    lane = lax.broadcasted_iota(I32, logits.shape, 1)
    vals = jnp.where(lane < N_EXPERTS, logits, -jnp.inf)
    top_v, top_e = [], []
    for _ in range(TOP_K):
        mx = jnp.max(vals, axis=1, keepdims=True)
        idx = jnp.min(jnp.where(vals == mx, lane, LANES), axis=1, keepdims=True)
        top_v.append(mx)
        top_e.append(idx)
        vals = jnp.where(lane == idx, -jnp.inf, vals)
    ex = [jnp.exp(v - top_v[0]) for v in top_v]
    den = ex[0] + ex[1] + ex[2] + ex[3]
    route = jnp.zeros(logits.shape, F32)
    for kx in range(TOP_K):
        route = jnp.where(lane == kx, top_e[kx].astype(F32), route)
        route = jnp.where(lane == TOP_K + kx, ex[kx] / den, route)
    route_ref[0] = route


def merge_call(ocmp, osel, owin, ofox, small, x, ggrp, wout, gpost, gt, gpre, sc, sh, wrh, wrl, br, tm):
    b, t, d = x.shape
    per_row = sc.shape[1] != 1
    row = lambda w: pl.BlockSpec((1, tm, w), lambda bb, i: (bb, i, 0))
    const = lambda shape: pl.BlockSpec(shape, lambda bb, i: (0,) * len(shape))
    mod = row(d) if per_row else pl.BlockSpec((1, 1, d), lambda bb, i: (bb, 0, 0))
    return pl.pallas_call(
        _merge_kernel,
        grid=(b, t // tm),
        in_specs=[row(W_NSA), row(W_NSA), row(W_NSA), row(W_FOX), row(LANES), row(d),
                  const((1, d)), const((d, d)), const((1, d)), mod, const((1, d)), mod, mod,
                  const((d, LANES)), const((d, LANES)), const((1, LANES))],
        out_specs=[row(d), row(d), row(LANES)],
        out_shape=[jax.ShapeDtypeStruct((b, t, d), F32), jax.ShapeDtypeStruct((b, t, d), BF16),
                   jax.ShapeDtypeStruct((b, t, LANES), F32)],
        compiler_params=_cparams(("arbitrary", "arbitrary")),
        name="merge",
    )(ocmp, osel, owin, ofox, small, x, ggrp, wout, gpost, gt, gpre, sc, sh, wrh, wrl, br)


MOE_TM = 512
MOE_TF = 1024
MOE_TN = 1024


def _expert_changed(blk_e_ref, i):
    return (i == 0) | (blk_e_ref[i] != blk_e_ref[jnp.maximum(i - 1, 0)])


def _moe_up_kernel(blk_e_ref, nused_ref, x_ref, wg_ref, wl_ref, bg_ref, bl_ref, a_ref, wg_bf, wl_bf):
    i = pl.program_id(1)

    @pl.when(_expert_changed(blk_e_ref, i))
    def _():
        wg_bf[...] = wg_ref[0].astype(BF16)
        wl_bf[...] = wl_ref[0].astype(BF16)

    @pl.when(i < nused_ref[0])
    def _():
        x = x_ref[...]
        ug = jnp.dot(x, wg_bf[...], preferred_element_type=F32) + bg_ref[0]
        ul = jnp.dot(x, wl_bf[...], preferred_element_type=F32) + bl_ref[0]
        glu = jnp.minimum(ug, SWIGLU_LIMIT)
        lin = jnp.clip(ul, -SWIGLU_LIMIT, SWIGLU_LIMIT)
        a_ref[...] = (glu * jax.nn.sigmoid(SWIGLU_ALPHA * glu) * (lin + 1.0)).astype(a_ref.dtype)

    @pl.when(i >= nused_ref[0])
    def _():
        a_ref[...] = jnp.zeros(a_ref.shape, a_ref.dtype)


def moe_up_call(blk_e, n_used, xs, w_up, b_up):
    n_slots, d = xs.shape
    n_blk = n_slots // MOE_TM
    nf = D_FF // MOE_TF
    grid_spec = pltpu.PrefetchScalarGridSpec(
        num_scalar_prefetch=2,
        grid=(nf, n_blk),
        in_specs=[pl.BlockSpec((MOE_TM, d), lambda f, i, be, nu: (i, 0)),
                  pl.BlockSpec((1, d, MOE_TF), lambda f, i, be, nu: (be[i], 0, f)),
                  pl.BlockSpec((1, d, MOE_TF), lambda f, i, be, nu: (be[i], 0, nf + f)),
                  pl.BlockSpec((1, 1, MOE_TF), lambda f, i, be, nu: (be[i], 0, f)),
                  pl.BlockSpec((1, 1, MOE_TF), lambda f, i, be, nu: (be[i], 0, nf + f))],
        out_specs=pl.BlockSpec((MOE_TM, MOE_TF), lambda f, i, be, nu: (i, f)),
        scratch_shapes=[pltpu.VMEM((d, MOE_TF), BF16), pltpu.VMEM((d, MOE_TF), BF16)])
    return pl.pallas_call(
        _moe_up_kernel,
        grid_spec=grid_spec,
        out_shape=jax.ShapeDtypeStruct((n_slots, D_FF), BF16),
        compiler_params=_cparams(("arbitrary", "arbitrary")),
        name="moe_up",
    )(blk_e, n_used, xs, w_up, w_up, b_up, b_up)


def _moe_down_kernel(blk_e_ref, nused_ref, a_ref, wd_ref, bd_ref, sw_ref, y_ref, wd_bf):
    i = pl.program_id(1)

    @pl.when(_expert_changed(blk_e_ref, i))
    def _():
        wd_bf[...] = wd_ref[0].astype(BF16)

    @pl.when(i < nused_ref[0])
    def _():
        y = jnp.dot(a_ref[...], wd_bf[...], preferred_element_type=F32) + bd_ref[0]
        y_ref[...] = (y * sw_ref[...]).astype(y_ref.dtype)

    @pl.when(i >= nused_ref[0])
    def _():
        y_ref[...] = jnp.zeros(y_ref.shape, y_ref.dtype)


def moe_down_call(blk_e, n_used, a, w_down, b_down, slot_w):
    n_slots, dff = a.shape
    d = w_down.shape[2]
    n_blk = n_slots // MOE_TM
    grid_spec = pltpu.PrefetchScalarGridSpec(
        num_scalar_prefetch=2,
        grid=(d // MOE_TN, n_blk),
        in_specs=[pl.BlockSpec((MOE_TM, dff), lambda c, i, be, nu: (i, 0)),
                  pl.BlockSpec((1, dff, MOE_TN), lambda c, i, be, nu: (be[i], 0, c)),
                  pl.BlockSpec((1, 1, MOE_TN), lambda c, i, be, nu: (be[i], 0, c)),
                  pl.BlockSpec((MOE_TM, 1), lambda c, i, be, nu: (i, 0))],
        out_specs=pl.BlockSpec((MOE_TM, MOE_TN), lambda c, i, be, nu: (i, c)),
        scratch_shapes=[pltpu.VMEM((dff, MOE_TN), BF16)])
    return pl.pallas_call(
        _moe_down_kernel,
        grid_spec=grid_spec,
        out_shape=jax.ShapeDtypeStruct((n_slots, d), BF16),
        compiler_params=_cparams(("arbitrary", "arbitrary")),
        name="moe_down",
    )(blk_e, n_used, a, w_down, b_down, slot_w)


def moe_dispatch(route):
    n = route.shape[0]
    n_asg = n * TOP_K
    n_blk = -(-(n_asg + N_EXPERTS * (MOE_TM - 1)) // MOE_TM)
    e_flat = route[:, :TOP_K].astype(I32).reshape(-1)
    w_flat = route[:, TOP_K:2 * TOP_K].reshape(-1)
    order = jnp.argsort(e_flat).astype(I32)
    rank = jnp.argsort(order).astype(I32)
    counts = jnp.sum((e_flat[:, None] == jnp.arange(N_EXPERTS, dtype=I32)[None, :]).astype(I32), axis=0)
    padded = (counts + MOE_TM - 1) // MOE_TM * MOE_TM
    pad_end = jnp.cumsum(padded)
    pad_start = pad_end - padded
    grp_start = jnp.cumsum(counts) - counts
    blk_e = jnp.minimum(jnp.searchsorted(pad_end, jnp.arange(n_blk, dtype=I32) * MOE_TM, side="right"),
                        N_EXPERTS - 1).astype(I32)
    n_used = (pad_end[-1:] // MOE_TM).astype(I32)
    slot_e = jnp.repeat(blk_e, MOE_TM)
    off = jnp.arange(n_blk * MOE_TM, dtype=I32) - pad_start[slot_e]
    live = off < counts[slot_e]
    src = order[jnp.clip(grp_start[slot_e] + off, 0, n_asg - 1)]
    slot_tok = jnp.where(live, src // TOP_K, 0)
    slot_w = jnp.where(live, w_flat[src], 0.0)
    dest = (pad_start[e_flat] + rank - grp_start[e_flat]).reshape(n, TOP_K).T
    return slot_tok, slot_w[:, None], dest, blk_e, n_used


def _final_kernel(yg_ref, x1_ref, gt_ref, gpost_ref, o_ref):
    s = yg_ref[0, 0].astype(F32)
    for kx in range(1, TOP_K):
        s = s + yg_ref[kx, 0].astype(F32)
    o_ref[0] = x1_ref[0] + gt_ref[0] * _rms(s, gpost_ref[...])


def final_call(yg, x1, gt, gpost, tm):
    b, t, d = x1.shape
    per_row = gt.shape[1] != 1
    mod = (pl.BlockSpec((1, tm, d), lambda bb, i: (bb, i, 0)) if per_row
           else pl.BlockSpec((1, 1, d), lambda bb, i: (bb, 0, 0)))
    return pl.pallas_call(
        _final_kernel,
        grid=(b, t // tm),
        in_specs=[pl.BlockSpec((TOP_K, 1, tm, d), lambda bb, i: (0, bb, i, 0)),
                  pl.BlockSpec((1, tm, d), lambda bb, i: (bb, i, 0)),
                  mod, pl.BlockSpec((1, d), lambda bb, i: (0, 0))],
        out_specs=pl.BlockSpec((1, tm, d), lambda bb, i: (bb, i, 0)),
        out_shape=jax.ShapeDtypeStruct((b, t, d), F32),
        compiler_params=_cparams(("arbitrary", "arbitrary")),
        name="final",
    )(yg, x1, gt, gpost)


def prep_weights(w_in, b_nsa_gate, b_fox_forget, w_cmp1, w_out, w_router, b_router):
    d = w_in.shape[0]
    o_qn, o_kvn, o_gn, o_qf, o_kvf, o_fl = 0, 1024, 2560, 2584, 3608, 4632
    wm = jnp.concatenate([w_in[:, o_kvn:o_kvn + 1536], w_in[:, o_kvf:o_kvf + 1024],
                          w_in[:, o_qn:o_qn + 1024], w_in[:, o_qf:o_qf + 1024]], axis=1).astype(BF16)
    pad = LANES - 3 * H_NSA - H_FOX
    ws = jnp.concatenate([w_in[:, o_gn:o_gn + 3 * H_NSA], w_in[:, o_fl:o_fl + H_FOX],
                          jnp.zeros((d, pad), F32)], axis=1).astype(BF16)
    bs = jnp.concatenate([b_nsa_gate, b_fox_forget, jnp.zeros((pad,), F32)])[None, :]
    w1p = w_cmp1.reshape(2, CMP_STRIDE, 2 * HEAD_DIM, CMP_HID).astype(BF16)
    wr = jnp.concatenate([w_router, jnp.zeros((d, LANES - N_EXPERTS), F32)], axis=1)
    wrh = wr.astype(BF16)
    wrl = (wr - wrh.astype(F32)).astype(BF16)
    br = jnp.concatenate([b_router, jnp.zeros((LANES - N_EXPERTS,), F32)])[None, :]
    return dict(wm=wm, ws=ws, bs=bs, w1p=w1p, wout=w_out.astype(BF16), wrh=wrh, wrl=wrl, br=br)


def prompt_mixer(x, sc, sh, g_pre, W, pe, b1, w2b, b2, *, tm, tq, tk):
    b, t, _ = x.shape
    pb, kvcs, win, kvf, small = inproj_call(x, g_pre, sc, sh, W["wm"], W["ws"], W["bs"], tm)
    kvc = compress_prompt_call(kvcs, W["w1p"], pe, b1, w2b, b2)
    ocmp, selmask = cmpsel_prompt_call(pb, kvc, tq=tq)
    osel = flash_call("sel", pb, tq=tq, tk=tk, extra=(selmask,))
    owin = flash_call("win", pb, tq=tq, tk=tk)
    logf = small[:, :, 3 * H_NSA:3 * H_NSA + H_FOX]
    cum = cumsum_call(jnp.swapaxes(logf, 1, 2))
    c8 = jnp.pad(cum.reshape(b, KV_FOX, HPG_FOX, t), ((0, 0), (0, 0), (0, 8 - HPG_FOX), (0, 0)))
    ofox = flash_call("fox", pb, tq=tq, tk=tk, extra=(c8,))
    return ocmp, osel, owin, ofox, small, kvcs, win, kvf, logf


PAGE = 128
CMP_PAGES = 16
FOX_PAGES = 8
NS_PAD = 384


def _head_rows(qf, g):
    rows = [qf[:, (g * HPG_NSA + h) * HEAD_DIM:(g * HPG_NSA + h + 1) * HEAD_DIM] for h in range(HPG_NSA)]
    return jnp.concatenate(rows + [jnp.zeros((8 - HPG_NSA, HEAD_DIM), F32)], axis=0).astype(BF16)


def _slope_col(g):
    row = lax.broadcasted_iota(I32, (8, 1), 0)
    col = jnp.zeros((8, 1), F32)
    for h in range(HPG_NSA):
        col = jnp.where(row == h, 2.0 ** -(g * HPG_NSA + h + 1), col)
    return col


def _masked_softmax(s, mask):
    s = jnp.where(mask, s, -1e30)
    e = jnp.where(mask, jnp.exp(s - jnp.max(s, axis=1, keepdims=True)), 0.0)
    return e / jnp.maximum(jnp.sum(e, axis=1, keepdims=True), 1e-30)


def _store_heads(o_ref, o, g):
    for h in range(HPG_NSA):
        c = (g * HPG_NSA + h) * HEAD_DIM
        o_ref[0, :, c:c + HEAD_DIM] = o[h:h + 1].astype(o_ref.dtype)


def _compress_sample_kernel(pt_ref, *refs, npg):
    pages = refs[:npg + 1]
    w1_ref, pe_ref, b1_ref, w2_ref, b2_ref, o_ref = refs[npg + 1:]
    per_page = PAGE // CMP_STRIDE
    n = (npg + 1) * per_page

    def load(cb, s):
        return jnp.concatenate([pg[0, pl.ds(s * 8 + cb, per_page, stride=CMP_STRIDE * 8), :] for pg in pages], axis=0)

    r = _compress_rows(load, n, w1_ref, pe_ref, b1_ref, w2_ref, b2_ref)
    o_ref[0] = r[:npg * per_page].astype(o_ref.dtype)


def compress_sample_call(page_table, cache3, w1p, pe, b1, w2, b2):
    bsz, n_pages = page_table.shape
    npg = CMP_PAGES
    assert n_pages % npg == 0
    page_specs = [pl.BlockSpec((1, PAGE * 8, HEAD_DIM),
                               lambda bb, gi, pt, p=p: (pt[bb, jnp.minimum(gi * npg + p, n_pages - 1)], 0, 0))
                  for p in range(npg + 1)]
    rows_out = npg * PAGE // CMP_STRIDE
    grid_spec = pltpu.PrefetchScalarGridSpec(
        num_scalar_prefetch=1,
        grid=(bsz, n_pages // npg),
        in_specs=page_specs + _cmp_weight_specs(),
        out_specs=pl.BlockSpec((1, rows_out, 512), lambda bb, gi, pt: (bb, gi, 0)))
    return pl.pallas_call(
        functools.partial(_compress_sample_kernel, npg=npg),
        grid_spec=grid_spec,
        out_shape=jax.ShapeDtypeStruct((bsz, n_pages * PAGE // CMP_STRIDE, 512), BF16),
        compiler_params=_cparams(("arbitrary", "arbitrary")),
        name="compress_sample",
    )(page_table, *([cache3] * (npg + 1)), w1p, pe, b1, w2, b2)


def _cmpsel_sample_kernel(q_ref, kv_ref, o_ref, idx_ref, *, past, n_cmp, n_keep):
    nc = kv_ref.shape[1]
    qf = q_ref[0].astype(F32)
    cidx = lax.broadcasted_iota(I32, (1, nc), 1)
    mask = jnp.where(cidx < n_cmp, cidx * CMP_STRIDE + CMP_LEN - 1, 2 ** 30) <= past
    center = (cidx * CMP_STRIDE - past).astype(F32) + 0.5 * (CMP_LEN - 1)
    overlap = _overlap_matrix(nc, NS_PAD)
    blk = lax.broadcasted_iota(I32, (8, NS_PAD), 1)
    lag = past // SEL_BLOCK - blk
    ii =lax.broadcasted_iota(I32, (NS_PAD, NS_PAD), 0)
    jj = lax.broadcasted_iota(I32, (NS_PAD, NS_PAD), 1)
    slot = lax.broadcasted_iota(I32, (NS_PAD, LANES), 1).astype(F32)
    rows = []
    for g in range(G_NSA):
        k = kv_ref[0, :, g * HEAD_DIM:(g + 1) * HEAD_DIM]
        v = kv_ref[0, :, (G_NSA + g) * HEAD_DIM:(G_NSA + g + 1) * HEAD_DIM]
        s = _nt_dot(_head_rows(qf, g), k) * LN2 + _slope_col(g) * center
        p = _masked_softmax(s, jnp.broadcast_to(mask, s.shape))
        _store_heads(o_ref, jnp.dot(p.astype(BF16), v, preferred_element_type=F32), g)
        imp = jnp.sum(p[0:HPG_NSA], axis=0, keepdims=True)
        score = _split_dot(jnp.broadcast_to(imp, (8, nc)), overlap)
        score = _force_scores(score, blk, lag)
        row = score[0:1, :]
        col = score.T[:, 0:1]
        ge = jnp.where(col >= row, 1.0, 0.0)
        gt = jnp.where(col > row, 1.0, 0.0)
        rank_row = jnp.sum(jnp.where(ii < jj, ge, gt), axis=0, keepdims=True)
        rank_col = jnp.sum(jnp.where(jj < ii, 1.0 - gt, 1.0 - ge), axis=1, keepdims=True)
        sel_row = jnp.where(rank_row < n_keep, 1.0, 0.0)
        sel_col = jnp.where(rank_col < n_keep, 1.0, 0.0)
        before = jnp.sum(jnp.where(jj < ii, sel_row, 0.0), axis=1, keepdims=True)
        pick = jnp.where(before == slot, sel_col * ii[:, 0:1].astype(F32), 0.0)
        rows.append(jnp.sum(pick, axis=0, keepdims=True))
    idx_ref[0] = jnp.concatenate(rows + [jnp.zeros((8 - G_NSA, LANES), F32)], axis=0).astype(I32)


def cmpsel_sample_call(qn, kvc, *, past):
    bsz = qn.shape[0]
    nc = kvc.shape[1]
    n_cmp = (past + 1) // CMP_STRIDE - 1
    n_sel = -(-(past + 1) // SEL_BLOCK)
    assert n_sel <= NS_PAD and n_cmp <= nc
    n_keep = min(N_SEL, n_sel)
    return pl.pallas_call(
        functools.partial(_cmpsel_sample_kernel, past=past, n_cmp=n_cmp, n_keep=n_keep),
        grid=(bsz,),
        in_specs=[pl.BlockSpec((1, 1, W_NSA), lambda bb: (bb, 0, 0)),
                  pl.BlockSpec((1, nc, 512), lambda bb: (bb, 0, 0))],
        out_specs=[pl.BlockSpec((1, 1, W_NSA), lambda bb: (bb, 0, 0)),
                   pl.BlockSpec((1, 8, LANES), lambda bb: (bb, 0, 0))],
        out_shape=[jax.ShapeDtypeStruct((bsz, 1, W_NSA), BF16), jax.ShapeDtypeStruct((bsz, 8, LANES), I32)],
        compiler_params=_cparams(("arbitrary",)),
        name="cmpsel_sample",
    )(qn, kvc)


def _sel_sample_kernel(idx_ref, pt_ref, q_ref, new_ref, cache_ref, o_ref, kvbuf, sem, *, past, n_pages, n_keep):
    bb = pl.program_id(0)
    per_page = PAGE // SEL_BLOCK
    blk_rows = SEL_BLOCK * 8
    copies = []
    for g in range(G_NSA):
        for kx in range(n_keep):
            j = idx_ref[(bb * G_NSA + g) * n_keep + kx]
            page = pt_ref[bb, jnp.minimum(j // per_page, n_pages - 1)]
            r0 = pl.multiple_of((j % per_page) * blk_rows, blk_rows)
            cp = pltpu.make_async_copy(cache_ref.at[page, pl.ds(r0, blk_rows), :], kvbuf.at[g, kx], sem.at[0])
            cp.start()
            copies.append(cp)
    for cp in copies:
        cp.wait()
    qf = q_ref[0].astype(F32)
    nkeys = n_keep * SEL_BLOCK
    lane = lax.broadcasted_iota(I32, (1, nkeys), 1)
    row0 = lax.broadcasted_iota(I32, (SEL_BLOCK, HEAD_DIM), 0) == 0
    for g in range(G_NSA):
        kpos = lane % SEL_BLOCK
        ks, vs = [], []
        for kx in range(n_keep):
            j = idx_ref[(bb * G_NSA + g) * n_keep + kx]
            is_new = j * SEL_BLOCK >= past
            fresh = jnp.logical_and(is_new, row0)
            ks.append(jnp.where(fresh, new_ref[0, 4 + g:5 + g, :], kvbuf[g, kx, pl.ds(4 + g, SEL_BLOCK, stride=8), :]))
            vs.append(jnp.where(fresh, new_ref[0, 6 + g:7 + g, :], kvbuf[g, kx, pl.ds(6 + g, SEL_BLOCK, stride=8), :]))
            kpos = kpos + jnp.where(lane // SEL_BLOCK == kx, j * SEL_BLOCK, 0)
        k = jnp.concatenate(ks, axis=0).astype(BF16)
        v = jnp.concatenate(vs, axis=0).astype(BF16)
        s = _nt_dot(_head_rows(qf, g), k) * LN2 + _slope_col(g) * (kpos - past).astype(F32)
        p = _masked_softmax(s, jnp.broadcast_to(kpos <= past, s.shape))
        _store_heads(o_ref, jnp.dot(p.astype(BF16), v, preferred_element_type=F32), g)


def sel_sample_call(idx_flat, page_table, qn, kvcs_new, cache3, *, past, n_keep):
    bsz, n_pages = page_table.shape
    assert past % SEL_BLOCK == 0 and past == n_pages * PAGE
    grid_spec = pltpu.PrefetchScalarGridSpec(
        num_scalar_prefetch=2,
        grid=(bsz,),
        in_specs=[pl.BlockSpec((1, 1, W_NSA), lambda bb, ix, pt: (bb, 0, 0)),
                  pl.BlockSpec((1, 8, HEAD_DIM), lambda bb, ix, pt: (bb, 0, 0)),
                  pl.BlockSpec(memory_space=pl.ANY)],
        out_specs=pl.BlockSpec((1, 1, W_NSA), lambda bb, ix, pt: (bb, 0, 0)),
        scratch_shapes=[pltpu.VMEM((G_NSA, n_keep, SEL_BLOCK * 8, HEAD_DIM), F32),
                        pltpu.SemaphoreType.DMA((1,))])
    return pl.pallas_call(
        functools.partial(_sel_sample_kernel, past=past, n_pages=n_pages, n_keep=n_keep),
        grid_spec=grid_spec,
        out_shape=jax.ShapeDtypeStruct((bsz, 1, W_NSA), BF16),
        compiler_params=_cparams(("arbitrary",)),
        name="sel_sample",
    )(idx_flat, page_table, qn, kvcs_new, cache3)


def _win_sample_kernel(q_ref, buf_ref, new_ref, o_ref, nb_ref):
    rows = buf_ref.shape[1]
    wb = rows // 4
    nb_ref[0] = pltpu.roll(buf_ref[0], rows - 4, axis=0)
    nb_ref[0, rows - 4:rows, :] = new_ref[0]
    qf = q_ref[0].astype(F32)
    krel = (lax.broadcasted_iota(I32, (1, wb), 1) - (wb - 1)).astype(F32)
    for g in range(G_NSA):
        k = nb_ref[0, pl.ds(g, wb, stride=4), :].astype(BF16)
        v = nb_ref[0, pl.ds(G_NSA + g, wb, stride=4), :].astype(BF16)
        s = _nt_dot(_head_rows(qf, g), k) * LN2 + _slope_col(g) * krel
        p = _masked_softmax(s, jnp.full(s.shape, True))
        _store_heads(o_ref, jnp.dot(p.astype(BF16), v, preferred_element_type=F32), g)


def win_sample_call(qn, win_buf, win_new):
    bsz, rows, _ = win_buf.shape
    assert rows == WINDOW * 4
    return pl.pallas_call(
        _win_sample_kernel,
        grid=(bsz,),
        in_specs=[pl.BlockSpec((1, 1, W_NSA), lambda bb: (bb, 0, 0)),
                  pl.BlockSpec((1, rows, HEAD_DIM), lambda bb: (bb, 0, 0)),
                  pl.BlockSpec((1, 4, HEAD_DIM), lambda bb: (bb, 0, 0))],
        out_specs=[pl.BlockSpec((1, 1, W_NSA), lambda bb: (bb, 0, 0)),
                   pl.BlockSpec((1, rows, HEAD_DIM), lambda bb: (bb, 0, 0))],
        out_shape=[jax.ShapeDtypeStruct((bsz, 1, W_NSA), BF16), jax.ShapeDtypeStruct((bsz, rows, HEAD_DIM), F32)],
        compiler_params=_cparams(("arbitrary",)),
        name="win_sample",
    )(qn, win_buf, win_new)


def _fox_sample_kernel(pt_ref, *refs, npg, n_steps):
    kv_pages = refs[:npg]
    lf_pages = refs[npg:2 * npg]
    q_ref, new_ref, lfn_ref, o_ref, qbd_scr, m_scr, l_scr, acc_scr, carry_scr = refs[2 * npg:]
    st = pl.program_id(1)
    kvw = KV_FOX * HEAD_DIM

    @pl.when(st == 0)
    def _():
        qf = q_ref[0].astype(F32)
        zero = jnp.zeros((1, HEAD_DIM), F32)
        rows = [jnp.concatenate([qf[:, h * HEAD_DIM:(h + 1) * HEAD_DIM] if c == h // HPG_FOX else zero
                                 for c in range(KV_FOX)], axis=1) for h in range(H_FOX)]
        qbd_scr[...] = jnp.concatenate(rows, axis=0)
        m_scr[...] = jnp.full(m_scr.shape, M_FLOOR, F32)
        l_scr[...] = jnp.zeros(l_scr.shape, F32)
        acc_scr[...] = jnp.zeros(acc_scr.shape, F32)
        carry_scr[...] = lfn_ref[0]

    qbd = qbd_scr[...].astype(BF16)
    tri = jnp.where(lax.broadcasted_iota(I32, (PAGE, PAGE), 0) <= lax.broadcasted_iota(I32, (PAGE, PAGE), 1),
                    1.0, 0.0).astype(BF16)
    carry = carry_scr[...]
    parts = []
    def heads(ref, base):
        n = ref.shape[1] // 8
        return jnp.concatenate([ref[0, pl.ds(base + h, n, stride=8), :] for h in range(KV_FOX)], axis=1)

    for p in range(npg):
        k = heads(kv_pages[p], 0).astype(BF16)
        incl = _split_dot(lf_pages[p][0], tri)
        tot = incl[:, PAGE - 1:PAGE]
        parts.append(_nt_dot(qbd, k) * LN2 + (carry + (tot - incl)))
        carry = carry + tot
    carry_scr[...] = carry
    s = jnp.concatenate(parts, axis=1)
    m_prev = m_scr[...]
    m_new = jnp.maximum(m_prev, jnp.max(s, axis=1, keepdims=True))
    alpha = jnp.exp(m_prev - m_new)
    pr = jnp.exp(s - m_new)
    l_scr[...] = alpha * l_scr[...] + jnp.sum(pr, axis=1, keepdims=True)
    acc = alpha * acc_scr[...]
    for p in range(npg):
        v = heads(kv_pages[p], KV_FOX).astype(BF16)
        acc = acc + jnp.dot(pr[:, p * PAGE:(p + 1) * PAGE].astype(BF16), v, preferred_element_type=F32)
    acc_scr[...] = acc
    m_scr[...] = m_new

    @pl.when(st == n_steps - 1)
    def _():
        kn = heads(new_ref, 0)
        vn = heads(new_ref, KV_FOX)
        s_new = jnp.sum(qbd_scr[...] * kn, axis=1, keepdims=True) * LN2
        m_fin = jnp.maximum(m_scr[...], s_new)
        a2 = jnp.exp(m_scr[...] - m_fin)
        p_new = jnp.exp(s_new - m_fin)
        o = (a2 * acc_scr[...] + p_new * vn) / jnp.maximum(a2 * l_scr[...] + p_new, 1e-30)
        for h in range(H_FOX):
            c = (h // HPG_FOX) * HEAD_DIM
            o_ref[0, :, h * HEAD_DIM:(h + 1) * HEAD_DIM] = o[h:h + 1, c:c + HEAD_DIM].astype(o_ref.dtype)


def fox_sample_call(page_table, qf, kvf_new, lf_new, cache3, logf_t):
    bsz, n_pages = page_table.shape
    npg = FOX_PAGES
    assert n_pages % npg == 0
    n_steps = n_pages // npg
    page_of = lambda bb, st, pt, p: pt[bb, n_pages - 1 - (st * npg + p)]
    kv_specs = [pl.BlockSpec((1, PAGE * 8, HEAD_DIM), lambda bb, st, pt, p=p: (page_of(bb, st, pt, p), 0, 0))
                for p in range(npg)]
    lf_specs = [pl.BlockSpec((1, H_FOX, PAGE), lambda bb, st, pt, p=p: (page_of(bb, st, pt, p), 0, 0))
                for p in range(npg)]
    grid_spec = pltpu.PrefetchScalarGridSpec(
        num_scalar_prefetch=1,
        grid=(bsz, n_steps),
        in_specs=kv_specs + lf_specs + [pl.BlockSpec((1, 1, W_FOX), lambda bb, st, pt: (bb, 0, 0)),
                                        pl.BlockSpec((1, 8, HEAD_DIM), lambda bb, st, pt: (bb, 0, 0)),
                                        pl.BlockSpec((1, H_FOX, 1), lambda bb, st, pt: (bb, 0, 0))],
        out_specs=pl.BlockSpec((1, 1, W_FOX), lambda bb, st, pt: (bb, 0, 0)),
        scratch_shapes=[pltpu.VMEM((H_FOX, KV_FOX * HEAD_DIM), F32), pltpu.VMEM((H_FOX, 1), F32),
                        pltpu.VMEM((H_FOX, 1), F32), pltpu.VMEM((H_FOX, KV_FOX * HEAD_DIM), F32),
                        pltpu.VMEM((H_FOX, 1), F32)])
    return pl.pallas_call(
        functools.partial(_fox_sample_kernel, npg=npg, n_steps=n_steps),
        grid_spec=grid_spec,
        out_shape=jax.ShapeDtypeStruct((bsz, 1, W_FOX), BF16),
        compiler_params=_cparams(("arbitrary", "arbitrary")),
        name="fox_sample",
    )(page_table, *([cache3] * npg), *([logf_t] * npg), qf, kvf_new, lf_new)


def sample_mixer(x, sc, sh, g_pre, W, pe, b1, w2b, b2, cache_nsa3, win_buf, cache_fox3, logf_t, page_table):
    bsz = x.shape[1]
    n_pages = page_table.shape[1]
    past = n_pages * PAGE
    pb, kvcs, win, kvf, small = inproj_call(x, g_pre, sc, sh, W["wm"], W["ws"], W["bs"], bsz)
    per_seq = lambda a: a.reshape(bsz, -1, a.shape[-1])
    pb, kvcs, win, kvf, small = map(per_seq, (pb, kvcs, win, kvf, small))
    qn = pb[:, :, COL_QN:COL_QN + W_NSA]
    qf = pb[:, :, COL_QF:COL_QF + W_FOX]
    kvc = compress_sample_call(page_table, cache_nsa3, W["w1p"], pe, b1, w2b, b2)
    ocmp, idx = cmpsel_sample_call(qn, kvc, past=past)
    n_keep = min(N_SEL, -(-(past + 1) // SEL_BLOCK))
    idx_flat = idx[:, :G_NSA, :n_keep].reshape(-1)
    osel = sel_sample_call(idx_flat, page_table, qn, kvcs, cache_nsa3, past=past, n_keep=n_keep)
    owin, win_new = win_sample_call(qn, win_buf, win)
    logf = small[:, :, 3 * H_NSA:3 * H_NSA + H_FOX]
    ofox = fox_sample_call(page_table, qf, kvf, logf.reshape(bsz, H_FOX, 1), cache_fox3, logf_t)
    to_rows = lambda a: a.reshape(1, bsz, a.shape[-1])
    return tuple(map(to_rows, (ocmp, osel, owin, ofox, small))) + (kvcs, win_new, kvf, logf)


PROMPT_TM = 512
ATTN_TQ = 512
ATTN_TK = 512


def kernel(x_prompt, x_sample, c_prompt, c_sample, cache_nsa_kv, state_nsa_win, cache_fox_kv, cache_fox_logf,
           page_table, w_ada, b_ada, g_pre_mix, g_post_mix, g_pre_ffn, g_post_ffn, w_in, b_nsa_gate, b_fox_forget,
           w_cmp1, b_cmp1, w_cmp2, b_cmp2, pe_cmp, g_grp, w_out, w_router, b_router, w_up, b_up, w_down, b_down):
    assert w_ada.shape[0] == 1 and x_sample.shape[1] == 1
    bp, t, d = x_prompt.shape
    bs = x_sample.shape[0]
    n_phys = cache_nsa_kv.shape[1]
    W = prep_weights(w_in[0], b_nsa_gate[0], b_fox_forget[0], w_cmp1[0], w_out[0], w_router[0], b_router[0])
    w2b = w_cmp2[0].astype(BF16)
    row = lambda a: a[0][None, :]

    ada = ada_call(jnp.concatenate([c_prompt, c_sample], axis=0), w_ada[0], row(b_ada))
    mods_p = [m[:, None, :] for m in jnp.split(ada[:bp], 6, axis=-1)]
    mods_s = [m[None] for m in jnp.split(ada[bp:], 6, axis=-1)]
    xs = x_sample.reshape(1, bs, d)

    ocmp, osel, owin, ofox, small, kvcs_p, win_p, kvf_p, logf_p = prompt_mixer(
        x_prompt, mods_p[1], mods_p[0], row(g_pre_mix), W, pe_cmp[0], b_cmp1[0], w2b, b_cmp2[0],
        tm=PROMPT_TM, tq=ATTN_TQ, tk=ATTN_TK)
    merge_args = (row(g_grp), W["wout"], row(g_post_mix))
    ffn_args = (W["wrh"], W["wrl"], W["br"])
    x1_p, h2_p, route_p = merge_call(ocmp, osel, owin, ofox, small, x_prompt, *merge_args, mods_p[2],
                                     row(g_pre_ffn), mods_p[4], mods_p[3], *ffn_args, 256)

    cache_nsa3 = cache_nsa_kv.reshape(n_phys, PAGE * 8, HEAD_DIM)
    cache_fox3 = cache_fox_kv.reshape(n_phys, PAGE * 8, HEAD_DIM)
    logf_t = jnp.swapaxes(cache_fox_logf[0], 1, 2)
    ocmp_s, osel_s, owin_s, ofox_s, small_s, kvcs_s, win_s, kvf_s, logf_s = sample_mixer(
        xs, mods_s[1], mods_s[0], row(g_pre_mix), W, pe_cmp[0], b_cmp1[0], w2b, b_cmp2[0],
        cache_nsa3, state_nsa_win.reshape(bs, WINDOW * 4, HEAD_DIM), cache_fox3, logf_t, page_table)
    x1_s, h2_s, route_s = merge_call(ocmp_s, osel_s, owin_s, ofox_s, small_s, xs, *merge_args, mods_s[2],
                                     row(g_pre_ffn), mods_s[4], mods_s[3], *ffn_args, bs)

    n_p = bp * t
    h2_all = jnp.concatenate([h2_p.reshape(n_p, d), h2_s.reshape(bs, d)], axis=0)
    route_all = jnp.concatenate([route_p.reshape(n_p, LANES), route_s.reshape(bs, LANES)], axis=0)[:, :2 * TOP_K]
    slot_tok, slot_w, dest, blk_e, n_used = moe_dispatch(route_all)
    a = moe_up_call(blk_e, n_used, h2_all[slot_tok], w_up[0], b_up[0][:, None, :])
    y = moe_down_call(blk_e, n_used, a, w_down[0], b_down[0][:, None, :], slot_w)
    yg_p = y[dest[:, :n_p]].reshape(TOP_K, bp, t, d)
    yg_s = y[dest[:, n_p:]].reshape(TOP_K, 1, bs, d)
    y_p = final_call(yg_p, x1_p, mods_p[5], row(g_post_ffn), PROMPT_TM)
    y_s = final_call(yg_s, x1_s, mods_s[5], row(g_post_ffn), bs)

    wlen = min(WINDOW, t)
    return (y_p, y_s.reshape(bs, 1, d),
            kvcs_p.reshape(1, bp, t, 2, 2, G_NSA, HEAD_DIM),
            win_p[:, (t - wlen) * 4:].reshape(1, bp, wlen, 2, G_NSA, HEAD_DIM),
            kvf_p.reshape(1, bp, t, 2, KV_FOX, HEAD_DIM),
            logf_p[None],
            kvcs_s.reshape(1, bs, 1, 2, 2, G_NSA, HEAD_DIM),
            win_s.reshape(1, bs, WINDOW, 2, G_NSA, HEAD_DIM),
            kvf_s.reshape(1, bs, 1, 2, KV_FOX, HEAD_DIM),
            logf_s.reshape(1, bs, 1, H_FOX))
```

```python
import functools

import jax
import jax.numpy as jnp
from jax import lax
from jax.experimental import pallas as pl
from jax.experimental.pallas import tpu as pltpu

F32 = jnp.float32
BF16 = jnp.bfloat16
I32 = jnp.int32

D_MODEL = 2048
HEAD_DIM = 128
H_NSA = 8
H_FOX = 8
G_NSA = 2
HPG_NSA = 4
KV_FOX = 4
HPG_FOX = 2
W_NSA = H_NSA * HEAD_DIM
W_FOX = H_FOX * HEAD_DIM
CMP_LEN = 32
CMP_STRIDE = 16
CMP_HID = 256
SEL_BLOCK = 64
N_SEL = 16
N_LOCAL_SEL = 2
WINDOW = 512
N_EXPERTS = 32
TOP_K = 4
D_FF = 2048
SWIGLU_ALPHA = 1.702
SWIGLU_LIMIT = 7.0
RMS_EPS = 1e-6
ATTN_SCALE = HEAD_DIM ** -0.5
LOG2E = 1.4426950408889634
LN2 = 0.6931471805599453
Q_PRESCALE = ATTN_SCALE * LOG2E
FORCE_SCORE = 1e9
MASKED = -2e30
M_FLOOR = -1e30

LANES = 128
VMEM_LIMIT = 52 * 1024 * 1024

N_MAIN = 4608
PROJ_TN = 512
COL_KVCS, COL_WIN, COL_KVF, COL_QN, COL_QF = 0, 1024, 1536, 2560, 3584


def _cparams(sem, vmem=VMEM_LIMIT):
    return pltpu.CompilerParams(dimension_semantics=sem, vmem_limit_bytes=vmem)


def _rms(x, g):
    return x * lax.rsqrt(jnp.mean(x * x, axis=-1, keepdims=True) + RMS_EPS) * g


def _nt_dot(a, b):
    return lax.dot_general(a, b, (((1,), (1,)), ((), ())), preferred_element_type=F32)


def _ada_kernel(c_ref, w_ref, b_ref, o_ref):
    c = c_ref[...]
    a = (c * jax.nn.sigmoid(c)).astype(BF16)
    o_ref[...] = jnp.dot(a, w_ref[...].astype(BF16), preferred_element_type=F32) + b_ref[...]


def ada_call(c, w, b):
    bc, d = c.shape
    n = w.shape[1]
    tn = 1024
    return pl.pallas_call(
        _ada_kernel,
        grid=(n // tn,),
        in_specs=[pl.BlockSpec((bc, d), lambda j: (0, 0)),
                  pl.BlockSpec((d, tn), lambda j: (0, j)),
                  pl.BlockSpec((1, tn), lambda j: (0, j))],
        out_specs=pl.BlockSpec((bc, tn), lambda j: (0, j)),
        out_shape=jax.ShapeDtypeStruct((bc, n), F32),
        compiler_params=_cparams(("arbitrary",)),
        name="ada",
    )(c, w, b)


def _inproj_kernel(x_ref, g_ref, sc_ref, sh_ref, wm_ref, ws_ref, bs_ref,
                   pb_ref, kvcs_ref, win_ref, kvf_ref, small_ref, h_scr):
    j = pl.program_id(2)

    @pl.when(j == 0)
    def _():
        h = _rms(x_ref[0], g_ref[...]) * (1.0 + sc_ref[0]) + sh_ref[0]
        hb = h.astype(BF16)
        h_scr[...] = hb
        z = jnp.dot(hb, ws_ref[...], preferred_element_type=F32) + bs_ref[...]
        lane = lax.broadcasted_iota(I32, z.shape, 1)
        small_ref[0] = jnp.where(lane < 3 * H_NSA, jax.nn.sigmoid(z), jax.nn.log_sigmoid(z))

    r = jnp.dot(h_scr[...], wm_ref[...], preferred_element_type=F32)
    pb_ref[0] = (r * jnp.where(j >= COL_QN // PROJ_TN, Q_PRESCALE, 1.0)).astype(BF16)
    tm = r.shape[0]

    def scatter_rows(ref, base, per_tok):
        for c in range(PROJ_TN // HEAD_DIM):
            ref[0, pl.ds(base + c, tm, stride=per_tok), :] = r[:, c * HEAD_DIM:(c + 1) * HEAD_DIM]

    for step, (ref, base, per_tok) in enumerate(((kvcs_ref, 0, 8), (kvcs_ref, 4, 8), (win_ref, 0, 4),
                                                 (kvf_ref, 0, 8), (kvf_ref, 4, 8))):
        pl.when(j == step)(functools.partial(scatter_rows, ref, base, per_tok))


def inproj_call(x, g, sc, sh, wm, ws, bs, tm):
    b, t, d = x.shape
    per_row = sc.shape[1] != 1
    nj = N_MAIN // PROJ_TN
    mod_spec = (pl.BlockSpec((1, tm, d), lambda bb, i, j: (bb, i, 0)) if per_row
                else pl.BlockSpec((1, 1, d), lambda bb, i, j: (bb, 0, 0)))
    return pl.pallas_call(
        _inproj_kernel,
        grid=(b, t // tm, nj),
        in_specs=[pl.BlockSpec((1, tm, d), lambda bb, i, j: (bb, i, 0)),
                  pl.BlockSpec((1, d), lambda bb, i, j: (0, 0)),
                  mod_spec, mod_spec,
                  pl.BlockSpec((d, PROJ_TN), lambda bb, i, j: (0, j)),
                  pl.BlockSpec((d, LANES), lambda bb, i, j: (0, 0)),
                  pl.BlockSpec((1, LANES), lambda bb, i, j: (0, 0))],
        out_specs=[pl.BlockSpec((1, tm, PROJ_TN), lambda bb, i, j: (bb, i, j)),
                   pl.BlockSpec((1, tm * 8, HEAD_DIM), lambda bb, i, j: (bb, i, 0)),
                   pl.BlockSpec((1, tm * 4, HEAD_DIM), lambda bb, i, j: (bb, i, 0)),
                   pl.BlockSpec((1, tm * 8, HEAD_DIM), lambda bb, i, j: (bb, i, 0)),
                   pl.BlockSpec((1, tm, LANES), lambda bb, i, j: (bb, i, 0))],
        out_shape=[jax.ShapeDtypeStruct((b, t, N_MAIN), BF16),
                   jax.ShapeDtypeStruct((b, t * 8, HEAD_DIM), F32),
                   jax.ShapeDtypeStruct((b, t * 4, HEAD_DIM), F32),
                   jax.ShapeDtypeStruct((b, t * 8, HEAD_DIM), F32),
                   jax.ShapeDtypeStruct((b, t, LANES), F32)],
        scratch_shapes=[pltpu.VMEM((tm, d), BF16)],
        compiler_params=_cparams(("arbitrary", "arbitrary", "arbitrary")),
        name="inproj",
    )(x, g, sc, sh, wm, ws, bs)


def _lane_cumsum(x):
    lane = lax.broadcasted_iota(I32, x.shape, 1)
    d = 1
    while d < LANES:
        x = x + jnp.where(lane >= d, pltpu.roll(x, d, axis=1), 0.0)
        d *= 2
    return x


def _cumsum_kernel(x_ref, o_ref):
    t = x_ref.shape[2]
    carry = jnp.zeros((x_ref.shape[1], 1), F32)
    for c in range(t // LANES):
        sl = slice(c * LANES, (c + 1) * LANES)
        y = _lane_cumsum(x_ref[0, :, sl]) + carry
        o_ref[0, :, sl] = y
        carry = y[:, LANES - 1:LANES]


def cumsum_call(x):
    b, h, t = x.shape
    return pl.pallas_call(
        _cumsum_kernel,
        grid=(b,),
        in_specs=[pl.BlockSpec((1, h, t), lambda bb: (bb, 0, 0))],
        out_specs=pl.BlockSpec((1, h, t), lambda bb: (bb, 0, 0)),
        out_shape=jax.ShapeDtypeStruct((b, h, t), F32),
        compiler_params=_cparams(("arbitrary",)),
        name="logf_cumsum",
    )(x)


def _nsa_slope(g, h):
    return jnp.where(g == 0, 2.0 ** -(h + 1), 2.0 ** -(HPG_NSA + h + 1)).astype(F32)


def _flash_schedule(mode, t, tq, tk):
    qi, ki, fl = [], [], []
    for i in range(t // tq):
        lo = max(0, (i * tq - WINDOW + 1) // tk) if mode == "win" else 0
        hi = (i * tq + tq - 1) // tk
        for kb in range(lo, hi + 1):
            below_diag = kb * tk + tk - 1 <= i * tq
            in_window = mode != "win" or (i * tq + tq - 1) - kb * tk < WINDOW
            qi.append(i)
            ki.append(kb)
            fl.append((1 if kb == lo else 0) | (2 if kb == hi else 0) | (0 if below_diag and in_window else 4))
    return tuple(jnp.asarray(a, I32) for a in (qi, ki, fl))


def _flash_kernel(qi_ref, ki_ref, fl_ref, *refs, mode, hpg, tq, tk):
    if mode == "fox":
        q_ref, k_ref, v_ref, cq_ref, ck_ref, o_ref, m_scr, acc_scr = refs
    elif mode == "sel":
        q_ref, k_ref, v_ref, sm_ref, o_ref, m_scr, acc_scr = refs
    else:
        q_ref, k_ref, v_ref, o_ref, m_scr, acc_scr = refs
    g = pl.program_id(1)
    p_idx = pl.program_id(2)
    q0 = qi_ref[p_idx] * tq
    k0 = ki_ref[p_idx] * tk
    flags = fl_ref[p_idx]

    @pl.when((flags & 1) != 0)
    def _():
        m_scr[...] = jnp.full(m_scr.shape, M_FLOOR, F32)
        acc_scr[...] = jnp.zeros(acc_scr.shape, F32)

    def step(positional):
        k = k_ref[0]
        v = v_ref[0]
        mask = None
        if positional:
            dist = (q0 + lax.broadcasted_iota(I32, (tq, tk), 0)) - (k0 + lax.broadcasted_iota(I32, (tq, tk), 1))
        if mode == "win":
            mask = (lax.bitcast_convert_type(dist, jnp.uint32) < WINDOW) if positional else None
        elif mode == "fox":
            mask = (dist >= 0) if positional else None
        else:
            blk = (k0 + lax.broadcasted_iota(I32, (sm_ref.shape[3], tk), 1)) // SEL_BLOCK
            expand = jnp.where(blk == lax.broadcasted_iota(I32, blk.shape, 0), 1.0, 0.0).astype(BF16)
            chosen = jnp.dot(sm_ref[0, 0], expand, preferred_element_type=F32)
            mask = (jnp.where(dist >= 0, chosen, 0.0) if positional else chosen) > 0.5
        krel = (k0 - q0 + lax.broadcasted_iota(I32, (1, tk), 1)).astype(F32)
        v_ones = jnp.concatenate([v, jnp.ones((tk, HEAD_DIM), BF16)], axis=1)
        for h in range(hpg):
            q = q_ref[0, :, h * HEAD_DIM:(h + 1) * HEAD_DIM]
            if mode == "fox":
                bias = (cq_ref[0, 0, h:h + 1, 0:1] - ck_ref[0, 0, h:h + 1, :]) * LOG2E
            else:
                bias = (_nsa_slope(g, h) * LOG2E) * krel
            s = _nt_dot(q, k) + bias
            if mask is not None:
                s = jnp.where(mask, s, MASKED)
            m_prev = m_scr[h]
            m_new = jnp.maximum(m_prev, jnp.max(s, axis=1, keepdims=True))
            alpha = jnp.exp2(m_prev - m_new)
            p = jnp.exp2(s - jnp.concatenate([m_new] * (tk // LANES), axis=1))
            pv = jnp.dot(p.astype(BF16), v_ones, preferred_element_type=F32)
            acc_scr[h] = jnp.concatenate([alpha, alpha], axis=1) * acc_scr[h] + pv
            m_scr[h] = m_new

    pl.when((flags & 4) != 0)(functools.partial(step, True))
    pl.when((flags & 4) == 0)(functools.partial(step, False))

    @pl.when((flags & 2) != 0)
    def _():
        for h in range(hpg):
            o = acc_scr[h, :, :HEAD_DIM] / jnp.maximum(acc_scr[h, :, HEAD_DIM:], 1e-30)
            o_ref[0, :, h * HEAD_DIM:(h + 1) * HEAD_DIM] = o.astype(o_ref.dtype)


def flash_call(mode, pb, *, tq, tk, extra=()):
    b, t, _ = pb.shape
    if mode == "fox":
        hpg, ngrp = HPG_FOX, KV_FOX
        qcol, kcol, vcol = COL_QF // (hpg * HEAD_DIM), COL_KVF // HEAD_DIM, COL_KVF // HEAD_DIM + KV_FOX
    else:
        hpg, ngrp = HPG_NSA, G_NSA
        base = (COL_KVCS + 512) if mode == "sel" else COL_WIN
        qcol, kcol, vcol = COL_QN // (hpg * HEAD_DIM), base // HEAD_DIM, base // HEAD_DIM + G_NSA
    qi, ki, fl = _flash_schedule(mode, t, tq, tk)
    in_specs = [pl.BlockSpec((1, tq, hpg * HEAD_DIM), lambda bb, g, p, qi, ki, fl: (bb, qi[p], qcol + g)),
                pl.BlockSpec((1, tk, HEAD_DIM), lambda bb, g, p, qi, ki, fl: (bb, ki[p], kcol + g)),
                pl.BlockSpec((1, tk, HEAD_DIM), lambda bb, g, p, qi, ki, fl: (bb, ki[p], vcol + g))]
    args = [pb, pb, pb]
    if mode == "fox":
        c8 = extra[0]
        in_specs += [pl.BlockSpec((1, 1, 8, tq), lambda bb, g, p, qi, ki, fl: (bb, g, 0, qi[p])),
                     pl.BlockSpec((1, 1, 8, tk), lambda bb, g, p, qi, ki, fl: (bb, g, 0, ki[p]))]
        args += [c8, c8]
    elif mode == "sel":
        sm = extra[0]
        in_specs += [pl.BlockSpec((1, 1, tq, sm.shape[3]), lambda bb, g, p, qi, ki, fl: (bb, g, qi[p], 0))]
        args += [sm]
    grid_spec = pltpu.PrefetchScalarGridSpec(
        num_scalar_prefetch=3,
        grid=(b, ngrp, int(qi.shape[0])),
        in_specs=in_specs,
        out_specs=pl.BlockSpec((1, tq, hpg * HEAD_DIM), lambda bb, g, p, qi, ki, fl: (bb, qi[p], g)),
        scratch_shapes=[pltpu.VMEM((hpg, tq, LANES), F32), pltpu.VMEM((hpg, tq, 2 * HEAD_DIM), F32)])
    return pl.pallas_call(
        functools.partial(_flash_kernel, mode=mode, hpg=hpg, tq=tq, tk=tk),
        grid_spec=grid_spec,
        out_shape=jax.ShapeDtypeStruct((b, t, ngrp * hpg * HEAD_DIM), BF16),
        compiler_params=_cparams(("arbitrary",) * 3),
        name="flash_" + mode,
    )(qi, ki, fl, *args)


def _cmp_bias_kernel(pe_ref, w1_ref, b1_ref, o_ref):
    for j in range(2):
        acc = jnp.zeros((8, CMP_HID), F32)
        for p in range(CMP_LEN // 2):
            acc = acc + _split_dot(jnp.broadcast_to(pe_ref[j, p:p + 1, :], (8, 2 * HEAD_DIM)), w1_ref[j, p])
        o_ref[j] = acc + b1_ref[j:j + 1, :]


def cmp_bias_call(pe2, w1p, b1):
    return pl.pallas_call(_cmp_bias_kernel, out_shape=jax.ShapeDtypeStruct((2, 8, CMP_HID), F32),
                          compiler_params=_cparams(None), name="cmp_bias")(pe2, w1p, b1)


def _compress_rows(load, n, w1_ref, hb_ref, w2_ref, b2_ref):
    outs = []
    for j in range(2):
        both = jnp.zeros((2 * n, 2 * CMP_HID), F32)
        for sp in range(CMP_STRIDE // 2):
            parts = [jnp.concatenate([load(j * G_NSA + g, 2 * sp + u) for u in range(2)], axis=1)
                     for g in range(G_NSA)]
            both = both + jnp.dot(jnp.concatenate(parts, axis=0).astype(BF16), w1_ref[j, sp],
                                  preferred_element_type=F32)
        for g in range(G_NSA):
            f = both[g * n:(g + 1) * n, :CMP_HID]
            s = pltpu.roll(both[g * n:(g + 1) * n, CMP_HID:], n - 1, axis=0)
            hid = jax.nn.gelu(f + s + hb_ref[j, 0:1, :])
            outs.append(jnp.dot(hid.astype(BF16), w2_ref[j], preferred_element_type=F32) + b2_ref[j:j + 1, :])
    return jnp.concatenate(outs, axis=1)


def _compress_prompt_kernel(x_ref, w1_ref, hb_ref, w2_ref, b2_ref, o_ref):
    n = o_ref.shape[1]
    load = lambda cb, s: x_ref[0, pl.ds(s * 8 + cb, n, stride=CMP_STRIDE * 8), :]
    o_ref[0] = _compress_rows(load, n, w1_ref, hb_ref, w2_ref, b2_ref).astype(o_ref.dtype)


def _cmp_weight_specs():
    def const(shape):
        return pl.BlockSpec(shape, lambda *a: (0,) * len(shape))
    return [const((2, CMP_STRIDE // 2, 2 * HEAD_DIM, 2 * CMP_HID)), const((2, 8, CMP_HID)),
            const((2, CMP_HID, HEAD_DIM)), const((2, HEAD_DIM))]


def compress_prompt_call(kvcs, w1f, hb, w2, b2):
    b, t8, _ = kvcs.shape
    n = t8 // 8 // CMP_STRIDE
    return pl.pallas_call(
        _compress_prompt_kernel,
        grid=(b,),
        in_specs=[pl.BlockSpec((1, t8, HEAD_DIM), lambda bb: (bb, 0, 0))] + _cmp_weight_specs(),
        out_specs=pl.BlockSpec((1, n, 512), lambda bb: (bb, 0, 0)),
        out_shape=jax.ShapeDtypeStruct((b, n, 512), BF16),
        compiler_params=_cparams(("arbitrary",)),
        name="compress_prompt",
    )(kvcs, w1f, hb, w2, b2)


def _rank_select_cols(score_t, n_keep):
    ns = score_t.shape[0]
    row_id = lax.broadcasted_iota(I32, score_t.shape, 0)
    rank = jnp.zeros(score_t.shape, F32)
    for c in range(ns):
        row = score_t[c:c + 1, :]
        rank = rank + jnp.where(row_id > c, jnp.where(row >= score_t, 1.0, 0.0), jnp.where(row > score_t, 1.0, 0.0))
    return jnp.where(rank < n_keep, 1.0, 0.0)


def _split_dot(a, b_bf16):
    hi = a.astype(BF16)
    r1 = a - hi.astype(F32)
    mid = r1.astype(BF16)
    lo = (r1 - mid.astype(F32)).astype(BF16)
    return (jnp.dot(hi, b_bf16, preferred_element_type=F32) + jnp.dot(mid, b_bf16, preferred_element_type=F32)
            + jnp.dot(lo, b_bf16, preferred_element_type=F32))


def _overlap_matrix(nc, ns):
    ci = lax.broadcasted_iota(I32, (nc, ns), 0) * CMP_STRIDE
    sj = lax.broadcasted_iota(I32, (nc, ns), 1)
    return jnp.where(ci < (sj + 1) * SEL_BLOCK, jnp.where(ci + CMP_LEN > sj * SEL_BLOCK, 1.0, 0.0), 0.0).astype(BF16)


def _force_scores(score, blk, lag):
    recent = lax.bitcast_convert_type(lag, jnp.uint32) < N_LOCAL_SEL
    score = jnp.where(recent, FORCE_SCORE, jnp.where(lag >= 0, score, -FORCE_SCORE))
    return jnp.where(blk == 0, FORCE_SCORE, score)


def _cmpsel_prompt_kernel(q_ref, k_ref, v_ref, o_ref, sm_ref, *, tq, n_cmp, n_sel, n_keep):
    g = pl.program_id(1)
    i = pl.program_id(2)
    nc = k_ref.shape[1]
    k = k_ref[0]
    v = v_ref[0]
    qpos = i * tq + lax.broadcasted_iota(I32, (tq, nc), 0)
    cidx = lax.broadcasted_iota(I32, (tq, nc), 1)
    mask = jnp.where(cidx < n_cmp, cidx * CMP_STRIDE + CMP_LEN - 1, 2 ** 30) <= qpos
    center = (lax.broadcasted_iota(I32, (1, nc), 1) * CMP_STRIDE - i * tq).astype(F32) + 0.5 * (CMP_LEN - 1)
    imp = jnp.zeros((tq, nc), F32)
    for h in range(HPG_NSA):
        q = q_ref[0, :, h * HEAD_DIM:(h + 1) * HEAD_DIM]
        s = _nt_dot(q, k) * LN2 + _nsa_slope(g, h) * center
        s = jnp.where(mask, s, -1e30)
        e = jnp.where(mask, jnp.exp(s - jnp.max(s, axis=1, keepdims=True)), 0.0)
        p = e / jnp.maximum(jnp.sum(e, axis=1, keepdims=True), 1e-30)
        o_ref[0, :, h * HEAD_DIM:(h + 1) * HEAD_DIM] = jnp.dot(
            p.astype(BF16), v, preferred_element_type=F32).astype(o_ref.dtype)
        imp = imp + p
    ns_pad = -(-n_sel // LANES) * LANES
    ns8 = -(-n_sel // 8) * 8
    score_t = _split_dot(imp, _overlap_matrix(nc, ns_pad)).T[:ns8]
    blk = lax.broadcasted_iota(I32, (ns8, tq), 0)
    lag = (i * tq + lax.broadcasted_iota(I32, (ns8, tq), 1)) // SEL_BLOCK - blk
    score_t = jnp.where(blk < n_sel, _force_scores(score_t, blk, lag), -2.0 * FORCE_SCORE)
    chosen_t = jnp.concatenate([_rank_select_cols(score_t, n_keep), jnp.zeros((ns_pad - ns8, tq), F32)], axis=0)
    sm_ref[0, 0] = chosen_t.T[:, :n_sel].astype(sm_ref.dtype)


def cmpsel_prompt_call(pb, kvc, *, tq):
    b, t, _ = pb.shape
    nc = kvc.shape[1]
    n_cmp = nc - 1
    n_sel = -(-t // SEL_BLOCK)
    n_keep = min(N_SEL, n_sel)
    qcol = COL_QN // (HPG_NSA * HEAD_DIM)
    return pl.pallas_call(
        functools.partial(_cmpsel_prompt_kernel, tq=tq, n_cmp=n_cmp, n_sel=n_sel, n_keep=n_keep),
        grid=(b, G_NSA, t // tq),
        in_specs=[pl.BlockSpec((1, tq, HPG_NSA * HEAD_DIM), lambda bb, g, i: (bb, i, qcol + g)),
                  pl.BlockSpec((1, nc, HEAD_DIM), lambda bb, g, i: (bb, 0, g)),
                  pl.BlockSpec((1, nc, HEAD_DIM), lambda bb, g, i: (bb, 0, G_NSA + g))],
        out_specs=[pl.BlockSpec((1, tq, HPG_NSA * HEAD_DIM), lambda bb, g, i: (bb, i, g)),
                   pl.BlockSpec((1, 1, tq, n_sel), lambda bb, g, i: (bb, g, i, 0))],
        out_shape=[jax.ShapeDtypeStruct((b, t, W_NSA), BF16),
                   jax.ShapeDtypeStruct((b, G_NSA, t, n_sel), BF16)],
        compiler_params=_cparams(("arbitrary",) * 3),
        name="cmpsel_prompt",
    )(pb, kvc, kvc)


def _merge_kernel(ocmp_ref, osel_ref, owin_ref, ofox_ref, small_ref, x_ref, ggrp_ref, wout_ref, gpost_ref,
                  gt_ref, gpre_ref, sc_ref, sh_ref, wrh_ref, wrl_ref, br_ref, x1_ref, h2_ref, route_ref):
    gs = small_ref[0]
    parts = []
    for h in range(H_NSA):
        sl = slice(h * HEAD_DIM, (h + 1) * HEAD_DIM)
        parts.append(gs[:, h:h + 1] * ocmp_ref[0, :, sl].astype(F32)
                     + gs[:, H_NSA + h:H_NSA + h + 1] * osel_ref[0, :, sl].astype(F32)
                     + gs[:, 2 * H_NSA + h:2 * H_NSA + h + 1] * owin_ref[0, :, sl].astype(F32))
    o_nsa = jnp.concatenate(parts, axis=1)
    y = jnp.concatenate([_rms(o_nsa, ggrp_ref[:, :W_NSA]),
                         _rms(ofox_ref[0].astype(F32), ggrp_ref[:, W_NSA:])], axis=1).astype(BF16)
    m = jnp.dot(y, wout_ref[...], preferred_element_type=F32)
    x1 = x_ref[0] + gt_ref[0] * _rms(m, gpost_ref[...])
    x1_ref[0] = x1
    h2 = _rms(x1, gpre_ref[...]) * (1.0 + sc_ref[0]) + sh_ref[0]
    hi = h2.astype(BF16)
    h2_ref[0] = hi
    lo = (h2 - hi.astype(F32)).astype(BF16)
    logits = (jnp.dot(hi, wrh_ref[...], preferred_element_type=F32) + jnp.dot(hi, wrl_ref[...], preferred_element_type=F32)
              + jnp.dot(lo, wrh_ref[...], preferred_element_type=F32) + br_ref[...])
    lane = lax.broadcasted_iota(I32, logits.shape, 1)
    vals = jnp.where(lane < N_EXPERTS, logits, -jnp.inf)
    top_v, top_e = [], []
    for _ in range(TOP_K):
        mx = jnp.max(vals, axis=1, keepdims=True)
        idx = jnp.min(jnp.where(vals == mx, lane, LANES), axis=1, keepdims=True)
        top_v.append(mx)
        top_e.append(idx)
        vals = jnp.where(lane == idx, -jnp.inf, vals)
    ex = [jnp.exp(v - top_v[0]) for v in top_v]
    den = ex[0] + ex[1] + ex[2] + ex[3]
    route = jnp.zeros(logits.shape, F32)
    for kx in range(TOP_K):
        route = jnp.where(lane == kx, top_e[kx].astype(F32), route)
        route = jnp.where(lane == TOP_K + kx, ex[kx] / den, route)
    route_ref[0] = route


def merge_call(ocmp, osel, owin, ofox, small, x, ggrp, wout, gpost, gt, gpre, sc, sh, wrh, wrl, br, tm):
    b, t, d = x.shape
    per_row = sc.shape[1] != 1
    row = lambda w: pl.BlockSpec((1, tm, w), lambda bb, i: (bb, i, 0))
    const = lambda shape: pl.BlockSpec(shape, lambda bb, i: (0,) * len(shape))
    mod = row(d) if per_row else pl.BlockSpec((1, 1, d), lambda bb, i: (bb, 0, 0))
    return pl.pallas_call(
        _merge_kernel,
        grid=(b, t // tm),
        in_specs=[row(W_NSA), row(W_NSA), row(W_NSA), row(W_FOX), row(LANES), row(d),
                  const((1, d)), const((d, d)), const((1, d)), mod, const((1, d)), mod, mod,
                  const((d, LANES)), const((d, LANES)), const((1, LANES))],
        out_specs=[row(d), row(d), row(LANES)],
        out_shape=[jax.ShapeDtypeStruct((b, t, d), F32), jax.ShapeDtypeStruct((b, t, d), BF16),
                   jax.ShapeDtypeStruct((b, t, LANES), F32)],
        compiler_params=_cparams(("arbitrary", "arbitrary")),
        name="merge",
    )(ocmp, osel, owin, ofox, small, x, ggrp, wout, gpost, gt, gpre, sc, sh, wrh, wrl, br)


MOE_TM = 512
MOE_TF = 1024
MOE_TN = 1024


def _expert_changed(blk_e_ref, i):
    return (i == 0) | (blk_e_ref[i] != blk_e_ref[jnp.maximum(i - 1, 0)])


def _moe_up_kernel(blk_e_ref, nused_ref, x_ref, wg_ref, wl_ref, bg_ref, bl_ref, a_ref, wg_bf, wl_bf):
    i = pl.program_id(1)

    @pl.when(_expert_changed(blk_e_ref, i))
    def _():
        wg_bf[...] = wg_ref[0].astype(BF16)
        wl_bf[...] = wl_ref[0].astype(BF16)

    @pl.when(i < nused_ref[0])
    def _():
        x = x_ref[...]
        ug = jnp.dot(x, wg_bf[...], preferred_element_type=F32) + bg_ref[0]
        ul = jnp.dot(x, wl_bf[...], preferred_element_type=F32) + bl_ref[0]
        glu = jnp.minimum(ug, SWIGLU_LIMIT)
        lin = jnp.clip(ul, -SWIGLU_LIMIT, SWIGLU_LIMIT)
        a_ref[...] = (glu * jax.nn.sigmoid(SWIGLU_ALPHA * glu) * (lin + 1.0)).astype(a_ref.dtype)

    @pl.when(i >= nused_ref[0])
    def _():
        a_ref[...] = jnp.zeros(a_ref.shape, a_ref.dtype)


def moe_up_call(blk_e, n_used, xs, w_up, b_up):
    n_slots, d = xs.shape
    n_blk = n_slots // MOE_TM
    nf = D_FF // MOE_TF
    grid_spec = pltpu.PrefetchScalarGridSpec(
        num_scalar_prefetch=2,
        grid=(nf, n_blk),
        in_specs=[pl.BlockSpec((MOE_TM, d), lambda f, i, be, nu: (i, 0)),
                  pl.BlockSpec((1, d, MOE_TF), lambda f, i, be, nu: (be[i], 0, f)),
                  pl.BlockSpec((1, d, MOE_TF), lambda f, i, be, nu: (be[i], 0, nf + f)),
                  pl.BlockSpec((1, 1, MOE_TF), lambda f, i, be, nu: (be[i], 0, f)),
                  pl.BlockSpec((1, 1, MOE_TF), lambda f, i, be, nu: (be[i], 0, nf + f))],
        out_specs=pl.BlockSpec((MOE_TM, MOE_TF), lambda f, i, be, nu: (i, f)),
        scratch_shapes=[pltpu.VMEM((d, MOE_TF), BF16), pltpu.VMEM((d, MOE_TF), BF16)])
    return pl.pallas_call(
        _moe_up_kernel,
        grid_spec=grid_spec,
        out_shape=jax.ShapeDtypeStruct((n_slots, D_FF), BF16),
        compiler_params=_cparams(("arbitrary", "arbitrary")),
        name="moe_up",
    )(blk_e, n_used, xs, w_up, w_up, b_up, b_up)


def _moe_down_kernel(blk_e_ref, nused_ref, a_ref, wd_ref, bd_ref, sw_ref, y_ref, wd_bf):
    i = pl.program_id(1)

    @pl.when(_expert_changed(blk_e_ref, i))
    def _():
        wd_bf[...] = wd_ref[0].astype(BF16)

    @pl.when(i < nused_ref[0])
    def _():
        y = jnp.dot(a_ref[...], wd_bf[...], preferred_element_type=F32) + bd_ref[0]
        y_ref[...] = (y * sw_ref[...]).astype(y_ref.dtype)

    @pl.when(i >= nused_ref[0])
    def _():
        y_ref[...] = jnp.zeros(y_ref.shape, y_ref.dtype)


def moe_down_call(blk_e, n_used, a, w_down, b_down, slot_w, col0, ncol):
    n_slots, dff = a.shape
    n_blk = n_slots // MOE_TM
    grid_spec = pltpu.PrefetchScalarGridSpec(
        num_scalar_prefetch=2,
        grid=(ncol, n_blk),
        in_specs=[pl.BlockSpec((MOE_TM, dff), lambda c, i, be, nu: (i, 0)),
                  pl.BlockSpec((1, dff, MOE_TN), lambda c, i, be, nu: (be[i], 0, col0 + c)),
                  pl.BlockSpec((1, 1, MOE_TN), lambda c, i, be, nu: (be[i], 0, col0 + c)),
                  pl.BlockSpec((MOE_TM, 1), lambda c, i, be, nu: (i, 0))],
        out_specs=pl.BlockSpec((MOE_TM, MOE_TN), lambda c, i, be, nu: (i, c)),
        scratch_shapes=[pltpu.VMEM((dff, MOE_TN), BF16)])
    return pl.pallas_call(
        _moe_down_kernel,
        grid_spec=grid_spec,
        out_shape=jax.ShapeDtypeStruct((n_slots, ncol * MOE_TN), BF16),
        compiler_params=_cparams(("arbitrary", "arbitrary")),
        name="moe_down",
    )(blk_e, n_used, a, w_down, b_down, slot_w)


def moe_dispatch(route):
    n = route.shape[0]
    n_asg = n * TOP_K
    n_blk = -(-(n_asg + N_EXPERTS * (MOE_TM - 1)) // MOE_TM)
    e_flat = route[:, :TOP_K].astype(I32).reshape(-1)
    w_flat = route[:, TOP_K:2 * TOP_K].reshape(-1)
    order = jnp.argsort(e_flat).astype(I32)
    rank = jnp.argsort(order).astype(I32)
    grp_end = jnp.searchsorted(e_flat[order], jnp.arange(N_EXPERTS, dtype=I32), side="right").astype(I32)
    grp_start = jnp.concatenate([jnp.zeros((1,), I32), grp_end[:-1]])
    counts = grp_end - grp_start
    padded = (counts + MOE_TM - 1) // MOE_TM * MOE_TM
    pad_end = jnp.cumsum(padded)
    pad_start = pad_end - padded
    blk_e = jnp.minimum(jnp.searchsorted(pad_end, jnp.arange(n_blk, dtype=I32) * MOE_TM, side="right"),
                        N_EXPERTS - 1).astype(I32)
    n_used = (pad_end[-1:] // MOE_TM).astype(I32)
    slot_e = jnp.repeat(blk_e, MOE_TM)
    off = jnp.arange(n_blk * MOE_TM, dtype=I32) - pad_start[slot_e]
    live = off < counts[slot_e]
    src = order[jnp.clip(grp_start[slot_e] + off, 0, n_asg - 1)]
    slot_tok = jnp.where(live, src // TOP_K, 0)
    slot_w = jnp.where(live, w_flat[src], 0.0)
    dest = (pad_start[e_flat] + rank - grp_start[e_flat]).reshape(n, TOP_K).T
    return slot_tok, slot_w[:, None], dest, blk_e, n_used


def _final_kernel(*refs):
    *yg_refs, x1_ref, gt_ref, gpost_ref, o_ref = refs
    parts = []
    for yg_ref in yg_refs:
        s = yg_ref[0, 0].astype(F32)
        for kx in range(1, TOP_K):
            s = s + yg_ref[kx, 0].astype(F32)
        parts.append(s)
    o_ref[0] = x1_ref[0] + gt_ref[0] * _rms(jnp.concatenate(parts, axis=1), gpost_ref[...])


def final_call(ygs, x1, gt, gpost, tm):
    b, t, d = x1.shape
    per_row = gt.shape[1] != 1
    mod = (pl.BlockSpec((1, tm, d), lambda bb, i: (bb, i, 0)) if per_row
           else pl.BlockSpec((1, 1, d), lambda bb, i: (bb, 0, 0)))
    return pl.pallas_call(
        _final_kernel,
        grid=(b, t // tm),
        in_specs=[pl.BlockSpec((TOP_K, 1, tm, yg.shape[-1]), lambda bb, i: (0, bb, i, 0)) for yg in ygs]
                 + [pl.BlockSpec((1, tm, d), lambda bb, i: (bb, i, 0)),
                    mod, pl.BlockSpec((1, d), lambda bb, i: (0, 0))],
        out_specs=pl.BlockSpec((1, tm, d), lambda bb, i: (bb, i, 0)),
        out_shape=jax.ShapeDtypeStruct((b, t, d), F32),
        compiler_params=_cparams(("arbitrary", "arbitrary")),
        name="final",
    )(*ygs, x1, gt, gpost)


def prep_weights(w_in, b_nsa_gate, b_fox_forget, w_cmp1, w_out, w_router, b_router):
    d = w_in.shape[0]
    o_qn, o_kvn, o_gn, o_qf, o_kvf, o_fl = 0, 1024, 2560, 2584, 3608, 4632
    wm = jnp.concatenate([w_in[:, o_kvn:o_kvn + 1536], w_in[:, o_kvf:o_kvf + 1024],
                          w_in[:, o_qn:o_qn + 1024], w_in[:, o_qf:o_qf + 1024]], axis=1).astype(BF16)
    pad = LANES - 3 * H_NSA - H_FOX
    ws = jnp.concatenate([w_in[:, o_gn:o_gn + 3 * H_NSA], w_in[:, o_fl:o_fl + H_FOX],
                          jnp.zeros((d, pad), F32)], axis=1).astype(BF16)
    bs = jnp.concatenate([b_nsa_gate, b_fox_forget, jnp.zeros((pad,), F32)])[None, :]
    w1p = w_cmp1.reshape(2, CMP_STRIDE, 2 * HEAD_DIM, CMP_HID).astype(BF16)
    wr = jnp.concatenate([w_router, jnp.zeros((d, LANES - N_EXPERTS), F32)], axis=1)
    wrh = wr.astype(BF16)
    wrl = (wr - wrh.astype(F32)).astype(BF16)
    br = jnp.concatenate([b_router, jnp.zeros((LANES - N_EXPERTS,), F32)])[None, :]
    half = CMP_STRIDE // 2
    w1f = jnp.concatenate([w1p[:, :half], w1p[:, half:]], axis=-1)
    return dict(wm=wm, ws=ws, bs=bs, w1p=w1p, w1f=w1f, wout=w_out.astype(BF16), wrh=wrh, wrl=wrl, br=br)


def prompt_mixer(x, sc, sh, g_pre, W, cmpw, *, tm, tq, tk):
    b, t, _ = x.shape
    pb, kvcs, win, kvf, small = inproj_call(x, g_pre, sc, sh, W["wm"], W["ws"], W["bs"], tm)
    kvc = compress_prompt_call(kvcs, *cmpw)
    ocmp, selmask = cmpsel_prompt_call(pb, kvc, tq=tq)
    osel = flash_call("sel", pb, tq=tq, tk=tk, extra=(selmask,))
    owin = flash_call("win", pb, tq=tq, tk=tk)
    logf = small[:, :, 3 * H_NSA:3 * H_NSA + H_FOX]
    cum = cumsum_call(jnp.swapaxes(logf, 1, 2))
    c8 = jnp.pad(cum.reshape(b, KV_FOX, HPG_FOX, t), ((0, 0), (0, 0), (0, 8 - HPG_FOX), (0, 0)))
    ofox = flash_call("fox", pb, tq=tq, tk=tk, extra=(c8,))
    return ocmp, osel, owin, ofox, small, kvcs, win, kvf, logf


PAGE = 128
CMP_PAGES = 16
CMP_PITCH = 136
FOX_PAGES = 16
NS_PAD = 384


def _head_rows(qf, g):
    rows = [qf[:, (g * HPG_NSA + h) * HEAD_DIM:(g * HPG_NSA + h + 1) * HEAD_DIM] for h in range(HPG_NSA)]
    return jnp.concatenate(rows + [jnp.zeros((8 - HPG_NSA, HEAD_DIM), F32)], axis=0).astype(BF16)


def _slope_col(g):
    row = lax.broadcasted_iota(I32, (8, 1), 0)
    col = jnp.zeros((8, 1), F32)
    for h in range(HPG_NSA):
        col = jnp.where(row == h, 2.0 ** -(g * HPG_NSA + h + 1), col)
    return col


def _masked_softmax(s, mask):
    s = jnp.where(mask, s, -1e30)
    e = jnp.where(mask, jnp.exp(s - jnp.max(s, axis=1, keepdims=True)), 0.0)
    return e / jnp.maximum(jnp.sum(e, axis=1, keepdims=True), 1e-30)


def _store_heads(o_ref, o, g):
    for h in range(HPG_NSA):
        c = (g * HPG_NSA + h) * HEAD_DIM
        o_ref[0, :, c:c + HEAD_DIM] = o[h:h + 1].astype(o_ref.dtype)


def _compress_sample_kernel(pt_ref, *refs, npg):
    pages = refs[:npg + 1]
    w1_ref, hb_ref, w2_ref, b2_ref, o_ref, x_scr = refs[npg + 1:]
    per_page = PAGE // CMP_STRIDE
    n = (npg + 1) * per_page
    chunk_rows = CMP_STRIDE * 8
    for p in range(npg + 1):
        for c in range(per_page):
            x_scr[pl.ds((p * per_page + c) * CMP_PITCH, chunk_rows), :] = pages[p][0, c * chunk_rows:(c + 1) * chunk_rows, :]
    load = lambda cb, s: x_scr[pl.ds(s * 8 + cb, n, stride=CMP_PITCH), :]
    r = _compress_rows(load, n, w1_ref, hb_ref, w2_ref, b2_ref)
    o_ref[0] = r[:npg * per_page].astype(o_ref.dtype)


def compress_sample_call(page_table, cache3, w1f, hb, w2, b2):
    bsz, n_pages = page_table.shape
    npg = CMP_PAGES
    assert n_pages % npg == 0
    page_specs = [pl.BlockSpec((1, PAGE * 8, HEAD_DIM),
                               lambda bb, gi, pt, p=p: (pt[bb, jnp.minimum(gi * npg + p, n_pages - 1)], 0, 0))
                  for p in range(npg + 1)]
    rows_out = npg * PAGE // CMP_STRIDE
    grid_spec = pltpu.PrefetchScalarGridSpec(
        num_scalar_prefetch=1,
        grid=(bsz, n_pages // npg),
        in_specs=page_specs + _cmp_weight_specs(),
        out_specs=pl.BlockSpec((1, rows_out, 512), lambda bb, gi, pt: (bb, gi, 0)),
        scratch_shapes=[pltpu.VMEM(((npg + 1) * (PAGE // CMP_STRIDE) * CMP_PITCH, HEAD_DIM), F32)])
    return pl.pallas_call(
        functools.partial(_compress_sample_kernel, npg=npg),
        grid_spec=grid_spec,
        out_shape=jax.ShapeDtypeStruct((bsz, n_pages * PAGE // CMP_STRIDE, 512), BF16),
        compiler_params=_cparams(("arbitrary", "arbitrary")),
        name="compress_sample",
    )(page_table, *([cache3] * (npg + 1)), w1f, hb, w2, b2)


def _cmpsel_sample_kernel(q_ref, kv_ref, o_ref, idx_ref, *, past, n_cmp, n_keep):
    nc = kv_ref.shape[1]
    qf = q_ref[0].astype(F32)
    cidx = lax.broadcasted_iota(I32, (1, nc), 1)
    mask = jnp.where(cidx < n_cmp, cidx * CMP_STRIDE + CMP_LEN - 1, 2 ** 30) <= past
    center = (cidx * CMP_STRIDE - past).astype(F32) + 0.5 * (CMP_LEN - 1)
    overlap = _overlap_matrix(nc, NS_PAD)
    blk = lax.broadcasted_iota(I32, (8, NS_PAD), 1)
    lag = past // SEL_BLOCK - blk
    ii =lax.broadcasted_iota(I32, (NS_PAD, NS_PAD), 0)
    jj = lax.broadcasted_iota(I32, (NS_PAD, NS_PAD), 1)
    slot = lax.broadcasted_iota(I32, (NS_PAD, LANES), 1).astype(F32)
    rows = []
    for g in range(G_NSA):
        k = kv_ref[0, :, g * HEAD_DIM:(g + 1) * HEAD_DIM]
        v = kv_ref[0, :, (G_NSA + g) * HEAD_DIM:(G_NSA + g + 1) * HEAD_DIM]
        s = _nt_dot(_head_rows(qf, g), k) * LN2 + _slope_col(g) * center
        p = _masked_softmax(s, jnp.broadcast_to(mask, s.shape))
        _store_heads(o_ref, jnp.dot(p.astype(BF16), v, preferred_element_type=F32), g)
        imp = jnp.sum(p[0:HPG_NSA], axis=0, keepdims=True)
        score = _split_dot(jnp.broadcast_to(imp, (8, nc)), overlap)
        score = _force_scores(score, blk, lag)
        row = score[0:1, :]
        col = score.T[:, 0:1]
        ge = jnp.where(col >= row, 1.0, 0.0)
        gt = jnp.where(col > row, 1.0, 0.0)
        rank_row = jnp.sum(jnp.where(ii < jj, ge, gt), axis=0, keepdims=True)
        rank_col = jnp.sum(jnp.where(jj < ii, 1.0 - gt, 1.0 - ge), axis=1, keepdims=True)
        sel_row = jnp.where(rank_row < n_keep, 1.0, 0.0)
        sel_col = jnp.where(rank_col < n_keep, 1.0, 0.0)
        before = jnp.sum(jnp.where(jj < ii, sel_row, 0.0), axis=1, keepdims=True)
        pick = jnp.where(before == slot, sel_col * ii[:, 0:1].astype(F32), 0.0)
        rows.append(jnp.sum(pick, axis=0, keepdims=True))
    idx_ref[0] = jnp.concatenate(rows + [jnp.zeros((8 - G_NSA, LANES), F32)], axis=0).astype(I32)


def cmpsel_sample_call(qn, kvc, *, past):
    bsz = qn.shape[0]
    nc = kvc.shape[1]
    n_cmp = (past + 1) // CMP_STRIDE - 1
    n_sel = -(-(past + 1) // SEL_BLOCK)
    assert n_sel <= NS_PAD and n_cmp <= nc
    n_keep = min(N_SEL, n_sel)
    return pl.pallas_call(
        functools.partial(_cmpsel_sample_kernel, past=past, n_cmp=n_cmp, n_keep=n_keep),
        grid=(bsz,),
        in_specs=[pl.BlockSpec((1, 1, W_NSA), lambda bb: (bb, 0, 0)),
                  pl.BlockSpec((1, nc, 512), lambda bb: (bb, 0, 0))],
        out_specs=[pl.BlockSpec((1, 1, W_NSA), lambda bb: (bb, 0, 0)),
                   pl.BlockSpec((1, 8, LANES), lambda bb: (bb, 0, 0))],
        out_shape=[jax.ShapeDtypeStruct((bsz, 1, W_NSA), BF16), jax.ShapeDtypeStruct((bsz, 8, LANES), I32)],
        compiler_params=_cparams(("arbitrary",)),
        name="cmpsel_sample",
    )(qn, kvc)


def _sel_sample_kernel(idx_ref, pt_ref, q_ref, new_ref, cache_ref, o_ref, kvbuf, sem, *, past, n_pages, n_keep):
    bb = pl.program_id(0)
    per_page = PAGE // SEL_BLOCK
    blk_rows = SEL_BLOCK * 8
    copies = []
    for g in range(G_NSA):
        for kx in range(n_keep):
            j = idx_ref[(bb * G_NSA + g) * n_keep + kx]
            page = pt_ref[bb, jnp.minimum(j // per_page, n_pages - 1)]
            r0 = pl.multiple_of((j % per_page) * blk_rows, blk_rows)
            cp = pltpu.make_async_copy(cache_ref.at[page, pl.ds(r0, blk_rows), :], kvbuf.at[g, kx], sem.at[0])
            cp.start()
            copies.append(cp)
    for cp in copies:
        cp.wait()
    qf = q_ref[0].astype(F32)
    nkeys = n_keep * SEL_BLOCK
    lane = lax.broadcasted_iota(I32, (1, nkeys), 1)
    row0 = lax.broadcasted_iota(I32, (SEL_BLOCK, HEAD_DIM), 0) == 0
    for g in range(G_NSA):
        kpos = lane % SEL_BLOCK
        ks, vs = [], []
        for kx in range(n_keep):
            j = idx_ref[(bb * G_NSA + g) * n_keep + kx]
            is_new = j * SEL_BLOCK >= past
            fresh = jnp.logical_and(is_new, row0)
            ks.append(jnp.where(fresh, new_ref[0, 4 + g:5 + g, :], kvbuf[g, kx, pl.ds(4 + g, SEL_BLOCK, stride=8), :]))
            vs.append(jnp.where(fresh, new_ref[0, 6 + g:7 + g, :], kvbuf[g, kx, pl.ds(6 + g, SEL_BLOCK, stride=8), :]))
            kpos = kpos + jnp.where(lane // SEL_BLOCK == kx, j * SEL_BLOCK, 0)
        k = jnp.concatenate(ks, axis=0).astype(BF16)
        v = jnp.concatenate(vs, axis=0).astype(BF16)
        s = _nt_dot(_head_rows(qf, g), k) * LN2 + _slope_col(g) * (kpos - past).astype(F32)
        p = _masked_softmax(s, jnp.broadcast_to(kpos <= past, s.shape))
        _store_heads(o_ref, jnp.dot(p.astype(BF16), v, preferred_element_type=F32), g)


def sel_sample_call(idx_flat, page_table, qn, kvcs_new, cache3, *, past, n_keep):
    bsz, n_pages = page_table.shape
    assert past % SEL_BLOCK == 0 and past == n_pages * PAGE
    grid_spec = pltpu.PrefetchScalarGridSpec(
        num_scalar_prefetch=2,
        grid=(bsz,),
        in_specs=[pl.BlockSpec((1, 1, W_NSA), lambda bb, ix, pt: (bb, 0, 0)),
                  pl.BlockSpec((1, 8, HEAD_DIM), lambda bb, ix, pt: (bb, 0, 0)),
                  pl.BlockSpec(memory_space=pl.ANY)],
        out_specs=pl.BlockSpec((1, 1, W_NSA), lambda bb, ix, pt: (bb, 0, 0)),
        scratch_shapes=[pltpu.VMEM((G_NSA, n_keep, SEL_BLOCK * 8, HEAD_DIM), F32),
                        pltpu.SemaphoreType.DMA((1,))])
    return pl.pallas_call(
        functools.partial(_sel_sample_kernel, past=past, n_pages=n_pages, n_keep=n_keep),
        grid_spec=grid_spec,
        out_shape=jax.ShapeDtypeStruct((bsz, 1, W_NSA), BF16),
        compiler_params=_cparams(("arbitrary",)),
        name="sel_sample",
    )(idx_flat, page_table, qn, kvcs_new, cache3)


def _win_sample_kernel(q_ref, buf_ref, new_ref, o_ref, nb_ref):
    rows = buf_ref.shape[1]
    wb = rows // 4
    nb_ref[0] = pltpu.roll(buf_ref[0], rows - 4, axis=0)
    nb_ref[0, rows - 4:rows, :] = new_ref[0]
    qf = q_ref[0].astype(F32)
    krel = (lax.broadcasted_iota(I32, (1, wb), 1) - (wb - 1)).astype(F32)
    for g in range(G_NSA):
        k = nb_ref[0, pl.ds(g, wb, stride=4), :].astype(BF16)
        v = nb_ref[0, pl.ds(G_NSA + g, wb, stride=4), :].astype(BF16)
        s = _nt_dot(_head_rows(qf, g), k) * LN2 + _slope_col(g) * krel
        p = _masked_softmax(s, jnp.full(s.shape, True))
        _store_heads(o_ref, jnp.dot(p.astype(BF16), v, preferred_element_type=F32), g)


def win_sample_call(qn, win_buf, win_new):
    bsz, rows, _ = win_buf.shape
    assert rows == WINDOW * 4
    return pl.pallas_call(
        _win_sample_kernel,
        grid=(bsz,),
        in_specs=[pl.BlockSpec((1, 1, W_NSA), lambda bb: (bb, 0, 0)),
                  pl.BlockSpec((1, rows, HEAD_DIM), lambda bb: (bb, 0, 0)),
                  pl.BlockSpec((1, 4, HEAD_DIM), lambda bb: (bb, 0, 0))],
        out_specs=[pl.BlockSpec((1, 1, W_NSA), lambda bb: (bb, 0, 0)),
                   pl.BlockSpec((1, rows, HEAD_DIM), lambda bb: (bb, 0, 0))],
        out_shape=[jax.ShapeDtypeStruct((bsz, 1, W_NSA), BF16), jax.ShapeDtypeStruct((bsz, rows, HEAD_DIM), F32)],
        compiler_params=_cparams(("arbitrary",)),
        name="win_sample",
    )(qn, win_buf, win_new)


def _fox_sample_kernel(pt_ref, *refs, npg, n_steps):
    kv_pages = refs[:npg]
    lf_pages = refs[npg:2 * npg]
    q_ref, new_ref, lfn_ref, o_ref, qbd_scr, m_scr, l_scr, acc_scr, carry_scr = refs[2 * npg:]
    st = pl.program_id(1)
    kvw = KV_FOX * HEAD_DIM

    @pl.when(st == 0)
    def _():
        qf = q_ref[0].astype(F32)
        zero = jnp.zeros((1, HEAD_DIM), F32)
        rows = [jnp.concatenate([qf[:, h * HEAD_DIM:(h + 1) * HEAD_DIM] if c == h // HPG_FOX else zero
                                 for c in range(KV_FOX)], axis=1) for h in range(H_FOX)]
        qbd_scr[...] = jnp.concatenate(rows, axis=0)
        m_scr[...] = jnp.full(m_scr.shape, M_FLOOR, F32)
        l_scr[...] = jnp.zeros(l_scr.shape, F32)
        acc_scr[...] = jnp.zeros(acc_scr.shape, F32)
        carry_scr[...] = lfn_ref[0]

    qbd = qbd_scr[...].astype(BF16)
    tri = jnp.where(lax.broadcasted_iota(I32, (PAGE, PAGE), 0) <= lax.broadcasted_iota(I32, (PAGE, PAGE), 1),
                    1.0, 0.0).astype(BF16)
    carry = carry_scr[...]
    parts = []
    def heads(ref, base):
        n = ref.shape[1] // 8
        return jnp.concatenate([ref[0, pl.ds(base + h, n, stride=8), :] for h in range(KV_FOX)], axis=1)

    for p in range(npg):
        k = heads(kv_pages[p], 0).astype(BF16)
        incl = _split_dot(lf_pages[p][0], tri)
        tot = incl[:, PAGE - 1:PAGE]
        parts.append(_nt_dot(qbd, k) * LN2 + (carry + (tot - incl)))
        carry = carry + tot
    carry_scr[...] = carry
    s = jnp.concatenate(parts, axis=1)
    m_prev = m_scr[...]
    m_new = jnp.maximum(m_prev, jnp.max(s, axis=1, keepdims=True))
    alpha = jnp.exp(m_prev - m_new)
    pr = jnp.exp(s - m_new)
    l_scr[...] = alpha * l_scr[...] + jnp.sum(pr, axis=1, keepdims=True)
    acc = alpha * acc_scr[...]
    for p in range(npg):
        v = heads(kv_pages[p], KV_FOX).astype(BF16)
        acc = acc + jnp.dot(pr[:, p * PAGE:(p + 1) * PAGE].astype(BF16), v, preferred_element_type=F32)
    acc_scr[...] = acc
    m_scr[...] = m_new

    @pl.when(st == n_steps - 1)
    def _():
        kn = heads(new_ref, 0)
        vn = heads(new_ref, KV_FOX)
        s_new = jnp.sum(qbd_scr[...] * kn, axis=1, keepdims=True) * LN2
        m_fin = jnp.maximum(m_scr[...], s_new)
        a2 = jnp.exp(m_scr[...] - m_fin)
        p_new = jnp.exp(s_new - m_fin)
        o = (a2 * acc_scr[...] + p_new * vn) / jnp.maximum(a2 * l_scr[...] + p_new, 1e-30)
        for h in range(H_FOX):
            c = (h // HPG_FOX) * HEAD_DIM
            o_ref[0, :, h * HEAD_DIM:(h + 1) * HEAD_DIM] = o[h:h + 1, c:c + HEAD_DIM].astype(o_ref.dtype)


def fox_sample_call(page_table, qf, kvf_new, lf_new, cache3, logf_t):
    bsz, n_pages = page_table.shape
    npg = FOX_PAGES
    assert n_pages % npg == 0
    n_steps = n_pages // npg
    page_of = lambda bb, st, pt, p: pt[bb, n_pages - 1 - (st * npg + p)]
    kv_specs = [pl.BlockSpec((1, PAGE * 8, HEAD_DIM), lambda bb, st, pt, p=p: (page_of(bb, st, pt, p), 0, 0))
                for p in range(npg)]
    lf_specs = [pl.BlockSpec((1, H_FOX, PAGE), lambda bb, st, pt, p=p: (page_of(bb, st, pt, p), 0, 0))
                for p in range(npg)]
    grid_spec = pltpu.PrefetchScalarGridSpec(
        num_scalar_prefetch=1,
        grid=(bsz, n_steps),
        in_specs=kv_specs + lf_specs + [pl.BlockSpec((1, 1, W_FOX), lambda bb, st, pt: (bb, 0, 0)),
                                        pl.BlockSpec((1, 8, HEAD_DIM), lambda bb, st, pt: (bb, 0, 0)),
                                        pl.BlockSpec((1, H_FOX, 1), lambda bb, st, pt: (bb, 0, 0))],
        out_specs=pl.BlockSpec((1, 1, W_FOX), lambda bb, st, pt: (bb, 0, 0)),
        scratch_shapes=[pltpu.VMEM((H_FOX, KV_FOX * HEAD_DIM), F32), pltpu.VMEM((H_FOX, 1), F32),
                        pltpu.VMEM((H_FOX, 1), F32), pltpu.VMEM((H_FOX, KV_FOX * HEAD_DIM), F32),
                        pltpu.VMEM((H_FOX, 1), F32)])
    return pl.pallas_call(
        functools.partial(_fox_sample_kernel, npg=npg, n_steps=n_steps),
        grid_spec=grid_spec,
        out_shape=jax.ShapeDtypeStruct((bsz, 1, W_FOX), BF16),
        compiler_params=_cparams(("arbitrary", "arbitrary")),
        name="fox_sample",
    )(page_table, *([cache3] * npg), *([logf_t] * npg), qf, kvf_new, lf_new)


def sample_mixer(x, sc, sh, g_pre, W, cmpw, cache_nsa3, win_buf, cache_fox3, logf_t, page_table):
    bsz = x.shape[1]
    n_pages = page_table.shape[1]
    past = n_pages * PAGE
    pb, kvcs, win, kvf, small = inproj_call(x, g_pre, sc, sh, W["wm"], W["ws"], W["bs"], bsz)
    per_seq = lambda a: a.reshape(bsz, -1, a.shape[-1])
    pb, kvcs, win, kvf, small = map(per_seq, (pb, kvcs, win, kvf, small))
    qn = pb[:, :, COL_QN:COL_QN + W_NSA]
    qf = pb[:, :, COL_QF:COL_QF + W_FOX]
    kvc = compress_sample_call(page_table, cache_nsa3, *cmpw)
    ocmp, idx = cmpsel_sample_call(qn, kvc, past=past)
    n_keep = min(N_SEL, -(-(past + 1) // SEL_BLOCK))
    idx_flat = idx[:, :G_NSA, :n_keep].reshape(-1)
    osel = sel_sample_call(idx_flat, page_table, qn, kvcs, cache_nsa3, past=past, n_keep=n_keep)
    owin, win_new = win_sample_call(qn, win_buf, win)
    logf = small[:, :, 3 * H_NSA:3 * H_NSA + H_FOX]
    ofox = fox_sample_call(page_table, qf, kvf, logf.reshape(bsz, H_FOX, 1), cache_fox3, logf_t)
    to_rows = lambda a: a.reshape(1, bsz, a.shape[-1])
    return tuple(map(to_rows, (ocmp, osel, owin, ofox, small))) + (kvcs, win_new, kvf, logf)


PROMPT_TM = 512
ATTN_TQ = 512
ATTN_TK = 512


def kernel(x_prompt, x_sample, c_prompt, c_sample, cache_nsa_kv, state_nsa_win, cache_fox_kv, cache_fox_logf,
           page_table, w_ada, b_ada, g_pre_mix, g_post_mix, g_pre_ffn, g_post_ffn, w_in, b_nsa_gate, b_fox_forget,
           w_cmp1, b_cmp1, w_cmp2, b_cmp2, pe_cmp, g_grp, w_out, w_router, b_router, w_up, b_up, w_down, b_down):
    assert w_ada.shape[0] == 1 and x_sample.shape[1] == 1
    bp, t, d = x_prompt.shape
    bs = x_sample.shape[0]
    n_phys = cache_nsa_kv.shape[1]
    W = prep_weights(w_in[0], b_nsa_gate[0], b_fox_forget[0], w_cmp1[0], w_out[0], w_router[0], b_router[0])
    hb = cmp_bias_call(pe_cmp[0].reshape(2, CMP_LEN // 2, 2 * HEAD_DIM), W["w1p"], b_cmp1[0])
    cmpw = (W["w1f"], hb, w_cmp2[0].astype(BF16), b_cmp2[0])
    row = lambda a: a[0][None, :]

    ada = ada_call(jnp.concatenate([c_prompt, c_sample], axis=0), w_ada[0], row(b_ada))
    mods_p = [m[:, None, :] for m in jnp.split(ada[:bp], 6, axis=-1)]
    mods_s = [m[None] for m in jnp.split(ada[bp:], 6, axis=-1)]
    xs = x_sample.reshape(1, bs, d)

    ocmp, osel, owin, ofox, small, kvcs_p, win_p, kvf_p, logf_p = prompt_mixer(
        x_prompt, mods_p[1], mods_p[0], row(g_pre_mix), W, cmpw,
        tm=PROMPT_TM, tq=ATTN_TQ, tk=ATTN_TK)
    merge_args = (row(g_grp), W["wout"], row(g_post_mix))
    ffn_args = (W["wrh"], W["wrl"], W["br"])
    x1_p, h2_p, route_p = merge_call(ocmp, osel, owin, ofox, small, x_prompt, *merge_args, mods_p[2],
                                     row(g_pre_ffn), mods_p[4], mods_p[3], *ffn_args, 256)

    cache_nsa3 = cache_nsa_kv.reshape(n_phys, PAGE * 8, HEAD_DIM)
    cache_fox3 = cache_fox_kv.reshape(n_phys, PAGE * 8, HEAD_DIM)
    logf_t = jnp.swapaxes(cache_fox_logf[0], 1, 2)
    ocmp_s, osel_s, owin_s, ofox_s, small_s, kvcs_s, win_s, kvf_s, logf_s = sample_mixer(
        xs, mods_s[1], mods_s[0], row(g_pre_mix), W, cmpw,
        cache_nsa3, state_nsa_win.reshape(bs, WINDOW * 4, HEAD_DIM), cache_fox3, logf_t, page_table)
    x1_s, h2_s, route_s = merge_call(ocmp_s, osel_s, owin_s, ofox_s, small_s, xs, *merge_args, mods_s[2],
                                     row(g_pre_ffn), mods_s[4], mods_s[3], *ffn_args, bs)

    n_p = bp * t
    h2_all = jnp.concatenate([h2_p.reshape(n_p, d), h2_s.reshape(bs, d)], axis=0)
    route_all = jnp.concatenate([route_p.reshape(n_p, LANES), route_s.reshape(bs, LANES)], axis=0)[:, :2 * TOP_K]
    slot_tok, slot_w, dest, blk_e, n_used = moe_dispatch(route_all)
    a = moe_up_call(blk_e, n_used, h2_all[slot_tok], w_up[0], b_up[0][:, None, :])
    ncol = d // MOE_TN
    ys = [moe_down_call(blk_e, n_used, a, w_down[0], b_down[0][:, None, :], slot_w, c, 1) for c in range(ncol)]
    yg_p = [y[dest[:, :n_p]].reshape(TOP_K, bp, t, MOE_TN) for y in ys]
    yg_s = [y[dest[:, n_p:]].reshape(TOP_K, 1, bs, MOE_TN) for y in ys]
    y_p = final_call(yg_p, x1_p, mods_p[5], row(g_post_ffn), PROMPT_TM)
    y_s = final_call(yg_s, x1_s, mods_s[5], row(g_post_ffn), bs)

    wlen = min(WINDOW, t)
    return (y_p, y_s.reshape(bs, 1, d),
            kvcs_p.reshape(1, bp, t, 2, 2, G_NSA, HEAD_DIM),
            win_p[:, (t - wlen) * 4:].reshape(1, bp, wlen, 2, G_NSA, HEAD_DIM),
            kvf_p.reshape(1, bp, t, 2, KV_FOX, HEAD_DIM),
            logf_p[None],
            kvcs_s.reshape(1, bs, 1, 2, 2, G_NSA, HEAD_DIM),
            win_s.reshape(1, bs, WINDOW, 2, G_NSA, HEAD_DIM),
            kvf_s.reshape(1, bs, 1, 2, KV_FOX, HEAD_DIM),
            logf_s.reshape(1, bs, 1, H_FOX))
```

```python
import functools

import jax
import jax.numpy as jnp
from jax import lax
from jax.experimental import pallas as pl
from jax.experimental.pallas import tpu as pltpu

F32 = jnp.float32
BF16 = jnp.bfloat16
I32 = jnp.int32

D_MODEL = 2048
HEAD_DIM = 128
H_NSA = 8
H_FOX = 8
G_NSA = 2
HPG_NSA = 4
KV_FOX = 4
HPG_FOX = 2
W_NSA = H_NSA * HEAD_DIM
W_FOX = H_FOX * HEAD_DIM
CMP_LEN = 32
CMP_STRIDE = 16
CMP_HID = 256
SEL_BLOCK = 64
N_SEL = 16
N_LOCAL_SEL = 2
WINDOW = 512
N_EXPERTS = 32
TOP_K = 4
D_FF = 2048
SWIGLU_ALPHA = 1.702
SWIGLU_LIMIT = 7.0
RMS_EPS = 1e-6
ATTN_SCALE = HEAD_DIM ** -0.5
LOG2E = 1.4426950408889634
LN2 = 0.6931471805599453
Q_PRESCALE = ATTN_SCALE * LOG2E
FORCE_SCORE = 1e9
MASKED = -2e30
M_FLOOR = -1e30

LANES = 128
VMEM_LIMIT = 52 * 1024 * 1024

N_MAIN = 4608
PROJ_TN = 512
COL_KVCS, COL_WIN, COL_KVF, COL_QN, COL_QF = 0, 1024, 1536, 2560, 3584


def _cparams(sem, vmem=VMEM_LIMIT):
    return pltpu.CompilerParams(dimension_semantics=sem, vmem_limit_bytes=vmem)


def _rms(x, g):
    return x * lax.rsqrt(jnp.mean(x * x, axis=-1, keepdims=True) + RMS_EPS) * g


def _nt_dot(a, b):
    return lax.dot_general(a, b, (((1,), (1,)), ((), ())), preferred_element_type=F32)


def _ada_kernel(c_ref, w_ref, b_ref, o_ref):
    c = c_ref[...]
    a = (c * jax.nn.sigmoid(c)).astype(BF16)
    o_ref[...] = jnp.dot(a, w_ref[...].astype(BF16), preferred_element_type=F32) + b_ref[...]


def ada_call(c, w, b):
    bc, d = c.shape
    n = w.shape[1]
    tn = 1024
    return pl.pallas_call(
        _ada_kernel,
        grid=(n // tn,),
        in_specs=[pl.BlockSpec((bc, d), lambda j: (0, 0)),
                  pl.BlockSpec((d, tn), lambda j: (0, j)),
                  pl.BlockSpec((1, tn), lambda j: (0, j))],
        out_specs=pl.BlockSpec((bc, tn), lambda j: (0, j)),
        out_shape=jax.ShapeDtypeStruct((bc, n), F32),
        compiler_params=_cparams(("arbitrary",)),
        name="ada",
    )(c, w, b)


def _inproj_kernel(x_ref, g_ref, sc_ref, sh_ref, wm_ref, ws_ref, bs_ref,
                   pb_ref, kvcs_ref, win_ref, kvf_ref, small_ref, h_scr):
    j = pl.program_id(2)

    @pl.when(j == 0)
    def _():
        h = _rms(x_ref[0], g_ref[...]) * (1.0 + sc_ref[0]) + sh_ref[0]
        hb = h.astype(BF16)
        h_scr[...] = hb
        z = jnp.dot(hb, ws_ref[...], preferred_element_type=F32) + bs_ref[...]
        lane = lax.broadcasted_iota(I32, z.shape, 1)
        small_ref[0] = jnp.where(lane < 3 * H_NSA, jax.nn.sigmoid(z), jax.nn.log_sigmoid(z))

    r = jnp.dot(h_scr[...], wm_ref[...], preferred_element_type=F32)
    pb_ref[0] = (r * jnp.where(j >= COL_QN // PROJ_TN, Q_PRESCALE, 1.0)).astype(BF16)
    tm = r.shape[0]

    def scatter_rows(ref, base, per_tok):
        for c in range(PROJ_TN // HEAD_DIM):
            ref[0, pl.ds(base + c, tm, stride=per_tok), :] = r[:, c * HEAD_DIM:(c + 1) * HEAD_DIM]

    for step, (ref, base, per_tok) in enumerate(((kvcs_ref, 0, 8), (kvcs_ref, 4, 8), (win_ref, 0, 4),
                                                 (kvf_ref, 0, 8), (kvf_ref, 4, 8))):
        pl.when(j == step)(functools.partial(scatter_rows, ref, base, per_tok))


def inproj_call(x, g, sc, sh, wm, ws, bs, tm):
    b, t, d = x.shape
    per_row = sc.shape[1] != 1
    nj = N_MAIN // PROJ_TN
    mod_spec = (pl.BlockSpec((1, tm, d), lambda bb, i, j: (bb, i, 0)) if per_row
                else pl.BlockSpec((1, 1, d), lambda bb, i, j: (bb, 0, 0)))
    return pl.pallas_call(
        _inproj_kernel,
        grid=(b, t // tm, nj),
        in_specs=[pl.BlockSpec((1, tm, d), lambda bb, i, j: (bb, i, 0)),
                  pl.BlockSpec((1, d), lambda bb, i, j: (0, 0)),
                  mod_spec, mod_spec,
                  pl.BlockSpec((d, PROJ_TN), lambda bb, i, j: (0, j)),
                  pl.BlockSpec((d, LANES), lambda bb, i, j: (0, 0)),
                  pl.BlockSpec((1, LANES), lambda bb, i, j: (0, 0))],
        out_specs=[pl.BlockSpec((1, tm, PROJ_TN), lambda bb, i, j: (bb, i, j)),
                   pl.BlockSpec((1, tm * 8, HEAD_DIM), lambda bb, i, j: (bb, i, 0)),
                   pl.BlockSpec((1, tm * 4, HEAD_DIM), lambda bb, i, j: (bb, i, 0)),
                   pl.BlockSpec((1, tm * 8, HEAD_DIM), lambda bb, i, j: (bb, i, 0)),
                   pl.BlockSpec((1, tm, LANES), lambda bb, i, j: (bb, i, 0))],
        out_shape=[jax.ShapeDtypeStruct((b, t, N_MAIN), BF16),
                   jax.ShapeDtypeStruct((b, t * 8, HEAD_DIM), F32),
                   jax.ShapeDtypeStruct((b, t * 4, HEAD_DIM), F32),
                   jax.ShapeDtypeStruct((b, t * 8, HEAD_DIM), F32),
                   jax.ShapeDtypeStruct((b, t, LANES), F32)],
        scratch_shapes=[pltpu.VMEM((tm, d), BF16)],
        compiler_params=_cparams(("arbitrary", "arbitrary", "arbitrary")),
        name="inproj",
    )(x, g, sc, sh, wm, ws, bs)


def _lane_cumsum(x):
    lane = lax.broadcasted_iota(I32, x.shape, 1)
    d = 1
    while d < LANES:
        x = x + jnp.where(lane >= d, pltpu.roll(x, d, axis=1), 0.0)
        d *= 2
    return x


def _cumsum_kernel(x_ref, o_ref):
    t = x_ref.shape[2]
    carry = jnp.zeros((x_ref.shape[1], 1), F32)
    for c in range(t // LANES):
        sl = slice(c * LANES, (c + 1) * LANES)
        y = _lane_cumsum(x_ref[0, :, sl]) + carry
        o_ref[0, :, sl] = y
        carry = y[:, LANES - 1:LANES]


def cumsum_call(x):
    b, h, t = x.shape
    return pl.pallas_call(
        _cumsum_kernel,
        grid=(b,),
        in_specs=[pl.BlockSpec((1, h, t), lambda bb: (bb, 0, 0))],
        out_specs=pl.BlockSpec((1, h, t), lambda bb: (bb, 0, 0)),
        out_shape=jax.ShapeDtypeStruct((b, h, t), F32),
        compiler_params=_cparams(("arbitrary",)),
        name="logf_cumsum",
    )(x)


def _nsa_slope(g, h):
    return jnp.where(g == 0, 2.0 ** -(h + 1), 2.0 ** -(HPG_NSA + h + 1)).astype(F32)


def _flash_schedule(mode, t, tq, tk):
    qi, ki, fl = [], [], []
    for i in range(t // tq):
        lo = max(0, (i * tq - WINDOW + 1) // tk) if mode == "win" else 0
        hi = (i * tq + tq - 1) // tk
        for kb in range(lo, hi + 1):
            below_diag = kb * tk + tk - 1 <= i * tq
            in_window = mode != "win" or (i * tq + tq - 1) - kb * tk < WINDOW
            qi.append(i)
            ki.append(kb)
            fl.append((1 if kb == lo else 0) | (2 if kb == hi else 0) | (0 if below_diag and in_window else 4))
    return tuple(jnp.asarray(a, I32) for a in (qi, ki, fl))


def _flash_kernel(qi_ref, ki_ref, fl_ref, *refs, mode, hpg, tq, tk):
    if mode == "fox":
        q_ref, k_ref, v_ref, cq_ref, ck_ref, o_ref, m_scr, acc_scr = refs
    elif mode == "sel":
        q_ref, k_ref, v_ref, sm_ref, o_ref, m_scr, acc_scr = refs
    else:
        q_ref, k_ref, v_ref, o_ref, m_scr, acc_scr = refs
    g = pl.program_id(1)
    p_idx = pl.program_id(2)
    q0 = qi_ref[p_idx] * tq
    k0 = ki_ref[p_idx] * tk
    flags = fl_ref[p_idx]

    @pl.when((flags & 1) != 0)
    def _():
        m_scr[...] = jnp.full(m_scr.shape, M_FLOOR, F32)
        acc_scr[...] = jnp.zeros(acc_scr.shape, F32)

    def step(positional):
        k = k_ref[0]
        v = v_ref[0]
        mask = None
        if positional:
            dist = (q0 + lax.broadcasted_iota(I32, (tq, tk), 0)) - (k0 + lax.broadcasted_iota(I32, (tq, tk), 1))
        if mode == "win":
            mask = (lax.bitcast_convert_type(dist, jnp.uint32) < WINDOW) if positional else None
        elif mode == "fox":
            mask = (dist >= 0) if positional else None
        else:
            blk = (k0 + lax.broadcasted_iota(I32, (sm_ref.shape[3], tk), 1)) // SEL_BLOCK
            expand = jnp.where(blk == lax.broadcasted_iota(I32, blk.shape, 0), 1.0, 0.0).astype(BF16)
            chosen = jnp.dot(sm_ref[0, 0], expand, preferred_element_type=F32)
            mask = (jnp.where(dist >= 0, chosen, 0.0) if positional else chosen) > 0.5
        krel = (k0 - q0 + lax.broadcasted_iota(I32, (1, tk), 1)).astype(F32)
        v_ones = jnp.concatenate([v, jnp.ones((tk, HEAD_DIM), BF16)], axis=1)
        for h in range(hpg):
            q = q_ref[0, :, h * HEAD_DIM:(h + 1) * HEAD_DIM]
            if mode == "fox":
                bias = (cq_ref[0, 0, h:h + 1, 0:1] - ck_ref[0, 0, h:h + 1, :]) * LOG2E
            else:
                bias = (_nsa_slope(g, h) * LOG2E) * krel
            s = _nt_dot(q, k) + bias
            if mask is not None:
                s = jnp.where(mask, s, MASKED)
            m_prev = m_scr[h]
            m_new = jnp.maximum(m_prev, jnp.max(s, axis=1, keepdims=True))
            alpha = jnp.exp2(m_prev - m_new)
            p = jnp.exp2(s - jnp.concatenate([m_new] * (tk // LANES), axis=1))
            pv = jnp.dot(p.astype(BF16), v_ones, preferred_element_type=F32)
            acc_scr[h] = jnp.concatenate([alpha, alpha], axis=1) * acc_scr[h] + pv
            m_scr[h] = m_new

    pl.when((flags & 4) != 0)(functools.partial(step, True))
    pl.when((flags & 4) == 0)(functools.partial(step, False))

    @pl.when((flags & 2) != 0)
    def _():
        for h in range(hpg):
            o = acc_scr[h, :, :HEAD_DIM] / jnp.maximum(acc_scr[h, :, HEAD_DIM:], 1e-30)
            o_ref[0, :, h * HEAD_DIM:(h + 1) * HEAD_DIM] = o.astype(o_ref.dtype)


def flash_call(mode, pb, *, tq, tk, extra=()):
    b, t, _ = pb.shape
    if mode == "fox":
        hpg, ngrp = HPG_FOX, KV_FOX
        qcol, kcol, vcol = COL_QF // (hpg * HEAD_DIM), COL_KVF // HEAD_DIM, COL_KVF // HEAD_DIM + KV_FOX
    else:
        hpg, ngrp = HPG_NSA, G_NSA
        base = (COL_KVCS + 512) if mode == "sel" else COL_WIN
        qcol, kcol, vcol = COL_QN // (hpg * HEAD_DIM), base // HEAD_DIM, base // HEAD_DIM + G_NSA
    qi, ki, fl = _flash_schedule(mode, t, tq, tk)
    in_specs = [pl.BlockSpec((1, tq, hpg * HEAD_DIM), lambda bb, g, p, qi, ki, fl: (bb, qi[p], qcol + g)),
                pl.BlockSpec((1, tk, HEAD_DIM), lambda bb, g, p, qi, ki, fl: (bb, ki[p], kcol + g)),
                pl.BlockSpec((1, tk, HEAD_DIM), lambda bb, g, p, qi, ki, fl: (bb, ki[p], vcol + g))]
    args = [pb, pb, pb]
    if mode == "fox":
        c8 = extra[0]
        in_specs += [pl.BlockSpec((1, 1, 8, tq), lambda bb, g, p, qi, ki, fl: (bb, g, 0, qi[p])),
                     pl.BlockSpec((1, 1, 8, tk), lambda bb, g, p, qi, ki, fl: (bb, g, 0, ki[p]))]
        args += [c8, c8]
    elif mode == "sel":
        sm = extra[0]
        in_specs += [pl.BlockSpec((1, 1, tq, sm.shape[3]), lambda bb, g, p, qi, ki, fl: (bb, g, qi[p], 0))]
        args += [sm]
    grid_spec = pltpu.PrefetchScalarGridSpec(
        num_scalar_prefetch=3,
        grid=(b, ngrp, int(qi.shape[0])),
        in_specs=in_specs,
        out_specs=pl.BlockSpec((1, tq, hpg * HEAD_DIM), lambda bb, g, p, qi, ki, fl: (bb, qi[p], g)),
        scratch_shapes=[pltpu.VMEM((hpg, tq, LANES), F32), pltpu.VMEM((hpg, tq, 2 * HEAD_DIM), F32)])
    return pl.pallas_call(
        functools.partial(_flash_kernel, mode=mode, hpg=hpg, tq=tq, tk=tk),
        grid_spec=grid_spec,
        out_shape=jax.ShapeDtypeStruct((b, t, ngrp * hpg * HEAD_DIM), BF16),
        compiler_params=_cparams(("arbitrary",) * 3),
        name="flash_" + mode,
    )(qi, ki, fl, *args)


def _cmp_bias_kernel(pe_ref, w1_ref, b1_ref, o_ref):
    for j in range(2):
        acc = jnp.zeros((8, CMP_HID), F32)
        for p in range(CMP_LEN // 2):
            acc = acc + _split_dot(jnp.broadcast_to(pe_ref[j, p:p + 1, :], (8, 2 * HEAD_DIM)), w1_ref[j, p])
        o_ref[j] = acc + b1_ref[j:j + 1, :]


def cmp_bias_call(pe2, w1p, b1):
    return pl.pallas_call(_cmp_bias_kernel, out_shape=jax.ShapeDtypeStruct((2, 8, CMP_HID), F32),
                          compiler_params=_cparams(None), name="cmp_bias")(pe2, w1p, b1)


def _compress_rows(load, n, w1_ref, hb_ref, w2_ref, b2_ref):
    outs = []
    for j in range(2):
        both = jnp.zeros((2 * n, 2 * CMP_HID), F32)
        for sp in range(CMP_STRIDE // 2):
            parts = [jnp.concatenate([load(j * G_NSA + g, 2 * sp + u) for u in range(2)], axis=1)
                     for g in range(G_NSA)]
            both = both + jnp.dot(jnp.concatenate(parts, axis=0).astype(BF16), w1_ref[j, sp],
                                  preferred_element_type=F32)
        for g in range(G_NSA):
            f = both[g * n:(g + 1) * n, :CMP_HID]
            s = pltpu.roll(both[g * n:(g + 1) * n, CMP_HID:], n - 1, axis=0)
            hid = jax.nn.gelu(f + s + hb_ref[j, 0:1, :])
            outs.append(jnp.dot(hid.astype(BF16), w2_ref[j], preferred_element_type=F32) + b2_ref[j:j + 1, :])
    return jnp.concatenate(outs, axis=1)


def _compress_prompt_kernel(x_ref, w1_ref, hb_ref, w2_ref, b2_ref, o_ref):
    n = o_ref.shape[1]
    load = lambda cb, s: x_ref[0, pl.ds(s * 8 + cb, n, stride=CMP_STRIDE * 8), :]
    o_ref[0] = _compress_rows(load, n, w1_ref, hb_ref, w2_ref, b2_ref).astype(o_ref.dtype)


def _cmp_weight_specs():
    def const(shape):
        return pl.BlockSpec(shape, lambda *a: (0,) * len(shape))
    return [const((2, CMP_STRIDE // 2, 2 * HEAD_DIM, 2 * CMP_HID)), const((2, 8, CMP_HID)),
            const((2, CMP_HID, HEAD_DIM)), const((2, HEAD_DIM))]


def compress_prompt_call(kvcs, w1f, hb, w2, b2):
    b, t8, _ = kvcs.shape
    n = t8 // 8 // CMP_STRIDE
    return pl.pallas_call(
        _compress_prompt_kernel,
        grid=(b,),
        in_specs=[pl.BlockSpec((1, t8, HEAD_DIM), lambda bb: (bb, 0, 0))] + _cmp_weight_specs(),
        out_specs=pl.BlockSpec((1, n, 512), lambda bb: (bb, 0, 0)),
        out_shape=jax.ShapeDtypeStruct((b, n, 512), BF16),
        compiler_params=_cparams(("arbitrary",)),
        name="compress_prompt",
    )(kvcs, w1f, hb, w2, b2)


def _rank_select_cols(score_t, n_keep):
    ns = score_t.shape[0]
    row_id = lax.broadcasted_iota(I32, score_t.shape, 0)
    rank = jnp.zeros(score_t.shape, F32)
    for c in range(ns):
        row = score_t[c:c + 1, :]
        rank = rank + jnp.where(row_id > c, jnp.where(row >= score_t, 1.0, 0.0), jnp.where(row > score_t, 1.0, 0.0))
    return jnp.where(rank < n_keep, 1.0, 0.0)


def _split_dot(a, b_bf16):
    hi = a.astype(BF16)
    r1 = a - hi.astype(F32)
    mid = r1.astype(BF16)
    lo = (r1 - mid.astype(F32)).astype(BF16)
    return (jnp.dot(hi, b_bf16, preferred_element_type=F32) + jnp.dot(mid, b_bf16, preferred_element_type=F32)
            + jnp.dot(lo, b_bf16, preferred_element_type=F32))


def _overlap_matrix(nc, ns):
    ci = lax.broadcasted_iota(I32, (nc, ns), 0) * CMP_STRIDE
    sj = lax.broadcasted_iota(I32, (nc, ns), 1)
    return jnp.where(ci < (sj + 1) * SEL_BLOCK, jnp.where(ci + CMP_LEN > sj * SEL_BLOCK, 1.0, 0.0), 0.0).astype(BF16)


def _force_scores(score, blk, lag):
    recent = lax.bitcast_convert_type(lag, jnp.uint32) < N_LOCAL_SEL
    score = jnp.where(recent, FORCE_SCORE, jnp.where(lag >= 0, score, -FORCE_SCORE))
    return jnp.where(blk == 0, FORCE_SCORE, score)


def _cmpsel_prompt_kernel(q_ref, k_ref, v_ref, o_ref, sm_ref, *, tq, n_cmp, n_sel, n_keep):
    g = pl.program_id(1)
    i = pl.program_id(2)
    nc = k_ref.shape[1]
    k = k_ref[0]
    v = v_ref[0]
    qpos = i * tq + lax.broadcasted_iota(I32, (tq, nc), 0)
    cidx = lax.broadcasted_iota(I32, (tq, nc), 1)
    mask = jnp.where(cidx < n_cmp, cidx * CMP_STRIDE + CMP_LEN - 1, 2 ** 30) <= qpos
    center = (lax.broadcasted_iota(I32, (1, nc), 1) * CMP_STRIDE - i * tq).astype(F32) + 0.5 * (CMP_LEN - 1)
    imp = jnp.zeros((tq, nc), F32)
    for h in range(HPG_NSA):
        q = q_ref[0, :, h * HEAD_DIM:(h + 1) * HEAD_DIM]
        s = _nt_dot(q, k) * LN2 + _nsa_slope(g, h) * center
        s = jnp.where(mask, s, -1e30)
        e = jnp.where(mask, jnp.exp(s - jnp.max(s, axis=1, keepdims=True)), 0.0)
        p = e / jnp.maximum(jnp.sum(e, axis=1, keepdims=True), 1e-30)
        o_ref[0, :, h * HEAD_DIM:(h + 1) * HEAD_DIM] = jnp.dot(
            p.astype(BF16), v, preferred_element_type=F32).astype(o_ref.dtype)
        imp = imp + p
    ns_pad = -(-n_sel // LANES) * LANES
    ns8 = -(-n_sel // 8) * 8
    score_t = _split_dot(imp, _overlap_matrix(nc, ns_pad)).T[:ns8]
    blk = lax.broadcasted_iota(I32, (ns8, tq), 0)
    lag = (i * tq + lax.broadcasted_iota(I32, (ns8, tq), 1)) // SEL_BLOCK - blk
    score_t = jnp.where(blk < n_sel, _force_scores(score_t, blk, lag), -2.0 * FORCE_SCORE)
    chosen_t = jnp.concatenate([_rank_select_cols(score_t, n_keep), jnp.zeros((ns_pad - ns8, tq), F32)], axis=0)
    sm_ref[0, 0] = chosen_t.T[:, :n_sel].astype(sm_ref.dtype)


def cmpsel_prompt_call(pb, kvc, *, tq):
    b, t, _ = pb.shape
    nc = kvc.shape[1]
    n_cmp = nc - 1
    n_sel = -(-t // SEL_BLOCK)
    n_keep = min(N_SEL, n_sel)
    qcol = COL_QN // (HPG_NSA * HEAD_DIM)
    return pl.pallas_call(
        functools.partial(_cmpsel_prompt_kernel, tq=tq, n_cmp=n_cmp, n_sel=n_sel, n_keep=n_keep),
        grid=(b, G_NSA, t // tq),
        in_specs=[pl.BlockSpec((1, tq, HPG_NSA * HEAD_DIM), lambda bb, g, i: (bb, i, qcol + g)),
                  pl.BlockSpec((1, nc, HEAD_DIM), lambda bb, g, i: (bb, 0, g)),
                  pl.BlockSpec((1, nc, HEAD_DIM), lambda bb, g, i: (bb, 0, G_NSA + g))],
        out_specs=[pl.BlockSpec((1, tq, HPG_NSA * HEAD_DIM), lambda bb, g, i: (bb, i, g)),
                   pl.BlockSpec((1, 1, tq, n_sel), lambda bb, g, i: (bb, g, i, 0))],
        out_shape=[jax.ShapeDtypeStruct((b, t, W_NSA), BF16),
                   jax.ShapeDtypeStruct((b, G_NSA, t, n_sel), BF16)],
        compiler_params=_cparams(("arbitrary",) * 3),
        name="cmpsel_prompt",
    )(pb, kvc, kvc)


def _merge_kernel(ocmp_ref, osel_ref, owin_ref, ofox_ref, small_ref, x_ref, ggrp_ref, wout_ref, gpost_ref,
                  gt_ref, gpre_ref, sc_ref, sh_ref, wrh_ref, wrl_ref, br_ref, x1_ref, h2_ref, route_ref):
    gs = small_ref[0]
    parts = []
    for h in range(H_NSA):
        sl = slice(h * HEAD_DIM, (h + 1) * HEAD_DIM)
        parts.append(gs[:, h:h + 1] * ocmp_ref[0, :, sl].astype(F32)
                     + gs[:, H_NSA + h:H_NSA + h + 1] * osel_ref[0, :, sl].astype(F32)
                     + gs[:, 2 * H_NSA + h:2 * H_NSA + h + 1] * owin_ref[0, :, sl].astype(F32))
    o_nsa = jnp.concatenate(parts, axis=1)
    y = jnp.concatenate([_rms(o_nsa, ggrp_ref[:, :W_NSA]),
                         _rms(ofox_ref[0].astype(F32), ggrp_ref[:, W_NSA:])], axis=1).astype(BF16)
    m = jnp.dot(y, wout_ref[...], preferred_element_type=F32)
    x1 = x_ref[0] + gt_ref[0] * _rms(m, gpost_ref[...])
    x1_ref[0] = x1
    h2 = _rms(x1, gpre_ref[...]) * (1.0 + sc_ref[0]) + sh_ref[0]
    hi = h2.astype(BF16)
    h2_ref[0] = hi
    lo = (h2 - hi.astype(F32)).astype(BF16)
    logits = (jnp.dot(hi, wrh_ref[...], preferred_element_type=F32) + jnp.dot(hi, wrl_ref[...], preferred_element_type=F32)
              + jnp.dot(lo, wrh_ref[...], preferred_element_type=F32) + br_ref[...])
    lane = lax.broadcasted_iota(I32, logits.shape, 1)
    vals = jnp.where(lane < N_EXPERTS, logits, -jnp.inf)
    top_v, top_e = [], []
    for _ in range(TOP_K):
        mx = jnp.max(vals, axis=1, keepdims=True)
        idx = jnp.min(jnp.where(vals == mx, lane, LANES), axis=1, keepdims=True)
        top_v.append(mx)
        top_e.append(idx)
        vals = jnp.where(lane == idx, -jnp.inf, vals)
    ex = [jnp.exp(v - top_v[0]) for v in top_v]
    den = ex[0] + ex[1] + ex[2] + ex[3]
    route = jnp.zeros(logits.shape, F32)
    for kx in range(TOP_K):
        route = jnp.where(lane == kx, top_e[kx].astype(F32), route)
        route = jnp.where(lane == TOP_K + kx, ex[kx] / den, route)
    route_ref[0] = route


def merge_call(ocmp, osel, owin, ofox, small, x, ggrp, wout, gpost, gt, gpre, sc, sh, wrh, wrl, br, tm):
    b, t, d = x.shape
    per_row = sc.shape[1] != 1
    row = lambda w: pl.BlockSpec((1, tm, w), lambda bb, i: (bb, i, 0))
    const = lambda shape: pl.BlockSpec(shape, lambda bb, i: (0,) * len(shape))
    mod = row(d) if per_row else pl.BlockSpec((1, 1, d), lambda bb, i: (bb, 0, 0))
    return pl.pallas_call(
        _merge_kernel,
        grid=(b, t // tm),
        in_specs=[row(W_NSA), row(W_NSA), row(W_NSA), row(W_FOX), row(LANES), row(d),
                  const((1, d)), const((d, d)), const((1, d)), mod, const((1, d)), mod, mod,
                  const((d, LANES)), const((d, LANES)), const((1, LANES))],
        out_specs=[row(d), row(d), row(LANES)],
        out_shape=[jax.ShapeDtypeStruct((b, t, d), F32), jax.ShapeDtypeStruct((b, t, d), BF16),
                   jax.ShapeDtypeStruct((b, t, LANES), F32)],
        compiler_params=_cparams(("arbitrary", "arbitrary")),
        name="merge",
    )(ocmp, osel, owin, ofox, small, x, ggrp, wout, gpost, gt, gpre, sc, sh, wrh, wrl, br)


MOE_TM = 512
MOE_TF = 1024
MOE_TN = 1024
MOE_CHUNKS = 4


def _expert_changed(blk_e_ref, i):
    return (i == 0) | (blk_e_ref[i] != blk_e_ref[jnp.maximum(i - 1, 0)])


def _moe_up_kernel(blk_e_ref, nused_ref, x_ref, wg_ref, wl_ref, bg_ref, bl_ref, *rest, blk0, nb):
    a_ref, wg_bf, wl_bf = rest[-3:]
    step = pl.program_id(1)
    i = blk0 + step
    live = (step < nb) & (i < nused_ref[0])

    @pl.when(live & ((step == 0) | (blk_e_ref[i] != blk_e_ref[jnp.maximum(i - 1, 0)])))
    def _():
        wg_bf[...] = wg_ref[0].astype(BF16)
        wl_bf[...] = wl_ref[0].astype(BF16)

    @pl.when(live)
    def _():
        x = x_ref[...]
        ug = jnp.dot(x, wg_bf[...], preferred_element_type=F32) + bg_ref[0]
        ul = jnp.dot(x, wl_bf[...], preferred_element_type=F32) + bl_ref[0]
        glu = jnp.minimum(ug, SWIGLU_LIMIT)
        lin = jnp.clip(ul, -SWIGLU_LIMIT, SWIGLU_LIMIT)
        a_ref[...] = (glu * jax.nn.sigmoid(SWIGLU_ALPHA * glu) * (lin + 1.0)).astype(a_ref.dtype)

    @pl.when(jnp.logical_not(live))
    def _():
        a_ref[...] = jnp.zeros(a_ref.shape, a_ref.dtype)


def moe_up_call(blk_e, n_used, xs, w_up, b_up, a_prev, blk0):
    rows, d = xs.shape
    nb = rows // MOE_TM
    n_blk = blk_e.shape[0]
    n_slots = n_blk * MOE_TM
    nf = D_FF // MOE_TF
    assert a_prev is not None or blk0 == 0
    steps = n_blk if a_prev is None else nb
    xrow = lambda i: jnp.minimum(i, nb - 1)
    expert = lambda be, i: be[blk0 + jnp.minimum(i, nb - 1)]
    in_specs = [pl.BlockSpec((MOE_TM, d), lambda f, i, be, nu: (xrow(i), 0)),
                pl.BlockSpec((1, d, MOE_TF), lambda f, i, be, nu: (expert(be, i), 0, f)),
                pl.BlockSpec((1, d, MOE_TF), lambda f, i, be, nu: (expert(be, i), 0, nf + f)),
                pl.BlockSpec((1, 1, MOE_TF), lambda f, i, be, nu: (expert(be, i), 0, f)),
                pl.BlockSpec((1, 1, MOE_TF), lambda f, i, be, nu: (expert(be, i), 0, nf + f))]
    args = [blk_e, n_used, xs, w_up, w_up, b_up, b_up]
    aliases = {}
    if a_prev is not None:
        in_specs.append(pl.BlockSpec(memory_space=pl.ANY))
        aliases = {len(args): 0}
        args.append(a_prev)
    grid_spec = pltpu.PrefetchScalarGridSpec(
        num_scalar_prefetch=2,
        grid=(nf, steps),
        in_specs=in_specs,
        out_specs=pl.BlockSpec((MOE_TM, MOE_TF), lambda f, i, be, nu: (blk0 + i, f)),
        scratch_shapes=[pltpu.VMEM((d, MOE_TF), BF16), pltpu.VMEM((d, MOE_TF), BF16)])
    return pl.pallas_call(
        functools.partial(_moe_up_kernel, blk0=blk0, nb=nb),
        grid_spec=grid_spec,
        out_shape=jax.ShapeDtypeStruct((n_slots, D_FF), BF16),
        input_output_aliases=aliases,
        compiler_params=_cparams(("arbitrary", "arbitrary")),
        name="moe_up",
    )(*args)


def _moe_down_kernel(blk_e_ref, nused_ref, a_ref, wd_ref, bd_ref, sw_ref, y_ref, wd_bf):
    i = pl.program_id(1)

    @pl.when(_expert_changed(blk_e_ref, i))
    def _():
        wd_bf[...] = wd_ref[0].astype(BF16)

    @pl.when(i < nused_ref[0])
    def _():
        y = jnp.dot(a_ref[...], wd_bf[...], preferred_element_type=F32) + bd_ref[0]
        y_ref[...] = (y * sw_ref[...]).astype(y_ref.dtype)

    @pl.when(i >= nused_ref[0])
    def _():
        y_ref[...] = jnp.zeros(y_ref.shape, y_ref.dtype)


def moe_down_call(blk_e, n_used, a, w_down, b_down, slot_w, col0, ncol):
    n_slots, dff = a.shape
    n_blk = n_slots // MOE_TM
    grid_spec = pltpu.PrefetchScalarGridSpec(
        num_scalar_prefetch=2,
        grid=(ncol, n_blk),
        in_specs=[pl.BlockSpec((MOE_TM, dff), lambda c, i, be, nu: (i, 0)),
                  pl.BlockSpec((1, dff, MOE_TN), lambda c, i, be, nu: (be[i], 0, col0 + c)),
                  pl.BlockSpec((1, 1, MOE_TN), lambda c, i, be, nu: (be[i], 0, col0 + c)),
                  pl.BlockSpec((MOE_TM, 1), lambda c, i, be, nu: (i, 0))],
        out_specs=pl.BlockSpec((MOE_TM, MOE_TN), lambda c, i, be, nu: (i, c)),
        scratch_shapes=[pltpu.VMEM((dff, MOE_TN), BF16)])
    return pl.pallas_call(
        _moe_down_kernel,
        grid_spec=grid_spec,
        out_shape=jax.ShapeDtypeStruct((n_slots, ncol * MOE_TN), BF16),
        compiler_params=_cparams(("arbitrary", "arbitrary")),
        name="moe_down",
    )(blk_e, n_used, a, w_down, b_down, slot_w)


def moe_dispatch(route):
    n = route.shape[0]
    n_asg = n * TOP_K
    n_blk = -(-(n_asg + N_EXPERTS * (MOE_TM - 1)) // MOE_TM)
    e_flat = route[:, :TOP_K].astype(I32).reshape(-1)
    w_flat = route[:, TOP_K:2 * TOP_K].reshape(-1)
    order = jnp.argsort(e_flat).astype(I32)
    rank = jnp.argsort(order).astype(I32)
    grp_end = jnp.searchsorted(e_flat[order], jnp.arange(N_EXPERTS, dtype=I32), side="right").astype(I32)
    grp_start = jnp.concatenate([jnp.zeros((1,), I32), grp_end[:-1]])
    counts = grp_end - grp_start
    padded = (counts + MOE_TM - 1) // MOE_TM * MOE_TM
    pad_end = jnp.cumsum(padded)
    pad_start = pad_end - padded
    blk_e = jnp.minimum(jnp.searchsorted(pad_end, jnp.arange(n_blk, dtype=I32) * MOE_TM, side="right"),
                        N_EXPERTS - 1).astype(I32)
    n_used = (pad_end[-1:] // MOE_TM).astype(I32)
    off = (jnp.arange(n_blk * MOE_TM, dtype=I32).reshape(n_blk, MOE_TM) - pad_start[blk_e][:, None])
    live = (off < counts[blk_e][:, None]).reshape(-1)
    src = order[jnp.clip(grp_start[blk_e][:, None] + off, 0, n_asg - 1).reshape(-1)]
    slot_tok = jnp.where(live, src // TOP_K, 0)
    slot_w = jnp.where(live, w_flat[src], 0.0)
    dest = (rank + (pad_start - grp_start)[e_flat]).reshape(n, TOP_K).T
    return slot_tok, slot_w[:, None], dest, blk_e, n_used


def _final_kernel(*refs):
    *yg_refs, x1_ref, gt_ref, gpost_ref, o_ref = refs
    parts = []
    for yg_ref in yg_refs:
        s = yg_ref[0, 0].astype(F32)
        for kx in range(1, TOP_K):
            s = s + yg_ref[kx, 0].astype(F32)
        parts.append(s)
    o_ref[0] = x1_ref[0] + gt_ref[0] * _rms(jnp.concatenate(parts, axis=1), gpost_ref[...])


def final_call(ygs, x1, gt, gpost, tm):
    b, t, d = x1.shape
    per_row = gt.shape[1] != 1
    mod = (pl.BlockSpec((1, tm, d), lambda bb, i: (bb, i, 0)) if per_row
           else pl.BlockSpec((1, 1, d), lambda bb, i: (bb, 0, 0)))
    return pl.pallas_call(
        _final_kernel,
        grid=(b, t // tm),
        in_specs=[pl.BlockSpec((TOP_K, 1, tm, yg.shape[-1]), lambda bb, i: (0, bb, i, 0)) for yg in ygs]
                 + [pl.BlockSpec((1, tm, d), lambda bb, i: (bb, i, 0)),
                    mod, pl.BlockSpec((1, d), lambda bb, i: (0, 0))],
        out_specs=pl.BlockSpec((1, tm, d), lambda bb, i: (bb, i, 0)),
        out_shape=jax.ShapeDtypeStruct((b, t, d), F32),
        compiler_params=_cparams(("arbitrary", "arbitrary")),
        name="final",
    )(*ygs, x1, gt, gpost)


def prep_weights(w_in, b_nsa_gate, b_fox_forget, w_cmp1, w_out, w_router, b_router):
    d = w_in.shape[0]
    o_qn, o_kvn, o_gn, o_qf, o_kvf, o_fl = 0, 1024, 2560, 2584, 3608, 4632
    wm = jnp.concatenate([w_in[:, o_kvn:o_kvn + 1536], w_in[:, o_kvf:o_kvf + 1024],
                          w_in[:, o_qn:o_qn + 1024], w_in[:, o_qf:o_qf + 1024]], axis=1).astype(BF16)
    pad = LANES - 3 * H_NSA - H_FOX
    ws = jnp.concatenate([w_in[:, o_gn:o_gn + 3 * H_NSA], w_in[:, o_fl:o_fl + H_FOX],
                          jnp.zeros((d, pad), F32)], axis=1).astype(BF16)
    bs = jnp.concatenate([b_nsa_gate, b_fox_forget, jnp.zeros((pad,), F32)])[None, :]
    w1p = w_cmp1.reshape(2, CMP_STRIDE, 2 * HEAD_DIM, CMP_HID).astype(BF16)
    wr = jnp.concatenate([w_router, jnp.zeros((d, LANES - N_EXPERTS), F32)], axis=1)
    wrh = wr.astype(BF16)
    wrl = (wr - wrh.astype(F32)).astype(BF16)
    br = jnp.concatenate([b_router, jnp.zeros((LANES - N_EXPERTS,), F32)])[None, :]
    half = CMP_STRIDE // 2
    w1f = jnp.concatenate([w1p[:, :half], w1p[:, half:]], axis=-1)
    return dict(wm=wm, ws=ws, bs=bs, w1p=w1p, w1f=w1f, wout=w_out.astype(BF16), wrh=wrh, wrl=wrl, br=br)


def prompt_mixer(x, sc, sh, g_pre, W, cmpw, *, tm, tq, tk):
    b, t, _ = x.shape
    pb, kvcs, win, kvf, small = inproj_call(x, g_pre, sc, sh, W["wm"], W["ws"], W["bs"], tm)
    kvc = compress_prompt_call(kvcs, *cmpw)
    ocmp, selmask = cmpsel_prompt_call(pb, kvc, tq=tq)
    osel = flash_call("sel", pb, tq=tq, tk=tk, extra=(selmask,))
    owin = flash_call("win", pb, tq=tq, tk=tk)
    logf = small[:, :, 3 * H_NSA:3 * H_NSA + H_FOX]
    cum = cumsum_call(jnp.swapaxes(logf, 1, 2))
    c8 = jnp.pad(cum.reshape(b, KV_FOX, HPG_FOX, t), ((0, 0), (0, 0), (0, 8 - HPG_FOX), (0, 0)))
    ofox = flash_call("fox", pb, tq=tq, tk=tk, extra=(c8,))
    return ocmp, osel, owin, ofox, small, kvcs, win, kvf, logf


PAGE = 128
CMP_PAGES = 16
CMP_PITCH = 136
FOX_PAGES = 16
NS_PAD = 384


def _head_rows(qf, g):
    rows = [qf[:, (g * HPG_NSA + h) * HEAD_DIM:(g * HPG_NSA + h + 1) * HEAD_DIM] for h in range(HPG_NSA)]
    return jnp.concatenate(rows + [jnp.zeros((8 - HPG_NSA, HEAD_DIM), F32)], axis=0).astype(BF16)


def _slope_col(g):
    row = lax.broadcasted_iota(I32, (8, 1), 0)
    col = jnp.zeros((8, 1), F32)
    for h in range(HPG_NSA):
        col = jnp.where(row == h, 2.0 ** -(g * HPG_NSA + h + 1), col)
    return col


def _masked_softmax(s, mask):
    s = jnp.where(mask, s, -1e30)
    e = jnp.where(mask, jnp.exp(s - jnp.max(s, axis=1, keepdims=True)), 0.0)
    return e / jnp.maximum(jnp.sum(e, axis=1, keepdims=True), 1e-30)


def _store_heads(o_ref, o, g):
    for h in range(HPG_NSA):
        c = (g * HPG_NSA + h) * HEAD_DIM
        o_ref[0, :, c:c + HEAD_DIM] = o[h:h + 1].astype(o_ref.dtype)


def _compress_sample_kernel(pt_ref, *refs, npg):
    pages = refs[:npg + 1]
    w1_ref, hb_ref, w2_ref, b2_ref, o_ref, x_scr = refs[npg + 1:]
    per_page = PAGE // CMP_STRIDE
    n = (npg + 1) * per_page
    chunk_rows = CMP_STRIDE * 8
    for p in range(npg + 1):
        for c in range(per_page):
            x_scr[pl.ds((p * per_page + c) * CMP_PITCH, chunk_rows), :] = pages[p][0, c * chunk_rows:(c + 1) * chunk_rows, :]
    load = lambda cb, s: x_scr[pl.ds(s * 8 + cb, n, stride=CMP_PITCH), :]
    r = _compress_rows(load, n, w1_ref, hb_ref, w2_ref, b2_ref)
    o_ref[0] = r[:npg * per_page].astype(o_ref.dtype)


def compress_sample_call(page_table, cache3, w1f, hb, w2, b2):
    bsz, n_pages = page_table.shape
    npg = CMP_PAGES
    assert n_pages % npg == 0
    page_specs = [pl.BlockSpec((1, PAGE * 8, HEAD_DIM),
                               lambda bb, gi, pt, p=p: (pt[bb, jnp.minimum(gi * npg + p, n_pages - 1)], 0, 0))
                  for p in range(npg + 1)]
    rows_out = npg * PAGE // CMP_STRIDE
    grid_spec = pltpu.PrefetchScalarGridSpec(
        num_scalar_prefetch=1,
        grid=(bsz, n_pages // npg),
        in_specs=page_specs + _cmp_weight_specs(),
        out_specs=pl.BlockSpec((1, rows_out, 512), lambda bb, gi, pt: (bb, gi, 0)),
        scratch_shapes=[pltpu.VMEM(((npg + 1) * (PAGE // CMP_STRIDE) * CMP_PITCH, HEAD_DIM), F32)])
    return pl.pallas_call(
        functools.partial(_compress_sample_kernel, npg=npg),
        grid_spec=grid_spec,
        out_shape=jax.ShapeDtypeStruct((bsz, n_pages * PAGE // CMP_STRIDE, 512), BF16),
        compiler_params=_cparams(("arbitrary", "arbitrary")),
        name="compress_sample",
    )(page_table, *([cache3] * (npg + 1)), w1f, hb, w2, b2)


def _cmpsel_sample_kernel(q_ref, kv_ref, o_ref, idx_ref, *, past, n_cmp, n_keep):
    nc = kv_ref.shape[1]
    qf = q_ref[0].astype(F32)
    cidx = lax.broadcasted_iota(I32, (1, nc), 1)
    mask = jnp.where(cidx < n_cmp, cidx * CMP_STRIDE + CMP_LEN - 1, 2 ** 30) <= past
    center = (cidx * CMP_STRIDE - past).astype(F32) + 0.5 * (CMP_LEN - 1)
    overlap = _overlap_matrix(nc, NS_PAD)
    blk = lax.broadcasted_iota(I32, (8, NS_PAD), 1)
    lag = past // SEL_BLOCK - blk
    ii =lax.broadcasted_iota(I32, (NS_PAD, NS_PAD), 0)
    jj = lax.broadcasted_iota(I32, (NS_PAD, NS_PAD), 1)
    slot = lax.broadcasted_iota(I32, (NS_PAD, LANES), 1).astype(F32)
    rows = []
    for g in range(G_NSA):
        k = kv_ref[0, :, g * HEAD_DIM:(g + 1) * HEAD_DIM]
        v = kv_ref[0, :, (G_NSA + g) * HEAD_DIM:(G_NSA + g + 1) * HEAD_DIM]
        s = _nt_dot(_head_rows(qf, g), k) * LN2 + _slope_col(g) * center
        p = _masked_softmax(s, jnp.broadcast_to(mask, s.shape))
        _store_heads(o_ref, jnp.dot(p.astype(BF16), v, preferred_element_type=F32), g)
        imp = jnp.sum(p[0:HPG_NSA], axis=0, keepdims=True)
        score = _split_dot(jnp.broadcast_to(imp, (8, nc)), overlap)
        score = _force_scores(score, blk, lag)
        row = score[0:1, :]
        col = score.T[:, 0:1]
        ge = jnp.where(col >= row, 1.0, 0.0)
        gt = jnp.where(col > row, 1.0, 0.0)
        rank_row = jnp.sum(jnp.where(ii < jj, ge, gt), axis=0, keepdims=True)
        rank_col = jnp.sum(jnp.where(jj < ii, 1.0 - gt, 1.0 - ge), axis=1, keepdims=True)
        sel_row = jnp.where(rank_row < n_keep, 1.0, 0.0)
        sel_col = jnp.where(rank_col < n_keep, 1.0, 0.0)
        before = jnp.sum(jnp.where(jj < ii, sel_row, 0.0), axis=1, keepdims=True)
        pick = jnp.where(before == slot, sel_col * ii[:, 0:1].astype(F32), 0.0)
        rows.append(jnp.sum(pick, axis=0, keepdims=True))
    idx_ref[0] = jnp.concatenate(rows + [jnp.zeros((8 - G_NSA, LANES), F32)], axis=0).astype(I32)


def cmpsel_sample_call(qn, kvc, *, past):
    bsz = qn.shape[0]
    nc = kvc.shape[1]
    n_cmp = (past + 1) // CMP_STRIDE - 1
    n_sel = -(-(past + 1) // SEL_BLOCK)
    assert n_sel <= NS_PAD and n_cmp <= nc
    n_keep = min(N_SEL, n_sel)
    return pl.pallas_call(
        functools.partial(_cmpsel_sample_kernel, past=past, n_cmp=n_cmp, n_keep=n_keep),
        grid=(bsz,),
        in_specs=[pl.BlockSpec((1, 1, W_NSA), lambda bb: (bb, 0, 0)),
                  pl.BlockSpec((1, nc, 512), lambda bb: (bb, 0, 0))],
        out_specs=[pl.BlockSpec((1, 1, W_NSA), lambda bb: (bb, 0, 0)),
                   pl.BlockSpec((1, 8, LANES), lambda bb: (bb, 0, 0))],
        out_shape=[jax.ShapeDtypeStruct((bsz, 1, W_NSA), BF16), jax.ShapeDtypeStruct((bsz, 8, LANES), I32)],
        compiler_params=_cparams(("arbitrary",)),
        name="cmpsel_sample",
    )(qn, kvc)


def _sel_sample_kernel(idx_ref, pt_ref, q_ref, new_ref, cache_ref, o_ref, kvbuf, sem, *, past, n_pages, n_keep):
    bb = pl.program_id(0)
    per_page = PAGE // SEL_BLOCK
    blk_rows = SEL_BLOCK * 8
    copies = []
    for g in range(G_NSA):
        for kx in range(n_keep):
            j = idx_ref[(bb * G_NSA + g) * n_keep + kx]
            page = pt_ref[bb, jnp.minimum(j // per_page, n_pages - 1)]
            r0 = pl.multiple_of((j % per_page) * blk_rows, blk_rows)
            cp = pltpu.make_async_copy(cache_ref.at[page, pl.ds(r0, blk_rows), :], kvbuf.at[g, kx], sem.at[0])
            cp.start()
            copies.append(cp)
    for cp in copies:
        cp.wait()
    qf = q_ref[0].astype(F32)
    nkeys = n_keep * SEL_BLOCK
    lane = lax.broadcasted_iota(I32, (1, nkeys), 1)
    row0 = lax.broadcasted_iota(I32, (SEL_BLOCK, HEAD_DIM), 0) == 0
    for g in range(G_NSA):
        kpos = lane % SEL_BLOCK
        ks, vs = [], []
        for kx in range(n_keep):
            j = idx_ref[(bb * G_NSA + g) * n_keep + kx]
            is_new = j * SEL_BLOCK >= past
            fresh = jnp.logical_and(is_new, row0)
            ks.append(jnp.where(fresh, new_ref[0, 4 + g:5 + g, :], kvbuf[g, kx, pl.ds(4 + g, SEL_BLOCK, stride=8), :]))
            vs.append(jnp.where(fresh, new_ref[0, 6 + g:7 + g, :], kvbuf[g, kx, pl.ds(6 + g, SEL_BLOCK, stride=8), :]))
            kpos = kpos + jnp.where(lane // SEL_BLOCK == kx, j * SEL_BLOCK, 0)
        k = jnp.concatenate(ks, axis=0).astype(BF16)
        v = jnp.concatenate(vs, axis=0).astype(BF16)
        s = _nt_dot(_head_rows(qf, g), k) * LN2 + _slope_col(g) * (kpos - past).astype(F32)
        p = _masked_softmax(s, jnp.broadcast_to(kpos <= past, s.shape))
        _store_heads(o_ref, jnp.dot(p.astype(BF16), v, preferred_element_type=F32), g)


def sel_sample_call(idx_flat, page_table, qn, kvcs_new, cache3, *, past, n_keep):
    bsz, n_pages = page_table.shape
    assert past % SEL_BLOCK == 0 and past == n_pages * PAGE
    grid_spec = pltpu.PrefetchScalarGridSpec(
        num_scalar_prefetch=2,
        grid=(bsz,),
        in_specs=[pl.BlockSpec((1, 1, W_NSA), lambda bb, ix, pt: (bb, 0, 0)),
                  pl.BlockSpec((1, 8, HEAD_DIM), lambda bb, ix, pt: (bb, 0, 0)),
                  pl.BlockSpec(memory_space=pl.ANY)],
        out_specs=pl.BlockSpec((1, 1, W_NSA), lambda bb, ix, pt: (bb, 0, 0)),
        scratch_shapes=[pltpu.VMEM((G_NSA, n_keep, SEL_BLOCK * 8, HEAD_DIM), F32),
                        pltpu.SemaphoreType.DMA((1,))])
    return pl.pallas_call(
        functools.partial(_sel_sample_kernel, past=past, n_pages=n_pages, n_keep=n_keep),
        grid_spec=grid_spec,
        out_shape=jax.ShapeDtypeStruct((bsz, 1, W_NSA), BF16),
        compiler_params=_cparams(("arbitrary",)),
        name="sel_sample",
    )(idx_flat, page_table, qn, kvcs_new, cache3)


def _win_sample_kernel(q_ref, buf_ref, new_ref, o_ref, nb_ref):
    rows = buf_ref.shape[1]
    wb = rows // 4
    nb_ref[0] = pltpu.roll(buf_ref[0], rows - 4, axis=0)
    nb_ref[0, rows - 4:rows, :] = new_ref[0]
    qf = q_ref[0].astype(F32)
    krel = (lax.broadcasted_iota(I32, (1, wb), 1) - (wb - 1)).astype(F32)
    for g in range(G_NSA):
        k = nb_ref[0, pl.ds(g, wb, stride=4), :].astype(BF16)
        v = nb_ref[0, pl.ds(G_NSA + g, wb, stride=4), :].astype(BF16)
        s = _nt_dot(_head_rows(qf, g), k) * LN2 + _slope_col(g) * krel
        p = _masked_softmax(s, jnp.full(s.shape, True))
        _store_heads(o_ref, jnp.dot(p.astype(BF16), v, preferred_element_type=F32), g)


def win_sample_call(qn, win_buf, win_new):
    bsz, rows, _ = win_buf.shape
    assert rows == WINDOW * 4
    return pl.pallas_call(
        _win_sample_kernel,
        grid=(bsz,),
        in_specs=[pl.BlockSpec((1, 1, W_NSA), lambda bb: (bb, 0, 0)),
                  pl.BlockSpec((1, rows, HEAD_DIM), lambda bb: (bb, 0, 0)),
                  pl.BlockSpec((1, 4, HEAD_DIM), lambda bb: (bb, 0, 0))],
        out_specs=[pl.BlockSpec((1, 1, W_NSA), lambda bb: (bb, 0, 0)),
                   pl.BlockSpec((1, rows, HEAD_DIM), lambda bb: (bb, 0, 0))],
        out_shape=[jax.ShapeDtypeStruct((bsz, 1, W_NSA), BF16), jax.ShapeDtypeStruct((bsz, rows, HEAD_DIM), F32)],
        compiler_params=_cparams(("arbitrary",)),
        name="win_sample",
    )(qn, win_buf, win_new)


def _fox_sample_kernel(pt_ref, *refs, npg, n_steps):
    kv_pages = refs[:npg]
    lf_pages = refs[npg:2 * npg]
    q_ref, new_ref, lfn_ref, o_ref, qbd_scr, m_scr, l_scr, acc_scr, carry_scr = refs[2 * npg:]
    st = pl.program_id(1)
    kvw = KV_FOX * HEAD_DIM

    @pl.when(st == 0)
    def _():
        qf = q_ref[0].astype(F32)
        zero = jnp.zeros((1, HEAD_DIM), F32)
        rows = [jnp.concatenate([qf[:, h * HEAD_DIM:(h + 1) * HEAD_DIM] if c == h // HPG_FOX else zero
                                 for c in range(KV_FOX)], axis=1) for h in range(H_FOX)]
        qbd_scr[...] = jnp.concatenate(rows, axis=0)
        m_scr[...] = jnp.full(m_scr.shape, M_FLOOR, F32)
        l_scr[...] = jnp.zeros(l_scr.shape, F32)
        acc_scr[...] = jnp.zeros(acc_scr.shape, F32)
        carry_scr[...] = lfn_ref[0]

    qbd = qbd_scr[...].astype(BF16)
    tri = jnp.where(lax.broadcasted_iota(I32, (PAGE, PAGE), 0) <= lax.broadcasted_iota(I32, (PAGE, PAGE), 1),
                    1.0, 0.0).astype(BF16)
    carry = carry_scr[...]
    parts = []
    def heads(ref, base):
        n = ref.shape[1] // 8
        return jnp.concatenate([ref[0, pl.ds(base + h, n, stride=8), :] for h in range(KV_FOX)], axis=1)

    for p in range(npg):
        k = heads(kv_pages[p], 0).astype(BF16)
        incl = _split_dot(lf_pages[p][0], tri)
        tot = incl[:, PAGE - 1:PAGE]
        parts.append(_nt_dot(qbd, k) * LN2 + (carry + (tot - incl)))
        carry = carry + tot
    carry_scr[...] = carry
    s = jnp.concatenate(parts, axis=1)
    m_prev = m_scr[...]
    m_new = jnp.maximum(m_prev, jnp.max(s, axis=1, keepdims=True))
    alpha = jnp.exp(m_prev - m_new)
    pr = jnp.exp(s - m_new)
    l_scr[...] = alpha * l_scr[...] + jnp.sum(pr, axis=1, keepdims=True)
    acc = alpha * acc_scr[...]
    for p in range(npg):
        v = heads(kv_pages[p], KV_FOX).astype(BF16)
        acc = acc + jnp.dot(pr[:, p * PAGE:(p + 1) * PAGE].astype(BF16), v, preferred_element_type=F32)
    acc_scr[...] = acc
    m_scr[...] = m_new

    @pl.when(st == n_steps - 1)
    def _():
        kn = heads(new_ref, 0)
        vn = heads(new_ref, KV_FOX)
        s_new = jnp.sum(qbd_scr[...] * kn, axis=1, keepdims=True) * LN2
        m_fin = jnp.maximum(m_scr[...], s_new)
        a2 = jnp.exp(m_scr[...] - m_fin)
        p_new = jnp.exp(s_new - m_fin)
        o = (a2 * acc_scr[...] + p_new * vn) / jnp.maximum(a2 * l_scr[...] + p_new, 1e-30)
        for h in range(H_FOX):
            c = (h // HPG_FOX) * HEAD_DIM
            o_ref[0, :, h * HEAD_DIM:(h + 1) * HEAD_DIM] = o[h:h + 1, c:c + HEAD_DIM].astype(o_ref.dtype)


def fox_sample_call(page_table, qf, kvf_new, lf_new, cache3, logf_t):
    bsz, n_pages = page_table.shape
    npg = FOX_PAGES
    assert n_pages % npg == 0
    n_steps = n_pages // npg
    page_of = lambda bb, st, pt, p: pt[bb, n_pages - 1 - (st * npg + p)]
    kv_specs = [pl.BlockSpec((1, PAGE * 8, HEAD_DIM), lambda bb, st, pt, p=p: (page_of(bb, st, pt, p), 0, 0))
                for p in range(npg)]
    lf_specs = [pl.BlockSpec((1, H_FOX, PAGE), lambda bb, st, pt, p=p: (page_of(bb, st, pt, p), 0, 0))
                for p in range(npg)]
    grid_spec = pltpu.PrefetchScalarGridSpec(
        num_scalar_prefetch=1,
        grid=(bsz, n_steps),
        in_specs=kv_specs + lf_specs + [pl.BlockSpec((1, 1, W_FOX), lambda bb, st, pt: (bb, 0, 0)),
                                        pl.BlockSpec((1, 8, HEAD_DIM), lambda bb, st, pt: (bb, 0, 0)),
                                        pl.BlockSpec((1, H_FOX, 1), lambda bb, st, pt: (bb, 0, 0))],
        out_specs=pl.BlockSpec((1, 1, W_FOX), lambda bb, st, pt: (bb, 0, 0)),
        scratch_shapes=[pltpu.VMEM((H_FOX, KV_FOX * HEAD_DIM), F32), pltpu.VMEM((H_FOX, 1), F32),
                        pltpu.VMEM((H_FOX, 1), F32), pltpu.VMEM((H_FOX, KV_FOX * HEAD_DIM), F32),
                        pltpu.VMEM((H_FOX, 1), F32)])
    return pl.pallas_call(
        functools.partial(_fox_sample_kernel, npg=npg, n_steps=n_steps),
        grid_spec=grid_spec,
        out_shape=jax.ShapeDtypeStruct((bsz, 1, W_FOX), BF16),
        compiler_params=_cparams(("arbitrary", "arbitrary")),
        name="fox_sample",
    )(page_table, *([cache3] * npg), *([logf_t] * npg), qf, kvf_new, lf_new)


def sample_mixer(x, sc, sh, g_pre, W, cmpw, cache_nsa3, win_buf, cache_fox3, logf_t, page_table):
    bsz = x.shape[1]
    n_pages = page_table.shape[1]
    past = n_pages * PAGE
    pb, kvcs, win, kvf, small = inproj_call(x, g_pre, sc, sh, W["wm"], W["ws"], W["bs"], bsz)
    per_seq = lambda a: a.reshape(bsz, -1, a.shape[-1])
    pb, kvcs, win, kvf, small = map(per_seq, (pb, kvcs, win, kvf, small))
    qn = pb[:, :, COL_QN:COL_QN + W_NSA]
    qf = pb[:, :, COL_QF:COL_QF + W_FOX]
    kvc = compress_sample_call(page_table, cache_nsa3, *cmpw)
    ocmp, idx = cmpsel_sample_call(qn, kvc, past=past)
    n_keep = min(N_SEL, -(-(past + 1) // SEL_BLOCK))
    idx_flat = idx[:, :G_NSA, :n_keep].reshape(-1)
    osel = sel_sample_call(idx_flat, page_table, qn, kvcs, cache_nsa3, past=past, n_keep=n_keep)
    owin, win_new = win_sample_call(qn, win_buf, win)
    logf = small[:, :, 3 * H_NSA:3 * H_NSA + H_FOX]
    ofox = fox_sample_call(page_table, qf, kvf, logf.reshape(bsz, H_FOX, 1), cache_fox3, logf_t)
    to_rows = lambda a: a.reshape(1, bsz, a.shape[-1])
    return tuple(map(to_rows, (ocmp, osel, owin, ofox, small))) + (kvcs, win_new, kvf, logf)


PROMPT_TM = 512
ATTN_TQ = 512
ATTN_TK = 512


def kernel(x_prompt, x_sample, c_prompt, c_sample, cache_nsa_kv, state_nsa_win, cache_fox_kv, cache_fox_logf,
           page_table, w_ada, b_ada, g_pre_mix, g_post_mix, g_pre_ffn, g_post_ffn, w_in, b_nsa_gate, b_fox_forget,
           w_cmp1, b_cmp1, w_cmp2, b_cmp2, pe_cmp, g_grp, w_out, w_router, b_router, w_up, b_up, w_down, b_down):
    assert w_ada.shape[0] == 1 and x_sample.shape[1] == 1
    bp, t, d = x_prompt.shape
    bs = x_sample.shape[0]
    n_phys = cache_nsa_kv.shape[1]
    W = prep_weights(w_in[0], b_nsa_gate[0], b_fox_forget[0], w_cmp1[0], w_out[0], w_router[0], b_router[0])
    hb = cmp_bias_call(pe_cmp[0].reshape(2, CMP_LEN // 2, 2 * HEAD_DIM), W["w1p"], b_cmp1[0])
    cmpw = (W["w1f"], hb, w_cmp2[0].astype(BF16), b_cmp2[0])
    row = lambda a: a[0][None, :]

    ada = ada_call(jnp.concatenate([c_prompt, c_sample], axis=0), w_ada[0], row(b_ada))
    mods_p = [m[:, None, :] for m in jnp.split(ada[:bp], 6, axis=-1)]
    mods_s = [m[None] for m in jnp.split(ada[bp:], 6, axis=-1)]
    xs = x_sample.reshape(1, bs, d)

    ocmp, osel, owin, ofox, small, kvcs_p, win_p, kvf_p, logf_p = prompt_mixer(
        x_prompt, mods_p[1], mods_p[0], row(g_pre_mix), W, cmpw,
        tm=PROMPT_TM, tq=ATTN_TQ, tk=ATTN_TK)
    merge_args = (row(g_grp), W["wout"], row(g_post_mix))
    ffn_args = (W["wrh"], W["wrl"], W["br"])
    x1_p, h2_p, route_p = merge_call(ocmp, osel, owin, ofox, small, x_prompt, *merge_args, mods_p[2],
                                     row(g_pre_ffn), mods_p[4], mods_p[3], *ffn_args, 256)

    cache_nsa3 = cache_nsa_kv.reshape(n_phys, PAGE * 8, HEAD_DIM)
    cache_fox3 = cache_fox_kv.reshape(n_phys, PAGE * 8, HEAD_DIM)
    logf_t = jnp.swapaxes(cache_fox_logf[0], 1, 2)
    ocmp_s, osel_s, owin_s, ofox_s, small_s, kvcs_s, win_s, kvf_s, logf_s = sample_mixer(
        xs, mods_s[1], mods_s[0], row(g_pre_mix), W, cmpw,
        cache_nsa3, state_nsa_win.reshape(bs, WINDOW * 4, HEAD_DIM), cache_fox3, logf_t, page_table)
    x1_s, h2_s, route_s = merge_call(ocmp_s, osel_s, owin_s, ofox_s, small_s, xs, *merge_args, mods_s[2],
                                     row(g_pre_ffn), mods_s[4], mods_s[3], *ffn_args, bs)

    n_p = bp * t
    h2_all = jnp.concatenate([h2_p.reshape(n_p, d), h2_s.reshape(bs, d)], axis=0)
    route_all = jnp.concatenate([route_p.reshape(n_p, LANES), route_s.reshape(bs, LANES)], axis=0)[:, :2 * TOP_K]
    slot_tok, slot_w, dest, blk_e, n_used = moe_dispatch(route_all)
    n_blk = blk_e.shape[0]
    a = None
    for c in range(MOE_CHUNKS):
        b0, b1 = c * n_blk // MOE_CHUNKS, (c + 1) * n_blk // MOE_CHUNKS
        xs = h2_all[slot_tok[b0 * MOE_TM:b1 * MOE_TM]]
        a = moe_up_call(blk_e, n_used, xs, w_up[0], b_up[0][:, None, :], a, b0)
    ncol = d // MOE_TN
    ys = [moe_down_call(blk_e, n_used, a, w_down[0], b_down[0][:, None, :], slot_w, c, 1) for c in range(ncol)]
    yg_p = [y[dest[:, :n_p]].reshape(TOP_K, bp, t, MOE_TN) for y in ys]
    yg_s = [y[dest[:, n_p:]].reshape(TOP_K, 1, bs, MOE_TN) for y in ys]
    y_p = final_call(yg_p, x1_p, mods_p[5], row(g_post_ffn), PROMPT_TM)
    y_s = final_call(yg_s, x1_s, mods_s[5], row(g_post_ffn), bs)

    wlen = min(WINDOW, t)
    return (y_p, y_s.reshape(bs, 1, d),
            kvcs_p.reshape(1, bp, t, 2, 2, G_NSA, HEAD_DIM),
            win_p[:, (t - wlen) * 4:].reshape(1, bp, wlen, 2, G_NSA, HEAD_DIM),
            kvf_p.reshape(1, bp, t, 2, KV_FOX, HEAD_DIM),
            logf_p[None],
            kvcs_s.reshape(1, bs, 1, 2, 2, G_NSA, HEAD_DIM),
            win_s.reshape(1, bs, WINDOW, 2, G_NSA, HEAD_DIM),
            kvf_s.reshape(1, bs, 1, 2, KV_FOX, HEAD_DIM),
            logf_s.reshape(1, bs, 1, H_FOX))
```

```python
import functools

import jax
import jax.numpy as jnp
from jax import lax
from jax.experimental import pallas as pl
from jax.experimental.pallas import tpu as pltpu

F32 = jnp.float32
BF16 = jnp.bfloat16
I32 = jnp.int32

D_MODEL = 2048
HEAD_DIM = 128
H_NSA = 8
H_FOX = 8
G_NSA = 2
HPG_NSA = 4
KV_FOX = 4
HPG_FOX = 2
W_NSA = H_NSA * HEAD_DIM
W_FOX = H_FOX * HEAD_DIM
CMP_LEN = 32
CMP_STRIDE = 16
CMP_HID = 256
SEL_BLOCK = 64
N_SEL = 16
N_LOCAL_SEL = 2
WINDOW = 512
N_EXPERTS = 32
TOP_K = 4
D_FF = 2048
SWIGLU_ALPHA = 1.702
SWIGLU_LIMIT = 7.0
RMS_EPS = 1e-6
ATTN_SCALE = HEAD_DIM ** -0.5
LOG2E = 1.4426950408889634
LN2 = 0.6931471805599453
Q_PRESCALE = ATTN_SCALE * LOG2E
FORCE_SCORE = 1e9
MASKED = -2e30
M_FLOOR = -1e30

LANES = 128
VMEM_LIMIT = 52 * 1024 * 1024

N_MAIN = 4608
PROJ_TN = 512
COL_KVCS, COL_WIN, COL_KVF, COL_QN, COL_QF = 0, 1024, 1536, 2560, 3584


def _cparams(sem, vmem=VMEM_LIMIT):
    return pltpu.CompilerParams(dimension_semantics=sem, vmem_limit_bytes=vmem)


def _rms(x, g):
    return x * lax.rsqrt(jnp.mean(x * x, axis=-1, keepdims=True) + RMS_EPS) * g


def _nt_dot(a, b):
    return lax.dot_general(a, b, (((1,), (1,)), ((), ())), preferred_element_type=F32)


def _ada_kernel(c_ref, w_ref, b_ref, o_ref):
    c = c_ref[...]
    a = (c * jax.nn.sigmoid(c)).astype(BF16)
    o_ref[...] = jnp.dot(a, w_ref[...].astype(BF16), preferred_element_type=F32) + b_ref[...]


def ada_call(c, w, b):
    bc, d = c.shape
    n = w.shape[1]
    tn = 1024
    return pl.pallas_call(
        _ada_kernel,
        grid=(n // tn,),
        in_specs=[pl.BlockSpec((bc, d), lambda j: (0, 0)),
                  pl.BlockSpec((d, tn), lambda j: (0, j)),
                  pl.BlockSpec((1, tn), lambda j: (0, j))],
        out_specs=pl.BlockSpec((bc, tn), lambda j: (0, j)),
        out_shape=jax.ShapeDtypeStruct((bc, n), F32),
        compiler_params=_cparams(("arbitrary",)),
        name="ada",
    )(c, w, b)


def _inproj_kernel(x_ref, g_ref, sc_ref, sh_ref, wm_ref, ws_ref, bs_ref,
                   pb_ref, kvcs_ref, win_ref, kvf_ref, small_ref, h_scr):
    j = pl.program_id(2)

    @pl.when(j == 0)
    def _():
        h = _rms(x_ref[0], g_ref[...]) * (1.0 + sc_ref[0]) + sh_ref[0]
        hb = h.astype(BF16)
        h_scr[...] = hb
        z = jnp.dot(hb, ws_ref[...], preferred_element_type=F32) + bs_ref[...]
        lane = lax.broadcasted_iota(I32, z.shape, 1)
        small_ref[0] = jnp.where(lane < 3 * H_NSA, jax.nn.sigmoid(z), jax.nn.log_sigmoid(z))

    r = jnp.dot(h_scr[...], wm_ref[...], preferred_element_type=F32)
    pb_ref[0] = (r * jnp.where(j >= COL_QN // PROJ_TN, Q_PRESCALE, 1.0)).astype(BF16)
    tm = r.shape[0]

    def scatter_rows(ref, base, per_tok):
        for c in range(PROJ_TN // HEAD_DIM):
            ref[0, pl.ds(base + c, tm, stride=per_tok), :] = r[:, c * HEAD_DIM:(c + 1) * HEAD_DIM]

    for step, (ref, base, per_tok) in enumerate(((kvcs_ref, 0, 8), (kvcs_ref, 4, 8), (win_ref, 0, 4),
                                                 (kvf_ref, 0, 8), (kvf_ref, 4, 8))):
        pl.when(j == step)(functools.partial(scatter_rows, ref, base, per_tok))


def inproj_call(x, g, sc, sh, wm, ws, bs, tm):
    b, t, d = x.shape
    per_row = sc.shape[1] != 1
    nj = N_MAIN // PROJ_TN
    mod_spec = (pl.BlockSpec((1, tm, d), lambda bb, i, j: (bb, i, 0)) if per_row
                else pl.BlockSpec((1, 1, d), lambda bb, i, j: (bb, 0, 0)))
    return pl.pallas_call(
        _inproj_kernel,
        grid=(b, t // tm, nj),
        in_specs=[pl.BlockSpec((1, tm, d), lambda bb, i, j: (bb, i, 0)),
                  pl.BlockSpec((1, d), lambda bb, i, j: (0, 0)),
                  mod_spec, mod_spec,
                  pl.BlockSpec((d, PROJ_TN), lambda bb, i, j: (0, j)),
                  pl.BlockSpec((d, LANES), lambda bb, i, j: (0, 0)),
                  pl.BlockSpec((1, LANES), lambda bb, i, j: (0, 0))],
        out_specs=[pl.BlockSpec((1, tm, PROJ_TN), lambda bb, i, j: (bb, i, j)),
                   pl.BlockSpec((1, tm * 8, HEAD_DIM), lambda bb, i, j: (bb, i, 0)),
                   pl.BlockSpec((1, tm * 4, HEAD_DIM), lambda bb, i, j: (bb, i, 0)),
                   pl.BlockSpec((1, tm * 8, HEAD_DIM), lambda bb, i, j: (bb, i, 0)),
                   pl.BlockSpec((1, tm, LANES), lambda bb, i, j: (bb, i, 0))],
        out_shape=[jax.ShapeDtypeStruct((b, t, N_MAIN), BF16),
                   jax.ShapeDtypeStruct((b, t * 8, HEAD_DIM), F32),
                   jax.ShapeDtypeStruct((b, t * 4, HEAD_DIM), F32),
                   jax.ShapeDtypeStruct((b, t * 8, HEAD_DIM), F32),
                   jax.ShapeDtypeStruct((b, t, LANES), F32)],
        scratch_shapes=[pltpu.VMEM((tm, d), BF16)],
        compiler_params=_cparams(("arbitrary", "arbitrary", "arbitrary")),
        name="inproj",
    )(x, g, sc, sh, wm, ws, bs)


def _lane_cumsum(x):
    lane = lax.broadcasted_iota(I32, x.shape, 1)
    d = 1
    while d < LANES:
        x = x + jnp.where(lane >= d, pltpu.roll(x, d, axis=1), 0.0)
        d *= 2
    return x


def _cumsum_kernel(x_ref, o_ref):
    t = x_ref.shape[2]
    carry = jnp.zeros((x_ref.shape[1], 1), F32)
    for c in range(t // LANES):
        sl = slice(c * LANES, (c + 1) * LANES)
        y = _lane_cumsum(x_ref[0, :, sl]) + carry
        o_ref[0, :, sl] = y
        carry = y[:, LANES - 1:LANES]


def cumsum_call(x):
    b, h, t = x.shape
    return pl.pallas_call(
        _cumsum_kernel,
        grid=(b,),
        in_specs=[pl.BlockSpec((1, h, t), lambda bb: (bb, 0, 0))],
        out_specs=pl.BlockSpec((1, h, t), lambda bb: (bb, 0, 0)),
        out_shape=jax.ShapeDtypeStruct((b, h, t), F32),
        compiler_params=_cparams(("arbitrary",)),
        name="logf_cumsum",
    )(x)


def _nsa_slope(g, h):
    return jnp.where(g == 0, 2.0 ** -(h + 1), 2.0 ** -(HPG_NSA + h + 1)).astype(F32)


def _flash_schedule(mode, t, tq, tk):
    qi, ki, fl = [], [], []
    for i in range(t // tq):
        lo = max(0, (i * tq - WINDOW + 1) // tk) if mode == "win" else 0
        hi = (i * tq + tq - 1) // tk
        for kb in range(lo, hi + 1):
            below_diag = kb * tk + tk - 1 <= i * tq
            in_window = mode != "win" or (i * tq + tq - 1) - kb * tk < WINDOW
            qi.append(i)
            ki.append(kb)
            fl.append((1 if kb == lo else 0) | (2 if kb == hi else 0) | (0 if below_diag and in_window else 4))
    return tuple(jnp.asarray(a, I32) for a in (qi, ki, fl))


def _flash_kernel(qi_ref, ki_ref, fl_ref, *refs, mode, hpg, tq, tk):
    if mode == "fox":
        q_ref, k_ref, v_ref, cq_ref, ck_ref, o_ref, m_scr, acc_scr = refs
    elif mode == "sel":
        q_ref, k_ref, v_ref, sm_ref, o_ref, m_scr, acc_scr = refs
    else:
        q_ref, k_ref, v_ref, o_ref, m_scr, acc_scr = refs
    g = pl.program_id(1)
    p_idx = pl.program_id(2)
    q0 = qi_ref[p_idx] * tq
    k0 = ki_ref[p_idx] * tk
    flags = fl_ref[p_idx]

    @pl.when((flags & 1) != 0)
    def _():
        m_scr[...] = jnp.full(m_scr.shape, M_FLOOR, F32)
        acc_scr[...] = jnp.zeros(acc_scr.shape, F32)

    def step(positional):
        k = k_ref[0]
        v = v_ref[0]
        mask = None
        if positional:
            dist = (q0 + lax.broadcasted_iota(I32, (tq, tk), 0)) - (k0 + lax.broadcasted_iota(I32, (tq, tk), 1))
        if mode == "win":
            mask = (lax.bitcast_convert_type(dist, jnp.uint32) < WINDOW) if positional else None
        elif mode == "fox":
            mask = (dist >= 0) if positional else None
        else:
            blk = (k0 + lax.broadcasted_iota(I32, (sm_ref.shape[3], tk), 1)) // SEL_BLOCK
            expand = jnp.where(blk == lax.broadcasted_iota(I32, blk.shape, 0), 1.0, 0.0).astype(BF16)
            chosen = jnp.dot(sm_ref[0, 0], expand, preferred_element_type=F32)
            mask = (jnp.where(dist >= 0, chosen, 0.0) if positional else chosen) > 0.5
        krel = (k0 - q0 + lax.broadcasted_iota(I32, (1, tk), 1)).astype(F32)
        v_ones = jnp.concatenate([v, jnp.ones((tk, HEAD_DIM), BF16)], axis=1)
        for h in range(hpg):
            q = q_ref[0, :, h * HEAD_DIM:(h + 1) * HEAD_DIM]
            if mode == "fox":
                bias = (cq_ref[0, 0, h:h + 1, 0:1] - ck_ref[0, 0, h:h + 1, :]) * LOG2E
            else:
                bias = (_nsa_slope(g, h) * LOG2E) * krel
            s = _nt_dot(q, k) + bias
            if mask is not None:
                s = jnp.where(mask, s, MASKED)
            m_prev = m_scr[h]
            m_new = jnp.maximum(m_prev, jnp.max(s, axis=1, keepdims=True))
            alpha = jnp.exp2(m_prev - m_new)
            p = jnp.exp2(s - jnp.concatenate([m_new] * (tk // LANES), axis=1))
            pv = jnp.dot(p.astype(BF16), v_ones, preferred_element_type=F32)
            acc_scr[h] = jnp.concatenate([alpha, alpha], axis=1) * acc_scr[h] + pv
            m_scr[h] = m_new

    pl.when((flags & 4) != 0)(functools.partial(step, True))
    pl.when((flags & 4) == 0)(functools.partial(step, False))

    @pl.when((flags & 2) != 0)
    def _():
        for h in range(hpg):
            o = acc_scr[h, :, :HEAD_DIM] / jnp.maximum(acc_scr[h, :, HEAD_DIM:], 1e-30)
            o_ref[0, :, h * HEAD_DIM:(h + 1) * HEAD_DIM] = o.astype(o_ref.dtype)


def flash_call(mode, pb, *, tq, tk, extra=()):
    b, t, _ = pb.shape
    if mode == "fox":
        hpg, ngrp = HPG_FOX, KV_FOX
        qcol, kcol, vcol = COL_QF // (hpg * HEAD_DIM), COL_KVF // HEAD_DIM, COL_KVF // HEAD_DIM + KV_FOX
    else:
        hpg, ngrp = HPG_NSA, G_NSA
        base = (COL_KVCS + 512) if mode == "sel" else COL_WIN
        qcol, kcol, vcol = COL_QN // (hpg * HEAD_DIM), base // HEAD_DIM, base // HEAD_DIM + G_NSA
    qi, ki, fl = _flash_schedule(mode, t, tq, tk)
    in_specs = [pl.BlockSpec((1, tq, hpg * HEAD_DIM), lambda bb, g, p, qi, ki, fl: (bb, qi[p], qcol + g)),
                pl.BlockSpec((1, tk, HEAD_DIM), lambda bb, g, p, qi, ki, fl: (bb, ki[p], kcol + g)),
                pl.BlockSpec((1, tk, HEAD_DIM), lambda bb, g, p, qi, ki, fl: (bb, ki[p], vcol + g))]
    args = [pb, pb, pb]
    if mode == "fox":
        c8 = extra[0]
        in_specs += [pl.BlockSpec((1, 1, 8, tq), lambda bb, g, p, qi, ki, fl: (bb, g, 0, qi[p])),
                     pl.BlockSpec((1, 1, 8, tk), lambda bb, g, p, qi, ki, fl: (bb, g, 0, ki[p]))]
        args += [c8, c8]
    elif mode == "sel":
        sm = extra[0]
        in_specs += [pl.BlockSpec((1, 1, tq, sm.shape[3]), lambda bb, g, p, qi, ki, fl: (bb, g, qi[p], 0))]
        args += [sm]
    grid_spec = pltpu.PrefetchScalarGridSpec(
        num_scalar_prefetch=3,
        grid=(b, ngrp, int(qi.shape[0])),
        in_specs=in_specs,
        out_specs=pl.BlockSpec((1, tq, hpg * HEAD_DIM), lambda bb, g, p, qi, ki, fl: (bb, qi[p], g)),
        scratch_shapes=[pltpu.VMEM((hpg, tq, LANES), F32), pltpu.VMEM((hpg, tq, 2 * HEAD_DIM), F32)])
    return pl.pallas_call(
        functools.partial(_flash_kernel, mode=mode, hpg=hpg, tq=tq, tk=tk),
        grid_spec=grid_spec,
        out_shape=jax.ShapeDtypeStruct((b, t, ngrp * hpg * HEAD_DIM), BF16),
        compiler_params=_cparams(("arbitrary",) * 3),
        name="flash_" + mode,
    )(qi, ki, fl, *args)


def _cmp_bias_kernel(pe_ref, w1_ref, b1_ref, o_ref):
    for j in range(2):
        acc = jnp.zeros((8, CMP_HID), F32)
        for p in range(CMP_LEN // 2):
            acc = acc + _split_dot(jnp.broadcast_to(pe_ref[j, p:p + 1, :], (8, 2 * HEAD_DIM)), w1_ref[j, p])
        o_ref[j] = acc + b1_ref[j:j + 1, :]


def cmp_bias_call(pe2, w1p, b1):
    return pl.pallas_call(_cmp_bias_kernel, out_shape=jax.ShapeDtypeStruct((2, 8, CMP_HID), F32),
                          compiler_params=_cparams(None), name="cmp_bias")(pe2, w1p, b1)


def _compress_rows(load, n, w1_ref, hb_ref, w2_ref, b2_ref):
    outs = []
    for j in range(2):
        both = jnp.zeros((2 * n, 2 * CMP_HID), F32)
        for sp in range(CMP_STRIDE // 2):
            parts = [jnp.concatenate([load(j * G_NSA + g, 2 * sp + u) for u in range(2)], axis=1)
                     for g in range(G_NSA)]
            both = both + jnp.dot(jnp.concatenate(parts, axis=0).astype(BF16), w1_ref[j, sp],
                                  preferred_element_type=F32)
        for g in range(G_NSA):
            f = both[g * n:(g + 1) * n, :CMP_HID]
            s = pltpu.roll(both[g * n:(g + 1) * n, CMP_HID:], n - 1, axis=0)
            hid = jax.nn.gelu(f + s + hb_ref[j, 0:1, :])
            outs.append(jnp.dot(hid.astype(BF16), w2_ref[j], preferred_element_type=F32) + b2_ref[j:j + 1, :])
    return jnp.concatenate(outs, axis=1)


def _compress_prompt_kernel(x_ref, w1_ref, hb_ref, w2_ref, b2_ref, o_ref):
    n = o_ref.shape[1]
    load = lambda cb, s: x_ref[0, pl.ds(s * 8 + cb, n, stride=CMP_STRIDE * 8), :]
    o_ref[0] = _compress_rows(load, n, w1_ref, hb_ref, w2_ref, b2_ref).astype(o_ref.dtype)


def _cmp_weight_specs():
    def const(shape):
        return pl.BlockSpec(shape, lambda *a: (0,) * len(shape))
    return [const((2, CMP_STRIDE // 2, 2 * HEAD_DIM, 2 * CMP_HID)), const((2, 8, CMP_HID)),
            const((2, CMP_HID, HEAD_DIM)), const((2, HEAD_DIM))]


def compress_prompt_call(kvcs, w1f, hb, w2, b2):
    b, t8, _ = kvcs.shape
    n = t8 // 8 // CMP_STRIDE
    return pl.pallas_call(
        _compress_prompt_kernel,
        grid=(b,),
        in_specs=[pl.BlockSpec((1, t8, HEAD_DIM), lambda bb: (bb, 0, 0))] + _cmp_weight_specs(),
        out_specs=pl.BlockSpec((1, n, 512), lambda bb: (bb, 0, 0)),
        out_shape=jax.ShapeDtypeStruct((b, n, 512), BF16),
        compiler_params=_cparams(("arbitrary",)),
        name="compress_prompt",
    )(kvcs, w1f, hb, w2, b2)


def _rank_select_cols(score_t, n_keep):
    ns = score_t.shape[0]
    row_id = lax.broadcasted_iota(I32, score_t.shape, 0)
    rank = jnp.zeros(score_t.shape, F32)
    for c in range(ns):
        row = score_t[c:c + 1, :]
        rank = rank + jnp.where(row_id > c, jnp.where(row >= score_t, 1.0, 0.0), jnp.where(row > score_t, 1.0, 0.0))
    return jnp.where(rank < n_keep, 1.0, 0.0)


def _split_dot(a, b_bf16):
    hi = a.astype(BF16)
    r1 = a - hi.astype(F32)
    mid = r1.astype(BF16)
    lo = (r1 - mid.astype(F32)).astype(BF16)
    return (jnp.dot(hi, b_bf16, preferred_element_type=F32) + jnp.dot(mid, b_bf16, preferred_element_type=F32)
            + jnp.dot(lo, b_bf16, preferred_element_type=F32))


def _overlap_matrix(nc, ns):
    ci = lax.broadcasted_iota(I32, (nc, ns), 0) * CMP_STRIDE
    sj = lax.broadcasted_iota(I32, (nc, ns), 1)
    return jnp.where(ci < (sj + 1) * SEL_BLOCK, jnp.where(ci + CMP_LEN > sj * SEL_BLOCK, 1.0, 0.0), 0.0).astype(BF16)


def _force_scores(score, blk, lag):
    recent = lax.bitcast_convert_type(lag, jnp.uint32) < N_LOCAL_SEL
    score = jnp.where(recent, FORCE_SCORE, jnp.where(lag >= 0, score, -FORCE_SCORE))
    return jnp.where(blk == 0, FORCE_SCORE, score)


def _cmpsel_prompt_kernel(q_ref, k_ref, v_ref, o_ref, sm_ref, *, tq, n_cmp, n_sel, n_keep):
    g = pl.program_id(1)
    i = pl.program_id(2)
    nc = k_ref.shape[1]
    k = k_ref[0]
    v = v_ref[0]
    qpos = i * tq + lax.broadcasted_iota(I32, (tq, nc), 0)
    cidx = lax.broadcasted_iota(I32, (tq, nc), 1)
    mask = jnp.where(cidx < n_cmp, cidx * CMP_STRIDE + CMP_LEN - 1, 2 ** 30) <= qpos
    center = (lax.broadcasted_iota(I32, (1, nc), 1) * CMP_STRIDE - i * tq).astype(F32) + 0.5 * (CMP_LEN - 1)
    imp = jnp.zeros((tq, nc), F32)
    for h in range(HPG_NSA):
        q = q_ref[0, :, h * HEAD_DIM:(h + 1) * HEAD_DIM]
        s = _nt_dot(q, k) * LN2 + _nsa_slope(g, h) * center
        s = jnp.where(mask, s, -1e30)
        e = jnp.where(mask, jnp.exp(s - jnp.max(s, axis=1, keepdims=True)), 0.0)
        p = e / jnp.maximum(jnp.sum(e, axis=1, keepdims=True), 1e-30)
        o_ref[0, :, h * HEAD_DIM:(h + 1) * HEAD_DIM] = jnp.dot(
            p.astype(BF16), v, preferred_element_type=F32).astype(o_ref.dtype)
        imp = imp + p
    ns_pad = -(-n_sel // LANES) * LANES
    ns8 = -(-n_sel // 8) * 8
    score_t = _split_dot(imp, _overlap_matrix(nc, ns_pad)).T[:ns8]
    blk = lax.broadcasted_iota(I32, (ns8, tq), 0)
    lag = (i * tq + lax.broadcasted_iota(I32, (ns8, tq), 1)) // SEL_BLOCK - blk
    score_t = jnp.where(blk < n_sel, _force_scores(score_t, blk, lag), -2.0 * FORCE_SCORE)
    chosen_t = jnp.concatenate([_rank_select_cols(score_t, n_keep), jnp.zeros((ns_pad - ns8, tq), F32)], axis=0)
    sm_ref[0, 0] = chosen_t.T[:, :n_sel].astype(sm_ref.dtype)


def cmpsel_prompt_call(pb, kvc, *, tq):
    b, t, _ = pb.shape
    nc = kvc.shape[1]
    n_cmp = nc - 1
    n_sel = -(-t // SEL_BLOCK)
    n_keep = min(N_SEL, n_sel)
    qcol = COL_QN // (HPG_NSA * HEAD_DIM)
    return pl.pallas_call(
        functools.partial(_cmpsel_prompt_kernel, tq=tq, n_cmp=n_cmp, n_sel=n_sel, n_keep=n_keep),
        grid=(b, G_NSA, t // tq),
        in_specs=[pl.BlockSpec((1, tq, HPG_NSA * HEAD_DIM), lambda bb, g, i: (bb, i, qcol + g)),
                  pl.BlockSpec((1, nc, HEAD_DIM), lambda bb, g, i: (bb, 0, g)),
                  pl.BlockSpec((1, nc, HEAD_DIM), lambda bb, g, i: (bb, 0, G_NSA + g))],
        out_specs=[pl.BlockSpec((1, tq, HPG_NSA * HEAD_DIM), lambda bb, g, i: (bb, i, g)),
                   pl.BlockSpec((1, 1, tq, n_sel), lambda bb, g, i: (bb, g, i, 0))],
        out_shape=[jax.ShapeDtypeStruct((b, t, W_NSA), BF16),
                   jax.ShapeDtypeStruct((b, G_NSA, t, n_sel), BF16)],
        compiler_params=_cparams(("arbitrary",) * 3),
        name="cmpsel_prompt",
    )(pb, kvc, kvc)


def _merge_kernel(ocmp_ref, osel_ref, owin_ref, ofox_ref, small_ref, x_ref, ggrp_ref, wout_ref, gpost_ref,
                  gt_ref, gpre_ref, sc_ref, sh_ref, wrh_ref, wrl_ref, br_ref, cnt0_ref,
                  x1_ref, h2_ref, route_ref, cnt_ref):
    gs = small_ref[0]
    parts = []
    for h in range(H_NSA):
        sl = slice(h * HEAD_DIM, (h + 1) * HEAD_DIM)
        parts.append(gs[:, h:h + 1] * ocmp_ref[0, :, sl].astype(F32)
                     + gs[:, H_NSA + h:H_NSA + h + 1] * osel_ref[0, :, sl].astype(F32)
                     + gs[:, 2 * H_NSA + h:2 * H_NSA + h + 1] * owin_ref[0, :, sl].astype(F32))
    o_nsa = jnp.concatenate(parts, axis=1)
    y = jnp.concatenate([_rms(o_nsa, ggrp_ref[:, :W_NSA]),
                         _rms(ofox_ref[0].astype(F32), ggrp_ref[:, W_NSA:])], axis=1).astype(BF16)
    m = jnp.dot(y, wout_ref[...], preferred_element_type=F32)
    x1 = x_ref[0] + gt_ref[0] * _rms(m, gpost_ref[...])
    x1_ref[0] = x1
    h2 = _rms(x1, gpre_ref[...]) * (1.0 + sc_ref[0]) + sh_ref[0]
    hi = h2.astype(BF16)
    h2_ref[0] = hi
    lo = (h2 - hi.astype(F32)).astype(BF16)
    logits = (jnp.dot(hi, wrh_ref[...], preferred_element_type=F32) + jnp.dot(hi, wrl_ref[...], preferred_element_type=F32)
              + jnp.dot(lo, wrh_ref[...], preferred_element_type=F32) + br_ref[...])
    lane = lax.broadcasted_iota(I32, logits.shape, 1)
    vals = jnp.where(lane < N_EXPERTS, logits, -jnp.inf)
    top_v, top_e = [], []
    for _ in range(TOP_K):
        mx = jnp.max(vals, axis=1, keepdims=True)
        idx = jnp.min(jnp.where(vals == mx, lane, LANES), axis=1, keepdims=True)
        top_v.append(mx)
        top_e.append(idx)
        vals = jnp.where(lane == idx, -jnp.inf, vals)
    ex = [jnp.exp(v - top_v[0]) for v in top_v]
    den = ex[0] + ex[1] + ex[2] + ex[3]
    @pl.when((pl.program_id(0) == 0) & (pl.program_id(1) == 0))
    def _():
        cnt_ref[...] = cnt0_ref[...]

    tm = logits.shape[0]
    hits = [jnp.where(lane == e, 1.0, 0.0) for e in top_e]
    per_tok = hits[0] + hits[1] + hits[2] + hits[3]
    earlier = jnp.where(lax.broadcasted_iota(I32, (tm, tm), 1) < lax.broadcasted_iota(I32, (tm, tm), 0),
                        1.0, 0.0).astype(BF16)
    before = cnt_ref[0:1, :] + jnp.dot(earlier, per_tok.astype(BF16), preferred_element_type=F32)
    cnt_ref[...] = cnt_ref[...] + jnp.sum(per_tok, axis=0, keepdims=True)
    route = jnp.zeros(logits.shape, F32)
    for kx in range(TOP_K):
        route = jnp.where(lane == kx, top_e[kx].astype(F32), route)
        route = jnp.where(lane == TOP_K + kx, ex[kx] / den, route)
        route = jnp.where(lane == 2 * TOP_K + kx, jnp.sum(hits[kx] * before, axis=1, keepdims=True), route)
    route_ref[0] = route


def merge_call(ocmp, osel, owin, ofox, small, x, ggrp, wout, gpost, gt, gpre, sc, sh, wrh, wrl, br, cnt0, tm):
    b, t, d = x.shape
    per_row = sc.shape[1] != 1
    row = lambda w: pl.BlockSpec((1, tm, w), lambda bb, i: (bb, i, 0))
    const = lambda shape: pl.BlockSpec(shape, lambda bb, i: (0,) * len(shape))
    mod = row(d) if per_row else pl.BlockSpec((1, 1, d), lambda bb, i: (bb, 0, 0))
    return pl.pallas_call(
        _merge_kernel,
        grid=(b, t // tm),
        in_specs=[row(W_NSA), row(W_NSA), row(W_NSA), row(W_FOX), row(LANES), row(d),
                  const((1, d)), const((d, d)), const((1, d)), mod, const((1, d)), mod, mod,
                  const((d, LANES)), const((d, LANES)), const((1, LANES)), const((8, LANES))],
        out_specs=[row(d), row(d), row(LANES), const((8, LANES))],
        out_shape=[jax.ShapeDtypeStruct((b, t, d), F32), jax.ShapeDtypeStruct((b, t, d), BF16),
                   jax.ShapeDtypeStruct((b, t, LANES), F32), jax.ShapeDtypeStruct((8, LANES), F32)],
        compiler_params=_cparams(("arbitrary", "arbitrary")),
        name="merge",
    )(ocmp, osel, owin, ofox, small, x, ggrp, wout, gpost, gt, gpre, sc, sh, wrh, wrl, br, cnt0)


MOE_TM = 512
MOE_TF = 1024
MOE_TN = 1024
MOE_CHUNKS = 4


def _expert_changed(blk_e_ref, i):
    return (i == 0) | (blk_e_ref[i] != blk_e_ref[jnp.maximum(i - 1, 0)])


def _moe_up_kernel(blk_e_ref, nused_ref, x_ref, wg_ref, wl_ref, bg_ref, bl_ref, a_ref, wg_bf, wl_bf, *, blk0):
    step = pl.program_id(1)
    i = blk0 + step
    live = i < nused_ref[0]

    @pl.when(live & ((step == 0) | (blk_e_ref[i] != blk_e_ref[jnp.maximum(i - 1, 0)])))
    def _():
        wg_bf[...] = wg_ref[0].astype(BF16)
        wl_bf[...] = wl_ref[0].astype(BF16)

    @pl.when(live)
    def _():
        x = x_ref[...]
        ug = jnp.dot(x, wg_bf[...], preferred_element_type=F32) + bg_ref[0]
        ul = jnp.dot(x, wl_bf[...], preferred_element_type=F32) + bl_ref[0]
        glu = jnp.minimum(ug, SWIGLU_LIMIT)
        lin = jnp.clip(ul, -SWIGLU_LIMIT, SWIGLU_LIMIT)
        a_ref[...] = (glu * jax.nn.sigmoid(SWIGLU_ALPHA * glu) * (lin + 1.0)).astype(a_ref.dtype)

    @pl.when(jnp.logical_not(live))
    def _():
        a_ref[...] = jnp.zeros(a_ref.shape, a_ref.dtype)


def moe_up_call(blk_e, n_used, xs, w_up, b_up, blk0):
    rows, d = xs.shape
    nf = D_FF // MOE_TF
    expert = lambda be, i: be[blk0 + i]
    grid_spec = pltpu.PrefetchScalarGridSpec(
        num_scalar_prefetch=2,
        grid=(nf, rows // MOE_TM),
        in_specs=[pl.BlockSpec((MOE_TM, d), lambda f, i, be, nu: (i, 0)),
                  pl.BlockSpec((1, d, MOE_TF), lambda f, i, be, nu: (expert(be, i), 0, f)),
                  pl.BlockSpec((1, d, MOE_TF), lambda f, i, be, nu: (expert(be, i), 0, nf + f)),
                  pl.BlockSpec((1, 1, MOE_TF), lambda f, i, be, nu: (expert(be, i), 0, f)),
                  pl.BlockSpec((1, 1, MOE_TF), lambda f, i, be, nu: (expert(be, i), 0, nf + f))],
        out_specs=pl.BlockSpec((MOE_TM, MOE_TF), lambda f, i, be, nu: (i, f)),
        scratch_shapes=[pltpu.VMEM((d, MOE_TF), BF16), pltpu.VMEM((d, MOE_TF), BF16)])
    return pl.pallas_call(
        functools.partial(_moe_up_kernel, blk0=blk0),
        grid_spec=grid_spec,
        out_shape=jax.ShapeDtypeStruct((rows, D_FF), BF16),
        compiler_params=_cparams(("arbitrary", "arbitrary")),
        name="moe_up",
    )(blk_e, n_used, xs, w_up, w_up, b_up, b_up)


def _moe_down_kernel(blk_e_ref, nused_ref, *refs, bounds):
    a_refs = refs[:len(bounds)]
    wd_ref, bd_ref, sw_ref, y_ref, wd_bf = refs[len(bounds):]
    i = pl.program_id(1)

    @pl.when(_expert_changed(blk_e_ref, i))
    def _():
        wd_bf[...] = wd_ref[0].astype(BF16)

    def project(a_ref):
        y = jnp.dot(a_ref[...], wd_bf[...], preferred_element_type=F32) + bd_ref[0]
        y_ref[...] = (y * sw_ref[...]).astype(y_ref.dtype)

    for (b0, b1), a_ref in zip(bounds, a_refs):
        pl.when((i >= b0) & (i < b1) & (i < nused_ref[0]))(functools.partial(project, a_ref))

    @pl.when(i >= nused_ref[0])
    def _():
        y_ref[...] = jnp.zeros(y_ref.shape, y_ref.dtype)


def moe_down_call(blk_e, n_used, a_chunks, bounds, w_down, b_down, slot_w, col0, ncol):
    dff = a_chunks[0].shape[1]
    n_blk = blk_e.shape[0]
    a_specs = [pl.BlockSpec((MOE_TM, dff), lambda c, i, be, nu, b0=b0, b1=b1: (jnp.clip(i - b0, 0, b1 - b0 - 1), 0))
               for b0, b1 in bounds]
    grid_spec = pltpu.PrefetchScalarGridSpec(
        num_scalar_prefetch=2,
        grid=(ncol, n_blk),
        in_specs=a_specs + [pl.BlockSpec((1, dff, MOE_TN), lambda c, i, be, nu: (be[i], 0, col0 + c)),
                            pl.BlockSpec((1, 1, MOE_TN), lambda c, i, be, nu: (be[i], 0, col0 + c)),
                            pl.BlockSpec((MOE_TM, 1), lambda c, i, be, nu: (i, 0))],
        out_specs=pl.BlockSpec((MOE_TM, MOE_TN), lambda c, i, be, nu: (i, c)),
        scratch_shapes=[pltpu.VMEM((dff, MOE_TN), BF16)])
    return pl.pallas_call(
        functools.partial(_moe_down_kernel, bounds=tuple(bounds)),
        grid_spec=grid_spec,
        out_shape=jax.ShapeDtypeStruct((n_blk * MOE_TM, ncol * MOE_TN), BF16),
        compiler_params=_cparams(("arbitrary", "arbitrary")),
        name="moe_down",
    )(blk_e, n_used, *a_chunks, w_down, b_down, slot_w)


def moe_dispatch(route, counts):
    n = route.shape[0]
    n_asg = n * TOP_K
    n_blk = -(-(n_asg + N_EXPERTS * (MOE_TM - 1)) // MOE_TM)
    e_flat = route[:, :TOP_K].astype(I32).reshape(-1)
    w_flat = route[:, TOP_K:2 * TOP_K].reshape(-1)
    pos = route[:, 2 * TOP_K:3 * TOP_K].astype(I32).reshape(-1)
    order = jnp.argsort(e_flat).astype(I32)
    grp_start = jnp.cumsum(counts) - counts
    padded = (counts + MOE_TM - 1) // MOE_TM * MOE_TM
    pad_end = jnp.cumsum(padded)
    pad_start = pad_end - padded
    blk_e = jnp.minimum(jnp.searchsorted(pad_end, jnp.arange(n_blk, dtype=I32) * MOE_TM, side="right"),
                        N_EXPERTS - 1).astype(I32)
    n_used = (pad_end[-1:] // MOE_TM).astype(I32)
    off = (jnp.arange(n_blk * MOE_TM, dtype=I32).reshape(n_blk, MOE_TM) - pad_start[blk_e][:, None])
    live = (off < counts[blk_e][:, None]).reshape(-1)
    src = order[jnp.clip(grp_start[blk_e][:, None] + off, 0, n_asg - 1).reshape(-1)]
    slot_tok = jnp.where(live, src // TOP_K, 0)
    slot_w = jnp.where(live, w_flat[src], 0.0)
    dest = (pos + pad_start[e_flat]).reshape(n, TOP_K).T
    return slot_tok, slot_w[:, None], dest, blk_e, n_used


def _final_kernel(*refs):
    *yg_refs, x1_ref, gt_ref, gpost_ref, o_ref = refs
    parts = []
    for yg_ref in yg_refs:
        s = yg_ref[0, 0].astype(F32)
        for kx in range(1, TOP_K):
            s = s + yg_ref[kx, 0].astype(F32)
        parts.append(s)
    o_ref[0] = x1_ref[0] + gt_ref[0] * _rms(jnp.concatenate(parts, axis=1), gpost_ref[...])


def final_call(ygs, x1, gt, gpost, tm):
    b, t, d = x1.shape
    per_row = gt.shape[1] != 1
    mod = (pl.BlockSpec((1, tm, d), lambda bb, i: (bb, i, 0)) if per_row
           else pl.BlockSpec((1, 1, d), lambda bb, i: (bb, 0, 0)))
    return pl.pallas_call(
        _final_kernel,
        grid=(b, t // tm),
        in_specs=[pl.BlockSpec((TOP_K, 1, tm, yg.shape[-1]), lambda bb, i: (0, bb, i, 0)) for yg in ygs]
                 + [pl.BlockSpec((1, tm, d), lambda bb, i: (bb, i, 0)),
                    mod, pl.BlockSpec((1, d), lambda bb, i: (0, 0))],
        out_specs=pl.BlockSpec((1, tm, d), lambda bb, i: (bb, i, 0)),
        out_shape=jax.ShapeDtypeStruct((b, t, d), F32),
        compiler_params=_cparams(("arbitrary", "arbitrary")),
        name="final",
    )(*ygs, x1, gt, gpost)


def prep_weights(w_in, b_nsa_gate, b_fox_forget, w_cmp1, w_out, w_router, b_router):
    d = w_in.shape[0]
    o_qn, o_kvn, o_gn, o_qf, o_kvf, o_fl = 0, 1024, 2560, 2584, 3608, 4632
    wm = jnp.concatenate([w_in[:, o_kvn:o_kvn + 1536], w_in[:, o_kvf:o_kvf + 1024],
                          w_in[:, o_qn:o_qn + 1024], w_in[:, o_qf:o_qf + 1024]], axis=1).astype(BF16)
    pad = LANES - 3 * H_NSA - H_FOX
    ws = jnp.concatenate([w_in[:, o_gn:o_gn + 3 * H_NSA], w_in[:, o_fl:o_fl + H_FOX],
                          jnp.zeros((d, pad), F32)], axis=1).astype(BF16)
    bs = jnp.concatenate([b_nsa_gate, b_fox_forget, jnp.zeros((pad,), F32)])[None, :]
    w1p = w_cmp1.reshape(2, CMP_STRIDE, 2 * HEAD_DIM, CMP_HID).astype(BF16)
    wr = jnp.concatenate([w_router, jnp.zeros((d, LANES - N_EXPERTS), F32)], axis=1)
    wrh = wr.astype(BF16)
    wrl = (wr - wrh.astype(F32)).astype(BF16)
    br = jnp.concatenate([b_router, jnp.zeros((LANES - N_EXPERTS,), F32)])[None, :]
    half = CMP_STRIDE // 2
    w1f = jnp.concatenate([w1p[:, :half], w1p[:, half:]], axis=-1)
    return dict(wm=wm, ws=ws, bs=bs, w1p=w1p, w1f=w1f, wout=w_out.astype(BF16), wrh=wrh, wrl=wrl, br=br)


def prompt_mixer(x, sc, sh, g_pre, W, cmpw, *, tm, tq, tk):
    b, t, _ = x.shape
    pb, kvcs, win, kvf, small = inproj_call(x, g_pre, sc, sh, W["wm"], W["ws"], W["bs"], tm)
    kvc = compress_prompt_call(kvcs, *cmpw)
    ocmp, selmask = cmpsel_prompt_call(pb, kvc, tq=tq)
    osel = flash_call("sel", pb, tq=tq, tk=tk, extra=(selmask,))
    owin = flash_call("win", pb, tq=tq, tk=tk)
    logf = small[:, :, 3 * H_NSA:3 * H_NSA + H_FOX]
    cum = cumsum_call(jnp.swapaxes(logf, 1, 2))
    c8 = jnp.pad(cum.reshape(b, KV_FOX, HPG_FOX, t), ((0, 0), (0, 0), (0, 8 - HPG_FOX), (0, 0)))
    ofox = flash_call("fox", pb, tq=tq, tk=tk, extra=(c8,))
    return ocmp, osel, owin, ofox, small, kvcs, win, kvf, logf


PAGE = 128
CMP_PAGES = 16
CMP_PITCH = 136
FOX_PAGES = 16
NS_PAD = 384


def _head_rows(qf, g):
    rows = [qf[:, (g * HPG_NSA + h) * HEAD_DIM:(g * HPG_NSA + h + 1) * HEAD_DIM] for h in range(HPG_NSA)]
    return jnp.concatenate(rows + [jnp.zeros((8 - HPG_NSA, HEAD_DIM), F32)], axis=0).astype(BF16)


def _slope_col(g):
    row = lax.broadcasted_iota(I32, (8, 1), 0)
    col = jnp.zeros((8, 1), F32)
    for h in range(HPG_NSA):
        col = jnp.where(row == h, 2.0 ** -(g * HPG_NSA + h + 1), col)
    return col


def _masked_softmax(s, mask):
    s = jnp.where(mask, s, -1e30)
    e = jnp.where(mask, jnp.exp(s - jnp.max(s, axis=1, keepdims=True)), 0.0)
    return e / jnp.maximum(jnp.sum(e, axis=1, keepdims=True), 1e-30)


def _store_heads(o_ref, o, g):
    for h in range(HPG_NSA):
        c = (g * HPG_NSA + h) * HEAD_DIM
        o_ref[0, :, c:c + HEAD_DIM] = o[h:h + 1].astype(o_ref.dtype)


def _compress_sample_kernel(pt_ref, *refs, npg):
    pages = refs[:npg + 1]
    w1_ref, hb_ref, w2_ref, b2_ref, o_ref, x_scr = refs[npg + 1:]
    per_page = PAGE // CMP_STRIDE
    n = (npg + 1) * per_page
    chunk_rows = CMP_STRIDE * 8
    for p in range(npg + 1):
        for c in range(per_page):
            x_scr[pl.ds((p * per_page + c) * CMP_PITCH, chunk_rows), :] = pages[p][0, c * chunk_rows:(c + 1) * chunk_rows, :]
    load = lambda cb, s: x_scr[pl.ds(s * 8 + cb, n, stride=CMP_PITCH), :]
    r = _compress_rows(load, n, w1_ref, hb_ref, w2_ref, b2_ref)
    o_ref[0] = r[:npg * per_page].astype(o_ref.dtype)


def compress_sample_call(page_table, cache3, w1f, hb, w2, b2):
    bsz, n_pages = page_table.shape
    npg = CMP_PAGES
    assert n_pages % npg == 0
    page_specs = [pl.BlockSpec((1, PAGE * 8, HEAD_DIM),
                               lambda bb, gi, pt, p=p: (pt[bb, jnp.minimum(gi * npg + p, n_pages - 1)], 0, 0))
                  for p in range(npg + 1)]
    rows_out = npg * PAGE // CMP_STRIDE
    grid_spec = pltpu.PrefetchScalarGridSpec(
        num_scalar_prefetch=1,
        grid=(bsz, n_pages // npg),
        in_specs=page_specs + _cmp_weight_specs(),
        out_specs=pl.BlockSpec((1, rows_out, 512), lambda bb, gi, pt: (bb, gi, 0)),
        scratch_shapes=[pltpu.VMEM(((npg + 1) * (PAGE // CMP_STRIDE) * CMP_PITCH, HEAD_DIM), F32)])
    return pl.pallas_call(
        functools.partial(_compress_sample_kernel, npg=npg),
        grid_spec=grid_spec,
        out_shape=jax.ShapeDtypeStruct((bsz, n_pages * PAGE // CMP_STRIDE, 512), BF16),
        compiler_params=_cparams(("arbitrary", "arbitrary")),
        name="compress_sample",
    )(page_table, *([cache3] * (npg + 1)), w1f, hb, w2, b2)


def _cmpsel_sample_kernel(q_ref, kv_ref, o_ref, idx_ref, *, past, n_cmp, n_keep):
    nc = kv_ref.shape[1]
    qf = q_ref[0].astype(F32)
    cidx = lax.broadcasted_iota(I32, (1, nc), 1)
    mask = jnp.where(cidx < n_cmp, cidx * CMP_STRIDE + CMP_LEN - 1, 2 ** 30) <= past
    center = (cidx * CMP_STRIDE - past).astype(F32) + 0.5 * (CMP_LEN - 1)
    overlap = _overlap_matrix(nc, NS_PAD)
    blk = lax.broadcasted_iota(I32, (8, NS_PAD), 1)
    lag = past // SEL_BLOCK - blk
    ii =lax.broadcasted_iota(I32, (NS_PAD, NS_PAD), 0)
    jj = lax.broadcasted_iota(I32, (NS_PAD, NS_PAD), 1)
    slot = lax.broadcasted_iota(I32, (NS_PAD, LANES), 1).astype(F32)
    rows = []
    for g in range(G_NSA):
        k = kv_ref[0, :, g * HEAD_DIM:(g + 1) * HEAD_DIM]
        v = kv_ref[0, :, (G_NSA + g) * HEAD_DIM:(G_NSA + g + 1) * HEAD_DIM]
        s = _nt_dot(_head_rows(qf, g), k) * LN2 + _slope_col(g) * center
        p = _masked_softmax(s, jnp.broadcast_to(mask, s.shape))
        _store_heads(o_ref, jnp.dot(p.astype(BF16), v, preferred_element_type=F32), g)
        imp = jnp.sum(p[0:HPG_NSA], axis=0, keepdims=True)
        score = _split_dot(jnp.broadcast_to(imp, (8, nc)), overlap)
        score = _force_scores(score, blk, lag)
        row = score[0:1, :]
        col = score.T[:, 0:1]
        ge = jnp.where(col >= row, 1.0, 0.0)
        gt = jnp.where(col > row, 1.0, 0.0)
        rank_row = jnp.sum(jnp.where(ii < jj, ge, gt), axis=0, keepdims=True)
        rank_col = jnp.sum(jnp.where(jj < ii, 1.0 - gt, 1.0 - ge), axis=1, keepdims=True)
        sel_row = jnp.where(rank_row < n_keep, 1.0, 0.0)
        sel_col = jnp.where(rank_col < n_keep, 1.0, 0.0)
        before = jnp.sum(jnp.where(jj < ii, sel_row, 0.0), axis=1, keepdims=True)
        pick = jnp.where(before == slot, sel_col * ii[:, 0:1].astype(F32), 0.0)
        rows.append(jnp.sum(pick, axis=0, keepdims=True))
    idx_ref[0] = jnp.concatenate(rows + [jnp.zeros((8 - G_NSA, LANES), F32)], axis=0).astype(I32)


def cmpsel_sample_call(qn, kvc, *, past):
    bsz = qn.shape[0]
    nc = kvc.shape[1]
    n_cmp = (past + 1) // CMP_STRIDE - 1
    n_sel = -(-(past + 1) // SEL_BLOCK)
    assert n_sel <= NS_PAD and n_cmp <= nc
    n_keep = min(N_SEL, n_sel)
    return pl.pallas_call(
        functools.partial(_cmpsel_sample_kernel, past=past, n_cmp=n_cmp, n_keep=n_keep),
        grid=(bsz,),
        in_specs=[pl.BlockSpec((1, 1, W_NSA), lambda bb: (bb, 0, 0)),
                  pl.BlockSpec((1, nc, 512), lambda bb: (bb, 0, 0))],
        out_specs=[pl.BlockSpec((1, 1, W_NSA), lambda bb: (bb, 0, 0)),
                   pl.BlockSpec((1, 8, LANES), lambda bb: (bb, 0, 0))],
        out_shape=[jax.ShapeDtypeStruct((bsz, 1, W_NSA), BF16), jax.ShapeDtypeStruct((bsz, 8, LANES), I32)],
        compiler_params=_cparams(("arbitrary",)),
        name="cmpsel_sample",
    )(qn, kvc)


def _sel_sample_kernel(idx_ref, pt_ref, q_ref, new_ref, cache_ref, o_ref, kvbuf, sem, *, past, n_pages, n_keep):
    bb = pl.program_id(0)
    per_page = PAGE // SEL_BLOCK
    blk_rows = SEL_BLOCK * 8
    copies = []
    for g in range(G_NSA):
        for kx in range(n_keep):
            j = idx_ref[(bb * G_NSA + g) * n_keep + kx]
            page = pt_ref[bb, jnp.minimum(j // per_page, n_pages - 1)]
            r0 = pl.multiple_of((j % per_page) * blk_rows, blk_rows)
            cp = pltpu.make_async_copy(cache_ref.at[page, pl.ds(r0, blk_rows), :], kvbuf.at[g, kx], sem.at[0])
            cp.start()
            copies.append(cp)
    for cp in copies:
        cp.wait()
    qf = q_ref[0].astype(F32)
    nkeys = n_keep * SEL_BLOCK
    lane = lax.broadcasted_iota(I32, (1, nkeys), 1)
    row0 = lax.broadcasted_iota(I32, (SEL_BLOCK, HEAD_DIM), 0) == 0
    for g in range(G_NSA):
        kpos = lane % SEL_BLOCK
        ks, vs = [], []
        for kx in range(n_keep):
            j = idx_ref[(bb * G_NSA + g) * n_keep + kx]
            is_new = j * SEL_BLOCK >= past
            fresh = jnp.logical_and(is_new, row0)
            ks.append(jnp.where(fresh, new_ref[0, 4 + g:5 + g, :], kvbuf[g, kx, pl.ds(4 + g, SEL_BLOCK, stride=8), :]))
            vs.append(jnp.where(fresh, new_ref[0, 6 + g:7 + g, :], kvbuf[g, kx, pl.ds(6 + g, SEL_BLOCK, stride=8), :]))
            kpos = kpos + jnp.where(lane // SEL_BLOCK == kx, j * SEL_BLOCK, 0)
        k = jnp.concatenate(ks, axis=0).astype(BF16)
        v = jnp.concatenate(vs, axis=0).astype(BF16)
        s = _nt_dot(_head_rows(qf, g), k) * LN2 + _slope_col(g) * (kpos - past).astype(F32)
        p = _masked_softmax(s, jnp.broadcast_to(kpos <= past, s.shape))
        _store_heads(o_ref, jnp.dot(p.astype(BF16), v, preferred_element_type=F32), g)


def sel_sample_call(idx_flat, page_table, qn, kvcs_new, cache3, *, past, n_keep):
    bsz, n_pages = page_table.shape
    assert past % SEL_BLOCK == 0 and past == n_pages * PAGE
    grid_spec = pltpu.PrefetchScalarGridSpec(
        num_scalar_prefetch=2,
        grid=(bsz,),
        in_specs=[pl.BlockSpec((1, 1, W_NSA), lambda bb, ix, pt: (bb, 0, 0)),
                  pl.BlockSpec((1, 8, HEAD_DIM), lambda bb, ix, pt: (bb, 0, 0)),
                  pl.BlockSpec(memory_space=pl.ANY)],
        out_specs=pl.BlockSpec((1, 1, W_NSA), lambda bb, ix, pt: (bb, 0, 0)),
        scratch_shapes=[pltpu.VMEM((G_NSA, n_keep, SEL_BLOCK * 8, HEAD_DIM), F32),
                        pltpu.SemaphoreType.DMA((1,))])
    return pl.pallas_call(
        functools.partial(_sel_sample_kernel, past=past, n_pages=n_pages, n_keep=n_keep),
        grid_spec=grid_spec,
        out_shape=jax.ShapeDtypeStruct((bsz, 1, W_NSA), BF16),
        compiler_params=_cparams(("arbitrary",)),
        name="sel_sample",
    )(idx_flat, page_table, qn, kvcs_new, cache3)


def _win_sample_kernel(q_ref, buf_ref, new_ref, o_ref, nb_ref):
    rows = buf_ref.shape[1]
    wb = rows // 4
    nb_ref[0] = pltpu.roll(buf_ref[0], rows - 4, axis=0)
    nb_ref[0, rows - 4:rows, :] = new_ref[0]
    qf = q_ref[0].astype(F32)
    krel = (lax.broadcasted_iota(I32, (1, wb), 1) - (wb - 1)).astype(F32)
    for g in range(G_NSA):
        k = nb_ref[0, pl.ds(g, wb, stride=4), :].astype(BF16)
        v = nb_ref[0, pl.ds(G_NSA + g, wb, stride=4), :].astype(BF16)
        s = _nt_dot(_head_rows(qf, g), k) * LN2 + _slope_col(g) * krel
        p = _masked_softmax(s, jnp.full(s.shape, True))
        _store_heads(o_ref, jnp.dot(p.astype(BF16), v, preferred_element_type=F32), g)


def win_sample_call(qn, win_buf, win_new):
    bsz, rows, _ = win_buf.shape
    assert rows == WINDOW * 4
    return pl.pallas_call(
        _win_sample_kernel,
        grid=(bsz,),
        in_specs=[pl.BlockSpec((1, 1, W_NSA), lambda bb: (bb, 0, 0)),
                  pl.BlockSpec((1, rows, HEAD_DIM), lambda bb: (bb, 0, 0)),
                  pl.BlockSpec((1, 4, HEAD_DIM), lambda bb: (bb, 0, 0))],
        out_specs=[pl.BlockSpec((1, 1, W_NSA), lambda bb: (bb, 0, 0)),
                   pl.BlockSpec((1, rows, HEAD_DIM), lambda bb: (bb, 0, 0))],
        out_shape=[jax.ShapeDtypeStruct((bsz, 1, W_NSA), BF16), jax.ShapeDtypeStruct((bsz, rows, HEAD_DIM), F32)],
        compiler_params=_cparams(("arbitrary",)),
        name="win_sample",
    )(qn, win_buf, win_new)


def _fox_sample_kernel(pt_ref, *refs, npg, n_steps):
    kv_pages = refs[:npg]
    lf_pages = refs[npg:2 * npg]
    q_ref, new_ref, lfn_ref, o_ref, qbd_scr, m_scr, l_scr, acc_scr, carry_scr = refs[2 * npg:]
    st = pl.program_id(1)
    kvw = KV_FOX * HEAD_DIM

    @pl.when(st == 0)
    def _():
        qf = q_ref[0].astype(F32)
        zero = jnp.zeros((1, HEAD_DIM), F32)
        rows = [jnp.concatenate([qf[:, h * HEAD_DIM:(h + 1) * HEAD_DIM] if c == h // HPG_FOX else zero
                                 for c in range(KV_FOX)], axis=1) for h in range(H_FOX)]
        qbd_scr[...] = jnp.concatenate(rows, axis=0)
        m_scr[...] = jnp.full(m_scr.shape, M_FLOOR, F32)
        l_scr[...] = jnp.zeros(l_scr.shape, F32)
        acc_scr[...] = jnp.zeros(acc_scr.shape, F32)
        carry_scr[...] = lfn_ref[0]

    qbd = qbd_scr[...].astype(BF16)
    tri = jnp.where(lax.broadcasted_iota(I32, (PAGE, PAGE), 0) <= lax.broadcasted_iota(I32, (PAGE, PAGE), 1),
                    1.0, 0.0).astype(BF16)
    carry = carry_scr[...]
    parts = []
    def heads(ref, base):
        n = ref.shape[1] // 8
        return jnp.concatenate([ref[0, pl.ds(base + h, n, stride=8), :] for h in range(KV_FOX)], axis=1)

    for p in range(npg):
        k = heads(kv_pages[p], 0).astype(BF16)
        incl = _split_dot(lf_pages[p][0], tri)
        tot = incl[:, PAGE - 1:PAGE]
        parts.append(_nt_dot(qbd, k) * LN2 + (carry + (tot - incl)))
        carry = carry + tot
    carry_scr[...] = carry
    s = jnp.concatenate(parts, axis=1)
    m_prev = m_scr[...]
    m_new = jnp.maximum(m_prev, jnp.max(s, axis=1, keepdims=True))
    alpha = jnp.exp(m_prev - m_new)
    pr = jnp.exp(s - m_new)
    l_scr[...] = alpha * l_scr[...] + jnp.sum(pr, axis=1, keepdims=True)
    acc = alpha * acc_scr[...]
    for p in range(npg):
        v = heads(kv_pages[p], KV_FOX).astype(BF16)
        acc = acc + jnp.dot(pr[:, p * PAGE:(p + 1) * PAGE].astype(BF16), v, preferred_element_type=F32)
    acc_scr[...] = acc
    m_scr[...] = m_new

    @pl.when(st == n_steps - 1)
    def _():
        kn = heads(new_ref, 0)
        vn = heads(new_ref, KV_FOX)
        s_new = jnp.sum(qbd_scr[...] * kn, axis=1, keepdims=True) * LN2
        m_fin = jnp.maximum(m_scr[...], s_new)
        a2 = jnp.exp(m_scr[...] - m_fin)
        p_new = jnp.exp(s_new - m_fin)
        o = (a2 * acc_scr[...] + p_new * vn) / jnp.maximum(a2 * l_scr[...] + p_new, 1e-30)
        for h in range(H_FOX):
            c = (h // HPG_FOX) * HEAD_DIM
            o_ref[0, :, h * HEAD_DIM:(h + 1) * HEAD_DIM] = o[h:h + 1, c:c + HEAD_DIM].astype(o_ref.dtype)


def fox_sample_call(page_table, qf, kvf_new, lf_new, cache3, logf_t):
    bsz, n_pages = page_table.shape
    npg = FOX_PAGES
    assert n_pages % npg == 0
    n_steps = n_pages // npg
    page_of = lambda bb, st, pt, p: pt[bb, n_pages - 1 - (st * npg + p)]
    kv_specs = [pl.BlockSpec((1, PAGE * 8, HEAD_DIM), lambda bb, st, pt, p=p: (page_of(bb, st, pt, p), 0, 0))
                for p in range(npg)]
    lf_specs = [pl.BlockSpec((1, H_FOX, PAGE), lambda bb, st, pt, p=p: (page_of(bb, st, pt, p), 0, 0))
                for p in range(npg)]
    grid_spec = pltpu.PrefetchScalarGridSpec(
        num_scalar_prefetch=1,
        grid=(bsz, n_steps),
        in_specs=kv_specs + lf_specs + [pl.BlockSpec((1, 1, W_FOX), lambda bb, st, pt: (bb, 0, 0)),
                                        pl.BlockSpec((1, 8, HEAD_DIM), lambda bb, st, pt: (bb, 0, 0)),
                                        pl.BlockSpec((1, H_FOX, 1), lambda bb, st, pt: (bb, 0, 0))],
        out_specs=pl.BlockSpec((1, 1, W_FOX), lambda bb, st, pt: (bb, 0, 0)),
        scratch_shapes=[pltpu.VMEM((H_FOX, KV_FOX * HEAD_DIM), F32), pltpu.VMEM((H_FOX, 1), F32),
                        pltpu.VMEM((H_FOX, 1), F32), pltpu.VMEM((H_FOX, KV_FOX * HEAD_DIM), F32),
                        pltpu.VMEM((H_FOX, 1), F32)])
    return pl.pallas_call(
        functools.partial(_fox_sample_kernel, npg=npg, n_steps=n_steps),
        grid_spec=grid_spec,
        out_shape=jax.ShapeDtypeStruct((bsz, 1, W_FOX), BF16),
        compiler_params=_cparams(("arbitrary", "arbitrary")),
        name="fox_sample",
    )(page_table, *([cache3] * npg), *([logf_t] * npg), qf, kvf_new, lf_new)


def sample_mixer(x, sc, sh, g_pre, W, cmpw, cache_nsa3, win_buf, cache_fox3, logf_t, page_table):
    bsz = x.shape[1]
    n_pages = page_table.shape[1]
    past = n_pages * PAGE
    pb, kvcs, win, kvf, small = inproj_call(x, g_pre, sc, sh, W["wm"], W["ws"], W["bs"], bsz)
    per_seq = lambda a: a.reshape(bsz, -1, a.shape[-1])
    pb, kvcs, win, kvf, small = map(per_seq, (pb, kvcs, win, kvf, small))
    qn = pb[:, :, COL_QN:COL_QN + W_NSA]
    qf = pb[:, :, COL_QF:COL_QF + W_FOX]
    kvc = compress_sample_call(page_table, cache_nsa3, *cmpw)
    ocmp, idx = cmpsel_sample_call(qn, kvc, past=past)
    n_keep = min(N_SEL, -(-(past + 1) // SEL_BLOCK))
    idx_flat = idx[:, :G_NSA, :n_keep].reshape(-1)
    osel = sel_sample_call(idx_flat, page_table, qn, kvcs, cache_nsa3, past=past, n_keep=n_keep)
    owin, win_new = win_sample_call(qn, win_buf, win)
    logf = small[:, :, 3 * H_NSA:3 * H_NSA + H_FOX]
    ofox = fox_sample_call(page_table, qf, kvf, logf.reshape(bsz, H_FOX, 1), cache_fox3, logf_t)
    to_rows = lambda a: a.reshape(1, bsz, a.shape[-1])
    return tuple(map(to_rows, (ocmp, osel, owin, ofox, small))) + (kvcs, win_new, kvf, logf)


PROMPT_TM = 512
ATTN_TQ = 512
ATTN_TK = 512


def kernel(x_prompt, x_sample, c_prompt, c_sample, cache_nsa_kv, state_nsa_win, cache_fox_kv, cache_fox_logf,
           page_table, w_ada, b_ada, g_pre_mix, g_post_mix, g_pre_ffn, g_post_ffn, w_in, b_nsa_gate, b_fox_forget,
           w_cmp1, b_cmp1, w_cmp2, b_cmp2, pe_cmp, g_grp, w_out, w_router, b_router, w_up, b_up, w_down, b_down):
    assert w_ada.shape[0] == 1 and x_sample.shape[1] == 1
    bp, t, d = x_prompt.shape
    bs = x_sample.shape[0]
    n_phys = cache_nsa_kv.shape[1]
    W = prep_weights(w_in[0], b_nsa_gate[0], b_fox_forget[0], w_cmp1[0], w_out[0], w_router[0], b_router[0])
    hb = cmp_bias_call(pe_cmp[0].reshape(2, CMP_LEN // 2, 2 * HEAD_DIM), W["w1p"], b_cmp1[0])
    cmpw = (W["w1f"], hb, w_cmp2[0].astype(BF16), b_cmp2[0])
    row = lambda a: a[0][None, :]

    ada = ada_call(jnp.concatenate([c_prompt, c_sample], axis=0), w_ada[0], row(b_ada))
    mods_p = [m[:, None, :] for m in jnp.split(ada[:bp], 6, axis=-1)]
    mods_s = [m[None] for m in jnp.split(ada[bp:], 6, axis=-1)]
    xs = x_sample.reshape(1, bs, d)

    ocmp, osel, owin, ofox, small, kvcs_p, win_p, kvf_p, logf_p = prompt_mixer(
        x_prompt, mods_p[1], mods_p[0], row(g_pre_mix), W, cmpw,
        tm=PROMPT_TM, tq=ATTN_TQ, tk=ATTN_TK)
    merge_args = (row(g_grp), W["wout"], row(g_post_mix))
    ffn_args = (W["wrh"], W["wrl"], W["br"])
    x1_p, h2_p, route_p, cnt_p = merge_call(ocmp, osel, owin, ofox, small, x_prompt, *merge_args, mods_p[2],
                                            row(g_pre_ffn), mods_p[4], mods_p[3], *ffn_args,
                                            jnp.zeros((8, LANES), F32), 256)

    cache_nsa3 = cache_nsa_kv.reshape(n_phys, PAGE * 8, HEAD_DIM)
    cache_fox3 = cache_fox_kv.reshape(n_phys, PAGE * 8, HEAD_DIM)
    logf_t = jnp.swapaxes(cache_fox_logf[0], 1, 2)
    ocmp_s, osel_s, owin_s, ofox_s, small_s, kvcs_s, win_s, kvf_s, logf_s = sample_mixer(
        xs, mods_s[1], mods_s[0], row(g_pre_mix), W, cmpw,
        cache_nsa3, state_nsa_win.reshape(bs, WINDOW * 4, HEAD_DIM), cache_fox3, logf_t, page_table)
    x1_s, h2_s, route_s, cnt_all = merge_call(ocmp_s, osel_s, owin_s, ofox_s, small_s, xs, *merge_args, mods_s[2],
                                              row(g_pre_ffn), mods_s[4], mods_s[3], *ffn_args, cnt_p, bs)

    n_p = bp * t
    h2_all = jnp.concatenate([h2_p.reshape(n_p, d), h2_s.reshape(bs, d)], axis=0)
    route_all = jnp.concatenate([route_p.reshape(n_p, LANES), route_s.reshape(bs, LANES)], axis=0)[:, :3 * TOP_K]
    slot_tok, slot_w, dest, blk_e, n_used = moe_dispatch(route_all, cnt_all[0, :N_EXPERTS].astype(I32))
    n_blk = blk_e.shape[0]
    bounds = [(c * n_blk // MOE_CHUNKS, (c + 1) * n_blk // MOE_CHUNKS) for c in range(MOE_CHUNKS)]
    a_chunks = [moe_up_call(blk_e, n_used, h2_all[slot_tok[b0 * MOE_TM:b1 * MOE_TM]], w_up[0], b_up[0][:, None, :], b0)
                for b0, b1 in bounds]
    ncol = d // MOE_TN
    ys = [moe_down_call(blk_e, n_used, a_chunks, bounds, w_down[0], b_down[0][:, None, :], slot_w, c, 1)
          for c in range(ncol)]
    yg_p = [y[dest[:, :n_p]].reshape(TOP_K, bp, t, MOE_TN) for y in ys]
    yg_s = [y[dest[:, n_p:]].reshape(TOP_K, 1, bs, MOE_TN) for y in ys]
    y_p = final_call(yg_p, x1_p, mods_p[5], row(g_post_ffn), PROMPT_TM)
    y_s = final_call(yg_s, x1_s, mods_s[5], row(g_post_ffn), bs)

    wlen = min(WINDOW, t)
    return (y_p, y_s.reshape(bs, 1, d),
            kvcs_p.reshape(1, bp, t, 2, 2, G_NSA, HEAD_DIM),
            win_p[:, (t - wlen) * 4:].reshape(1, bp, wlen, 2, G_NSA, HEAD_DIM),
            kvf_p.reshape(1, bp, t, 2, KV_FOX, HEAD_DIM),
            logf_p[None],
            kvcs_s.reshape(1, bs, 1, 2, 2, G_NSA, HEAD_DIM),
            win_s.reshape(1, bs, WINDOW, 2, G_NSA, HEAD_DIM),
            kvf_s.reshape(1, bs, 1, 2, KV_FOX, HEAD_DIM),
            logf_s.reshape(1, bs, 1, H_FOX))
```

```python
import functools

import jax
import jax.numpy as jnp
from jax import lax
from jax.experimental import pallas as pl
from jax.experimental.pallas import tpu as pltpu

F32 = jnp.float32
BF16 = jnp.bfloat16
I32 = jnp.int32

D_MODEL = 2048
HEAD_DIM = 128
H_NSA = 8
H_FOX = 8
G_NSA = 2
HPG_NSA = 4
KV_FOX = 4
HPG_FOX = 2
W_NSA = H_NSA * HEAD_DIM
W_FOX = H_FOX * HEAD_DIM
CMP_LEN = 32
CMP_STRIDE = 16
CMP_HID = 256
SEL_BLOCK = 64
N_SEL = 16
N_LOCAL_SEL = 2
WINDOW = 512
N_EXPERTS = 32
TOP_K = 4
D_FF = 2048
SWIGLU_ALPHA = 1.702
SWIGLU_LIMIT = 7.0
RMS_EPS = 1e-6
ATTN_SCALE = HEAD_DIM ** -0.5
LOG2E = 1.4426950408889634
LN2 = 0.6931471805599453
Q_PRESCALE = ATTN_SCALE * LOG2E
FORCE_SCORE = 1e9
MASKED = -2e30
M_FLOOR = -1e30

LANES = 128
VMEM_LIMIT = 52 * 1024 * 1024

N_MAIN = 4608
PROJ_TN = 1536
COL_KVCS, COL_WIN, COL_KVF, COL_QN, COL_QF = 0, 1024, 1536, 2560, 3584


def _cparams(sem, vmem=VMEM_LIMIT):
    return pltpu.CompilerParams(dimension_semantics=sem, vmem_limit_bytes=vmem)


def _rms(x, g):
    return x * lax.rsqrt(jnp.mean(x * x, axis=-1, keepdims=True) + RMS_EPS) * g


def _nt_dot(a, b):
    return lax.dot_general(a, b, (((1,), (1,)), ((), ())), preferred_element_type=F32)


def _ada_kernel(c_ref, w_ref, b_ref, o_ref):
    c = c_ref[...]
    a = (c * jax.nn.sigmoid(c)).astype(BF16)
    o_ref[...] = jnp.dot(a, w_ref[...].astype(BF16), preferred_element_type=F32) + b_ref[...]


def ada_call(c, w, b):
    bc, d = c.shape
    n = w.shape[1]
    tn = 1024
    return pl.pallas_call(
        _ada_kernel,
        grid=(n // tn,),
        in_specs=[pl.BlockSpec((bc, d), lambda j: (0, 0)),
                  pl.BlockSpec((d, tn), lambda j: (0, j)),
                  pl.BlockSpec((1, tn), lambda j: (0, j))],
        out_specs=pl.BlockSpec((bc, tn), lambda j: (0, j)),
        out_shape=jax.ShapeDtypeStruct((bc, n), F32),
        compiler_params=_cparams(("arbitrary",)),
        name="ada",
    )(c, w, b)


def _inproj_kernel(x_ref, g_ref, sc_ref, sh_ref, wm_ref, ws_ref, bs_ref,
                   pb_ref, kvcs_ref, win_ref, kvf_ref, small_ref, h_scr):
    j = pl.program_id(2)

    @pl.when(j == 0)
    def _():
        h = _rms(x_ref[0], g_ref[...]) * (1.0 + sc_ref[0]) + sh_ref[0]
        hb = h.astype(BF16)
        h_scr[...] = hb
        z = jnp.dot(hb, ws_ref[...], preferred_element_type=F32) + bs_ref[...]
        lane = lax.broadcasted_iota(I32, z.shape, 1)
        small_ref[0] = jnp.where(lane < 3 * H_NSA, jax.nn.sigmoid(z), jax.nn.log_sigmoid(z))

    r = jnp.dot(h_scr[...], wm_ref[...], preferred_element_type=F32)
    col = j * PROJ_TN + lax.broadcasted_iota(I32, (1, PROJ_TN), 1)
    pb_ref[0] = (r * jnp.where(col >= COL_QN, Q_PRESCALE, 1.0)).astype(BF16)
    tm = r.shape[0]
    per_tile = PROJ_TN // HEAD_DIM

    def scatter_rows(step):
        for c in range(per_tile):
            blk = step * per_tile + c
            piece = r[:, c * HEAD_DIM:(c + 1) * HEAD_DIM]
            if blk < COL_WIN // HEAD_DIM:
                kvcs_ref[0, pl.ds(blk, tm, stride=8), :] = piece
            elif blk < COL_KVF // HEAD_DIM:
                win_ref[0, pl.ds(blk - COL_WIN // HEAD_DIM, tm, stride=4), :] = piece
            elif blk < COL_QN // HEAD_DIM:
                kvf_ref[0, pl.ds(blk - COL_KVF // HEAD_DIM, tm, stride=8), :] = piece

    for step in range(-(-COL_QN // PROJ_TN)):
        pl.when(j == step)(functools.partial(scatter_rows, step))


def inproj_call(x, g, sc, sh, wm, ws, bs, tm):
    b, t, d = x.shape
    per_row = sc.shape[1] != 1
    nj = N_MAIN // PROJ_TN
    mod_spec = (pl.BlockSpec((1, tm, d), lambda bb, i, j: (bb, i, 0)) if per_row
                else pl.BlockSpec((1, 1, d), lambda bb, i, j: (bb, 0, 0)))
    return pl.pallas_call(
        _inproj_kernel,
        grid=(b, t // tm, nj),
        in_specs=[pl.BlockSpec((1, tm, d), lambda bb, i, j: (bb, i, 0)),
                  pl.BlockSpec((1, d), lambda bb, i, j: (0, 0)),
                  mod_spec, mod_spec,
                  pl.BlockSpec((d, PROJ_TN), lambda bb, i, j: (0, j)),
                  pl.BlockSpec((d, LANES), lambda bb, i, j: (0, 0)),
                  pl.BlockSpec((1, LANES), lambda bb, i, j: (0, 0))],
        out_specs=[pl.BlockSpec((1, tm, PROJ_TN), lambda bb, i, j: (bb, i, j)),
                   pl.BlockSpec((1, tm * 8, HEAD_DIM), lambda bb, i, j: (bb, i, 0)),
                   pl.BlockSpec((1, tm * 4, HEAD_DIM), lambda bb, i, j: (bb, i, 0)),
                   pl.BlockSpec((1, tm * 8, HEAD_DIM), lambda bb, i, j: (bb, i, 0)),
                   pl.BlockSpec((1, tm, LANES), lambda bb, i, j: (bb, i, 0))],
        out_shape=[jax.ShapeDtypeStruct((b, t, N_MAIN), BF16),
                   jax.ShapeDtypeStruct((b, t * 8, HEAD_DIM), F32),
                   jax.ShapeDtypeStruct((b, t * 4, HEAD_DIM), F32),
                   jax.ShapeDtypeStruct((b, t * 8, HEAD_DIM), F32),
                   jax.ShapeDtypeStruct((b, t, LANES), F32)],
        scratch_shapes=[pltpu.VMEM((tm, d), BF16)],
        compiler_params=_cparams(("arbitrary", "arbitrary", "arbitrary")),
        name="inproj",
    )(x, g, sc, sh, wm, ws, bs)


def _lane_cumsum(x):
    lane = lax.broadcasted_iota(I32, x.shape, 1)
    d = 1
    while d < LANES:
        x = x + jnp.where(lane >= d, pltpu.roll(x, d, axis=1), 0.0)
        d *= 2
    return x


def _cumsum_kernel(x_ref, o_ref):
    t = x_ref.shape[2]
    carry = jnp.zeros((x_ref.shape[1], 1), F32)
    for c in range(t // LANES):
        sl = slice(c * LANES, (c + 1) * LANES)
        y = _lane_cumsum(x_ref[0, :, sl]) + carry
        o_ref[0, :, sl] = y
        carry = y[:, LANES - 1:LANES]


def cumsum_call(x):
    b, h, t = x.shape
    return pl.pallas_call(
        _cumsum_kernel,
        grid=(b,),
        in_specs=[pl.BlockSpec((1, h, t), lambda bb: (bb, 0, 0))],
        out_specs=pl.BlockSpec((1, h, t), lambda bb: (bb, 0, 0)),
        out_shape=jax.ShapeDtypeStruct((b, h, t), F32),
        compiler_params=_cparams(("arbitrary",)),
        name="logf_cumsum",
    )(x)


def _nsa_slope(g, h):
    return jnp.where(g == 0, 2.0 ** -(h + 1), 2.0 ** -(HPG_NSA + h + 1)).astype(F32)


def _flash_schedule(mode, t, tq, tk):
    qi, ki, fl = [], [], []
    for i in range(t // tq):
        lo = max(0, (i * tq - WINDOW + 1) // tk) if mode == "win" else 0
        hi = (i * tq + tq - 1) // tk
        for kb in range(lo, hi + 1):
            below_diag = kb * tk + tk - 1 <= i * tq
            in_window = mode != "win" or (i * tq + tq - 1) - kb * tk < WINDOW
            qi.append(i)
            ki.append(kb)
            fl.append((1 if kb == lo else 0) | (2 if kb == hi else 0) | (0 if below_diag and in_window else 4))
    return tuple(jnp.asarray(a, I32) for a in (qi, ki, fl))


def _flash_kernel(qi_ref, ki_ref, fl_ref, *refs, mode, hpg, tq, tk):
    if mode == "fox":
        q_ref, k_ref, v_ref, cq_ref, ck_ref, o_ref, m_scr, acc_scr = refs
    elif mode == "sel":
        q_ref, k_ref, v_ref, sm_ref, o_ref, m_scr, acc_scr = refs
    else:
        q_ref, k_ref, v_ref, o_ref, m_scr, acc_scr = refs
    g = pl.program_id(1)
    p_idx = pl.program_id(2)
    q0 = qi_ref[p_idx] * tq
    k0 = ki_ref[p_idx] * tk
    flags = fl_ref[p_idx]

    @pl.when((flags & 1) != 0)
    def _():
        m_scr[...] = jnp.full(m_scr.shape, M_FLOOR, F32)
        acc_scr[...] = jnp.zeros(acc_scr.shape, F32)

    def step(positional):
        k = k_ref[0]
        v = v_ref[0]
        mask = None
        if positional:
            dist = (q0 + lax.broadcasted_iota(I32, (tq, tk), 0)) - (k0 + lax.broadcasted_iota(I32, (tq, tk), 1))
        if mode == "win":
            mask = (lax.bitcast_convert_type(dist, jnp.uint32) < WINDOW) if positional else None
        elif mode == "fox":
            mask = (dist >= 0) if positional else None
        else:
            blk = (k0 + lax.broadcasted_iota(I32, (sm_ref.shape[3], tk), 1)) // SEL_BLOCK
            expand = jnp.where(blk == lax.broadcasted_iota(I32, blk.shape, 0), 1.0, 0.0).astype(BF16)
            chosen = jnp.dot(sm_ref[0, 0], expand, preferred_element_type=F32)
            mask = (jnp.where(dist >= 0, chosen, 0.0) if positional else chosen) > 0.5
        krel = (k0 - q0 + lax.broadcasted_iota(I32, (1, tk), 1)).astype(F32)
        v_ones = jnp.concatenate([v, jnp.ones((tk, HEAD_DIM), BF16)], axis=1)
        for h in range(hpg):
            q = q_ref[0, :, h * HEAD_DIM:(h + 1) * HEAD_DIM]
            if mode == "fox":
                bias = (cq_ref[0, 0, h:h + 1, 0:1] - ck_ref[0, 0, h:h + 1, :]) * LOG2E
            else:
                bias = (_nsa_slope(g, h) * LOG2E) * krel
            s = _nt_dot(q, k) + bias
            if mask is not None:
                s = jnp.where(mask, s, MASKED)
            m_prev = m_scr[h]
            m_new = jnp.maximum(m_prev, jnp.max(s, axis=1, keepdims=True))
            alpha = jnp.exp2(m_prev - m_new)
            p = jnp.exp2(s - jnp.concatenate([m_new] * (tk // LANES), axis=1))
            pv = jnp.dot(p.astype(BF16), v_ones, preferred_element_type=F32)
            acc_scr[h] = jnp.concatenate([alpha, alpha], axis=1) * acc_scr[h] + pv
            m_scr[h] = m_new

    pl.when((flags & 4) != 0)(functools.partial(step, True))
    pl.when((flags & 4) == 0)(functools.partial(step, False))

    @pl.when((flags & 2) != 0)
    def _():
        for h in range(hpg):
            o = acc_scr[h, :, :HEAD_DIM] / jnp.maximum(acc_scr[h, :, HEAD_DIM:], 1e-30)
            o_ref[0, :, h * HEAD_DIM:(h + 1) * HEAD_DIM] = o.astype(o_ref.dtype)


def flash_call(mode, pb, *, tq, tk, extra=()):
    b, t, _ = pb.shape
    if mode == "fox":
        hpg, ngrp = HPG_FOX, KV_FOX
        qcol, kcol, vcol = COL_QF // (hpg * HEAD_DIM), COL_KVF // HEAD_DIM, COL_KVF // HEAD_DIM + KV_FOX
    else:
        hpg, ngrp = HPG_NSA, G_NSA
        base = (COL_KVCS + 512) if mode == "sel" else COL_WIN
        qcol, kcol, vcol = COL_QN // (hpg * HEAD_DIM), base // HEAD_DIM, base // HEAD_DIM + G_NSA
    qi, ki, fl = _flash_schedule(mode, t, tq, tk)
    in_specs = [pl.BlockSpec((1, tq, hpg * HEAD_DIM), lambda bb, g, p, qi, ki, fl: (bb, qi[p], qcol + g)),
                pl.BlockSpec((1, tk, HEAD_DIM), lambda bb, g, p, qi, ki, fl: (bb, ki[p], kcol + g)),
                pl.BlockSpec((1, tk, HEAD_DIM), lambda bb, g, p, qi, ki, fl: (bb, ki[p], vcol + g))]
    args = [pb, pb, pb]
    if mode == "fox":
        c8 = extra[0]
        in_specs += [pl.BlockSpec((1, 1, 8, tq), lambda bb, g, p, qi, ki, fl: (bb, g, 0, qi[p])),
                     pl.BlockSpec((1, 1, 8, tk), lambda bb, g, p, qi, ki, fl: (bb, g, 0, ki[p]))]
        args += [c8, c8]
    elif mode == "sel":
        sm = extra[0]
        in_specs += [pl.BlockSpec((1, 1, tq, sm.shape[3]), lambda bb, g, p, qi, ki, fl: (bb, g, qi[p], 0))]
        args += [sm]
    grid_spec = pltpu.PrefetchScalarGridSpec(
        num_scalar_prefetch=3,
        grid=(b, ngrp, int(qi.shape[0])),
        in_specs=in_specs,
        out_specs=pl.BlockSpec((1, tq, hpg * HEAD_DIM), lambda bb, g, p, qi, ki, fl: (bb, qi[p], g)),
        scratch_shapes=[pltpu.VMEM((hpg, tq, LANES), F32), pltpu.VMEM((hpg, tq, 2 * HEAD_DIM), F32)])
    return pl.pallas_call(
        functools.partial(_flash_kernel, mode=mode, hpg=hpg, tq=tq, tk=tk),
        grid_spec=grid_spec,
        out_shape=jax.ShapeDtypeStruct((b, t, ngrp * hpg * HEAD_DIM), BF16),
        compiler_params=_cparams(("arbitrary",) * 3),
        name="flash_" + mode,
    )(qi, ki, fl, *args)


def _cmp_bias_kernel(pe_ref, w1_ref, b1_ref, o_ref):
    for j in range(2):
        acc = jnp.zeros((8, CMP_HID), F32)
        for p in range(CMP_LEN // 2):
            acc = acc + _split_dot(jnp.broadcast_to(pe_ref[j, p:p + 1, :], (8, 2 * HEAD_DIM)), w1_ref[j, p])
        o_ref[j] = acc + b1_ref[j:j + 1, :]


def cmp_bias_call(pe2, w1p, b1):
    return pl.pallas_call(_cmp_bias_kernel, out_shape=jax.ShapeDtypeStruct((2, 8, CMP_HID), F32),
                          compiler_params=_cparams(None), name="cmp_bias")(pe2, w1p, b1)


def _compress_rows(load, n, w1_ref, hb_ref, w2_ref, b2_ref):
    outs = []
    for j in range(2):
        both = jnp.zeros((2 * n, 2 * CMP_HID), F32)
        for sp in range(CMP_STRIDE // 2):
            parts = [jnp.concatenate([load(j * G_NSA + g, 2 * sp + u) for u in range(2)], axis=1)
                     for g in range(G_NSA)]
            both = both + jnp.dot(jnp.concatenate(parts, axis=0).astype(BF16), w1_ref[j, sp],
                                  preferred_element_type=F32)
        for g in range(G_NSA):
            f = both[g * n:(g + 1) * n, :CMP_HID]
            s = pltpu.roll(both[g * n:(g + 1) * n, CMP_HID:], n - 1, axis=0)
            hid = jax.nn.gelu(f + s + hb_ref[j, 0:1, :])
            outs.append(jnp.dot(hid.astype(BF16), w2_ref[j], preferred_element_type=F32) + b2_ref[j:j + 1, :])
    return jnp.concatenate(outs, axis=1)


def _compress_prompt_kernel(x_ref, w1_ref, hb_ref, w2_ref, b2_ref, o_ref):
    n = o_ref.shape[1]
    load = lambda cb, s: x_ref[0, pl.ds(s * 8 + cb, n, stride=CMP_STRIDE * 8), :]
    o_ref[0] = _compress_rows(load, n, w1_ref, hb_ref, w2_ref, b2_ref).astype(o_ref.dtype)


def _cmp_weight_specs():
    def const(shape):
        return pl.BlockSpec(shape, lambda *a: (0,) * len(shape))
    return [const((2, CMP_STRIDE // 2, 2 * HEAD_DIM, 2 * CMP_HID)), const((2, 8, CMP_HID)),
            const((2, CMP_HID, HEAD_DIM)), const((2, HEAD_DIM))]


def compress_prompt_call(kvcs, w1f, hb, w2, b2):
    b, t8, _ = kvcs.shape
    n = t8 // 8 // CMP_STRIDE
    return pl.pallas_call(
        _compress_prompt_kernel,
        grid=(b,),
        in_specs=[pl.BlockSpec((1, t8, HEAD_DIM), lambda bb: (bb, 0, 0))] + _cmp_weight_specs(),
        out_specs=pl.BlockSpec((1, n, 512), lambda bb: (bb, 0, 0)),
        out_shape=jax.ShapeDtypeStruct((b, n, 512), BF16),
        compiler_params=_cparams(("arbitrary",)),
        name="compress_prompt",
    )(kvcs, w1f, hb, w2, b2)


def _rank_select_cols(score_t, n_keep):
    ns = score_t.shape[0]
    row_id = lax.broadcasted_iota(I32, score_t.shape, 0)
    rank = jnp.zeros(score_t.shape, F32)
    for c in range(ns):
        row = score_t[c:c + 1, :]
        rank = rank + jnp.where(row_id > c, jnp.where(row >= score_t, 1.0, 0.0), jnp.where(row > score_t, 1.0, 0.0))
    return jnp.where(rank < n_keep, 1.0, 0.0)


def _split_dot(a, b_bf16):
    hi = a.astype(BF16)
    r1 = a - hi.astype(F32)
    mid = r1.astype(BF16)
    lo = (r1 - mid.astype(F32)).astype(BF16)
    return (jnp.dot(hi, b_bf16, preferred_element_type=F32) + jnp.dot(mid, b_bf16, preferred_element_type=F32)
            + jnp.dot(lo, b_bf16, preferred_element_type=F32))


def _overlap_matrix(nc, ns):
    ci = lax.broadcasted_iota(I32, (nc, ns), 0) * CMP_STRIDE
    sj = lax.broadcasted_iota(I32, (nc, ns), 1)
    return jnp.where(ci < (sj + 1) * SEL_BLOCK, jnp.where(ci + CMP_LEN > sj * SEL_BLOCK, 1.0, 0.0), 0.0).astype(BF16)


def _force_scores(score, blk, lag):
    recent = lax.bitcast_convert_type(lag, jnp.uint32) < N_LOCAL_SEL
    score = jnp.where(recent, FORCE_SCORE, jnp.where(lag >= 0, score, -FORCE_SCORE))
    return jnp.where(blk == 0, FORCE_SCORE, score)


def _cmpsel_prompt_kernel(q_ref, k_ref, v_ref, o_ref, sm_ref, *, tq, n_cmp, n_sel, n_keep):
    g = pl.program_id(1)
    i = pl.program_id(2)
    nc = k_ref.shape[1]
    k = k_ref[0]
    v = v_ref[0]
    qpos = i * tq + lax.broadcasted_iota(I32, (tq, nc), 0)
    cidx = lax.broadcasted_iota(I32, (tq, nc), 1)
    mask = jnp.where(cidx < n_cmp, cidx * CMP_STRIDE + CMP_LEN - 1, 2 ** 30) <= qpos
    center = (lax.broadcasted_iota(I32, (1, nc), 1) * CMP_STRIDE - i * tq).astype(F32) + 0.5 * (CMP_LEN - 1)
    imp = jnp.zeros((tq, nc), F32)
    for h in range(HPG_NSA):
        q = q_ref[0, :, h * HEAD_DIM:(h + 1) * HEAD_DIM]
        s = _nt_dot(q, k) * LN2 + _nsa_slope(g, h) * center
        s = jnp.where(mask, s, -1e30)
        e = jnp.where(mask, jnp.exp(s - jnp.max(s, axis=1, keepdims=True)), 0.0)
        p = e / jnp.maximum(jnp.sum(e, axis=1, keepdims=True), 1e-30)
        o_ref[0, :, h * HEAD_DIM:(h + 1) * HEAD_DIM] = jnp.dot(
            p.astype(BF16), v, preferred_element_type=F32).astype(o_ref.dtype)
        imp = imp + p
    ns_pad = -(-n_sel // LANES) * LANES
    ns8 = -(-n_sel // 8) * 8
    score_t = _split_dot(imp, _overlap_matrix(nc, ns_pad)).T[:ns8]
    blk = lax.broadcasted_iota(I32, (ns8, tq), 0)
    lag = (i * tq + lax.broadcasted_iota(I32, (ns8, tq), 1)) // SEL_BLOCK - blk
    score_t = jnp.where(blk < n_sel, _force_scores(score_t, blk, lag), -2.0 * FORCE_SCORE)
    chosen_t = jnp.concatenate([_rank_select_cols(score_t, n_keep), jnp.zeros((ns_pad - ns8, tq), F32)], axis=0)
    sm_ref[0, 0] = chosen_t.T[:, :n_sel].astype(sm_ref.dtype)


def cmpsel_prompt_call(pb, kvc, *, tq):
    b, t, _ = pb.shape
    nc = kvc.shape[1]
    n_cmp = nc - 1
    n_sel = -(-t // SEL_BLOCK)
    n_keep = min(N_SEL, n_sel)
    qcol = COL_QN // (HPG_NSA * HEAD_DIM)
    return pl.pallas_call(
        functools.partial(_cmpsel_prompt_kernel, tq=tq, n_cmp=n_cmp, n_sel=n_sel, n_keep=n_keep),
        grid=(b, G_NSA, t // tq),
        in_specs=[pl.BlockSpec((1, tq, HPG_NSA * HEAD_DIM), lambda bb, g, i: (bb, i, qcol + g)),
                  pl.BlockSpec((1, nc, HEAD_DIM), lambda bb, g, i: (bb, 0, g)),
                  pl.BlockSpec((1, nc, HEAD_DIM), lambda bb, g, i: (bb, 0, G_NSA + g))],
        out_specs=[pl.BlockSpec((1, tq, HPG_NSA * HEAD_DIM), lambda bb, g, i: (bb, i, g)),
                   pl.BlockSpec((1, 1, tq, n_sel), lambda bb, g, i: (bb, g, i, 0))],
        out_shape=[jax.ShapeDtypeStruct((b, t, W_NSA), BF16),
                   jax.ShapeDtypeStruct((b, G_NSA, t, n_sel), BF16)],
        compiler_params=_cparams(("arbitrary",) * 3),
        name="cmpsel_prompt",
    )(pb, kvc, kvc)


def _merge_kernel(ocmp_ref, osel_ref, owin_ref, ofox_ref, small_ref, x_ref, ggrp_ref, wout_ref, gpost_ref,
                  gt_ref, gpre_ref, sc_ref, sh_ref, wrh_ref, wrl_ref, br_ref, cnt0_ref,
                  x1_ref, h2_ref, route_ref, cnt_ref, *, n_parts):
    @pl.when((pl.program_id(0) == 0) & (pl.program_id(1) == 0))
    def _():
        cnt_ref[...] = cnt0_ref[...]

    rn = x_ref.shape[1] // n_parts
    for part in range(n_parts):
        rs = slice(part * rn, (part + 1) * rn)
        mod = lambda ref: ref[0] if ref.shape[1] == 1 else ref[0, rs]
        gs = small_ref[0, rs]
        parts = []
        for h in range(H_NSA):
            sl = slice(h * HEAD_DIM, (h + 1) * HEAD_DIM)
            parts.append(gs[:, h:h + 1] * ocmp_ref[0, rs, sl].astype(F32)
                         + gs[:, H_NSA + h:H_NSA + h + 1] * osel_ref[0, rs, sl].astype(F32)
                         + gs[:, 2 * H_NSA + h:2 * H_NSA + h + 1] * owin_ref[0, rs, sl].astype(F32))
        o_nsa = jnp.concatenate(parts, axis=1)
        y = jnp.concatenate([_rms(o_nsa, ggrp_ref[:, :W_NSA]),
                             _rms(ofox_ref[0, rs].astype(F32), ggrp_ref[:, W_NSA:])], axis=1).astype(BF16)
        m = jnp.dot(y, wout_ref[...], preferred_element_type=F32)
        x1 = x_ref[0, rs] + mod(gt_ref) * _rms(m, gpost_ref[...])
        x1_ref[0, rs] = x1
        h2 = _rms(x1, gpre_ref[...]) * (1.0 + mod(sc_ref)) + mod(sh_ref)
        hi = h2.astype(BF16)
        h2_ref[0, rs] = hi
        lo = (h2 - hi.astype(F32)).astype(BF16)
        logits = (jnp.dot(hi, wrh_ref[...], preferred_element_type=F32)
                  + jnp.dot(hi, wrl_ref[...], preferred_element_type=F32)
                  + jnp.dot(lo, wrh_ref[...], preferred_element_type=F32) + br_ref[...])
        lane = lax.broadcasted_iota(I32, logits.shape, 1)
        vals = jnp.where(lane < N_EXPERTS, logits, -jnp.inf)
        top_v, top_e = [], []
        for _ in range(TOP_K):
            mx = jnp.max(vals, axis=1, keepdims=True)
            idx = jnp.min(jnp.where(vals == mx, lane, LANES), axis=1, keepdims=True)
            top_v.append(mx)
            top_e.append(idx)
            vals = jnp.where(lane == idx, -jnp.inf, vals)
        ex = [jnp.exp(v - top_v[0]) for v in top_v]
        den = ex[0] + ex[1] + ex[2] + ex[3]
        hits = [jnp.where(lane == e, 1.0, 0.0) for e in top_e]
        per_tok = hits[0] + hits[1] + hits[2] + hits[3]
        earlier = jnp.where(lax.broadcasted_iota(I32, (rn, rn), 1) < lax.broadcasted_iota(I32, (rn, rn), 0),
                            1.0, 0.0).astype(BF16)
        before = cnt_ref[0:1, :] + jnp.dot(earlier, per_tok.astype(BF16), preferred_element_type=F32)
        cnt_ref[...] = cnt_ref[...] + jnp.sum(per_tok, axis=0, keepdims=True)
        route = jnp.zeros(logits.shape, F32)
        for kx in range(TOP_K):
            route = jnp.where(lane == kx, top_e[kx].astype(F32), route)
            route = jnp.where(lane == TOP_K + kx, ex[kx] / den, route)
            route = jnp.where(lane == 2 * TOP_K + kx, jnp.sum(hits[kx] * before, axis=1, keepdims=True), route)
        route_ref[0, rs] = route


def merge_call(ocmp, osel, owin, ofox, small, x, ggrp, wout, gpost, gt, gpre, sc, sh, wrh, wrl, br, cnt0, tm):
    b, t, d = x.shape
    per_row = sc.shape[1] != 1
    row = lambda w: pl.BlockSpec((1, tm, w), lambda bb, i: (bb, i, 0))
    const = lambda shape: pl.BlockSpec(shape, lambda bb, i: (0,) * len(shape))
    mod = row(d) if per_row else pl.BlockSpec((1, 1, d), lambda bb, i: (bb, 0, 0))
    return pl.pallas_call(
        functools.partial(_merge_kernel, n_parts=2 if tm % 512 == 0 else 1),
        grid=(b, t // tm),
        in_specs=[row(W_NSA), row(W_NSA), row(W_NSA), row(W_FOX), row(LANES), row(d),
                  const((1, d)), const((d, d)), const((1, d)), mod, const((1, d)), mod, mod,
                  const((d, LANES)), const((d, LANES)), const((1, LANES)), const((8, LANES))],
        out_specs=[row(d), row(d), row(LANES), const((8, LANES))],
        out_shape=[jax.ShapeDtypeStruct((b, t, d), F32), jax.ShapeDtypeStruct((b, t, d), BF16),
                   jax.ShapeDtypeStruct((b, t, LANES), F32), jax.ShapeDtypeStruct((8, LANES), F32)],
        compiler_params=_cparams(("arbitrary", "arbitrary")),
        name="merge",
    )(ocmp, osel, owin, ofox, small, x, ggrp, wout, gpost, gt, gpre, sc, sh, wrh, wrl, br, cnt0)


MOE_TM = 512
MOE_TF = 1024
MOE_TN = 1024
MOE_CHUNKS = 4


def _expert_changed(blk_e_ref, i):
    return (i == 0) | (blk_e_ref[i] != blk_e_ref[jnp.maximum(i - 1, 0)])


def _moe_up_kernel(blk_e_ref, nused_ref, x_ref, wg_ref, wl_ref, bg_ref, bl_ref, a_ref, wg_bf, wl_bf, *, blk0):
    step = pl.program_id(1)
    i = blk0 + step
    live = i < nused_ref[0]

    @pl.when(live & ((step == 0) | (blk_e_ref[i] != blk_e_ref[jnp.maximum(i - 1, 0)])))
    def _():
        wg_bf[...] = wg_ref[0].astype(BF16)
        wl_bf[...] = wl_ref[0].astype(BF16)

    @pl.when(live)
    def _():
        x = x_ref[...]
        ug = jnp.dot(x, wg_bf[...], preferred_element_type=F32) + bg_ref[0]
        ul = jnp.dot(x, wl_bf[...], preferred_element_type=F32) + bl_ref[0]
        glu = jnp.minimum(ug, SWIGLU_LIMIT)
        lin = jnp.clip(ul, -SWIGLU_LIMIT, SWIGLU_LIMIT)
        a_ref[...] = (glu * jax.nn.sigmoid(SWIGLU_ALPHA * glu) * (lin + 1.0)).astype(a_ref.dtype)

    @pl.when(jnp.logical_not(live))
    def _():
        a_ref[...] = jnp.zeros(a_ref.shape, a_ref.dtype)


def moe_up_call(blk_e, n_used, xs, w_up, b_up, blk0):
    rows, d = xs.shape
    nf = D_FF // MOE_TF
    expert = lambda be, i: be[blk0 + i]
    grid_spec = pltpu.PrefetchScalarGridSpec(
        num_scalar_prefetch=2,
        grid=(nf, rows // MOE_TM),
        in_specs=[pl.BlockSpec((MOE_TM, d), lambda f, i, be, nu: (i, 0)),
                  pl.BlockSpec((1, d, MOE_TF), lambda f, i, be, nu: (expert(be, i), 0, f)),
                  pl.BlockSpec((1, d, MOE_TF), lambda f, i, be, nu: (expert(be, i), 0, nf + f)),
                  pl.BlockSpec((1, 1, MOE_TF), lambda f, i, be, nu: (expert(be, i), 0, f)),
                  pl.BlockSpec((1, 1, MOE_TF), lambda f, i, be, nu: (expert(be, i), 0, nf + f))],
        out_specs=pl.BlockSpec((MOE_TM, MOE_TF), lambda f, i, be, nu: (i, f)),
        scratch_shapes=[pltpu.VMEM((d, MOE_TF), BF16), pltpu.VMEM((d, MOE_TF), BF16)])
    return pl.pallas_call(
        functools.partial(_moe_up_kernel, blk0=blk0),
        grid_spec=grid_spec,
        out_shape=jax.ShapeDtypeStruct((rows, D_FF), BF16),
        compiler_params=_cparams(("arbitrary", "arbitrary")),
        name="moe_up",
    )(blk_e, n_used, xs, w_up, w_up, b_up, b_up)


def _moe_down_kernel(blk_e_ref, nused_ref, *refs, bounds):
    a_refs = refs[:len(bounds)]
    wd_ref, bd_ref, sw_ref, y_ref, wd_bf = refs[len(bounds):]
    i = pl.program_id(1)

    @pl.when(_expert_changed(blk_e_ref, i))
    def _():
        wd_bf[...] = wd_ref[0].astype(BF16)

    def project(a_ref):
        y = jnp.dot(a_ref[...], wd_bf[...], preferred_element_type=F32) + bd_ref[0]
        y_ref[...] = (y * sw_ref[...]).astype(y_ref.dtype)

    for (b0, b1), a_ref in zip(bounds, a_refs):
        pl.when((i >= b0) & (i < b1) & (i < nused_ref[0]))(functools.partial(project, a_ref))

    @pl.when(i >= nused_ref[0])
    def _():
        y_ref[...] = jnp.zeros(y_ref.shape, y_ref.dtype)


def moe_down_call(blk_e, n_used, a_chunks, bounds, w_down, b_down, slot_w, col0, ncol):
    dff = a_chunks[0].shape[1]
    n_blk = blk_e.shape[0]
    a_specs = [pl.BlockSpec((MOE_TM, dff), lambda c, i, be, nu, b0=b0, b1=b1: (jnp.clip(i - b0, 0, b1 - b0 - 1), 0))
               for b0, b1 in bounds]
    grid_spec = pltpu.PrefetchScalarGridSpec(
        num_scalar_prefetch=2,
        grid=(ncol, n_blk),
        in_specs=a_specs + [pl.BlockSpec((1, dff, MOE_TN), lambda c, i, be, nu: (be[i], 0, col0 + c)),
                            pl.BlockSpec((1, 1, MOE_TN), lambda c, i, be, nu: (be[i], 0, col0 + c)),
                            pl.BlockSpec((MOE_TM, 1), lambda c, i, be, nu: (i, 0))],
        out_specs=pl.BlockSpec((MOE_TM, MOE_TN), lambda c, i, be, nu: (i, c)),
        scratch_shapes=[pltpu.VMEM((dff, MOE_TN), BF16)])
    return pl.pallas_call(
        functools.partial(_moe_down_kernel, bounds=tuple(bounds)),
        grid_spec=grid_spec,
        out_shape=jax.ShapeDtypeStruct((n_blk * MOE_TM, ncol * MOE_TN), BF16),
        compiler_params=_cparams(("arbitrary", "arbitrary")),
        name="moe_down",
    )(blk_e, n_used, *a_chunks, w_down, b_down, slot_w)


def moe_dispatch(route, counts):
    n = route.shape[0]
    n_asg = n * TOP_K
    n_blk = -(-(n_asg + N_EXPERTS * (MOE_TM - 1)) // MOE_TM)
    e_flat = route[:, :TOP_K].astype(I32).reshape(-1)
    w_flat = route[:, TOP_K:2 * TOP_K].reshape(-1)
    pos = route[:, 2 * TOP_K:3 * TOP_K].astype(I32).reshape(-1)
    order = jnp.argsort(e_flat).astype(I32)
    grp_start = jnp.cumsum(counts) - counts
    padded = (counts + MOE_TM - 1) // MOE_TM * MOE_TM
    pad_end = jnp.cumsum(padded)
    pad_start = pad_end - padded
    blk_start = jnp.arange(n_blk, dtype=I32) * MOE_TM
    blk_e = jnp.minimum(jnp.sum((pad_end[None, :] <= blk_start[:, None]).astype(I32), axis=1), N_EXPERTS - 1)
    n_used = (pad_end[-1:] // MOE_TM).astype(I32)
    off = (jnp.arange(n_blk * MOE_TM, dtype=I32).reshape(n_blk, MOE_TM) - pad_start[blk_e][:, None])
    live = (off < counts[blk_e][:, None]).reshape(-1)
    src = order[jnp.clip(grp_start[blk_e][:, None] + off, 0, n_asg - 1).reshape(-1)]
    slot_tok = jnp.where(live, src // TOP_K, 0)
    slot_w = jnp.where(live, w_flat[src], 0.0)
    dest = (pos + pad_start[e_flat]).reshape(n, TOP_K).T
    return slot_tok, slot_w[:, None], dest, blk_e, n_used


def _final_kernel(*refs):
    *yg_refs, x1_ref, gt_ref, gpost_ref, o_ref = refs
    parts = []
    for yg_ref in yg_refs:
        s = yg_ref[0, 0].astype(F32)
        for kx in range(1, TOP_K):
            s = s + yg_ref[kx, 0].astype(F32)
        parts.append(s)
    o_ref[0] = x1_ref[0] + gt_ref[0] * _rms(jnp.concatenate(parts, axis=1), gpost_ref[...])


def final_call(ygs, x1, gt, gpost, tm):
    b, t, d = x1.shape
    per_row = gt.shape[1] != 1
    mod = (pl.BlockSpec((1, tm, d), lambda bb, i: (bb, i, 0)) if per_row
           else pl.BlockSpec((1, 1, d), lambda bb, i: (bb, 0, 0)))
    return pl.pallas_call(
        _final_kernel,
        grid=(b, t // tm),
        in_specs=[pl.BlockSpec((TOP_K, 1, tm, yg.shape[-1]), lambda bb, i: (0, bb, i, 0)) for yg in ygs]
                 + [pl.BlockSpec((1, tm, d), lambda bb, i: (bb, i, 0)),
                    mod, pl.BlockSpec((1, d), lambda bb, i: (0, 0))],
        out_specs=pl.BlockSpec((1, tm, d), lambda bb, i: (bb, i, 0)),
        out_shape=jax.ShapeDtypeStruct((b, t, d), F32),
        compiler_params=_cparams(("arbitrary", "arbitrary")),
        name="final",
    )(*ygs, x1, gt, gpost)


def prep_weights(w_in, b_nsa_gate, b_fox_forget, w_cmp1, w_out, w_router, b_router):
    d = w_in.shape[0]
    o_qn, o_kvn, o_gn, o_qf, o_kvf, o_fl = 0, 1024, 2560, 2584, 3608, 4632
    wm = jnp.concatenate([w_in[:, o_kvn:o_kvn + 1536], w_in[:, o_kvf:o_kvf + 1024],
                          w_in[:, o_qn:o_qn + 1024], w_in[:, o_qf:o_qf + 1024]], axis=1).astype(BF16)
    pad = LANES - 3 * H_NSA - H_FOX
    ws = jnp.concatenate([w_in[:, o_gn:o_gn + 3 * H_NSA], w_in[:, o_fl:o_fl + H_FOX],
                          jnp.zeros((d, pad), F32)], axis=1).astype(BF16)
    bs = jnp.concatenate([b_nsa_gate, b_fox_forget, jnp.zeros((pad,), F32)])[None, :]
    w1p = w_cmp1.reshape(2, CMP_STRIDE, 2 * HEAD_DIM, CMP_HID).astype(BF16)
    wr = jnp.concatenate([w_router, jnp.zeros((d, LANES - N_EXPERTS), F32)], axis=1)
    wrh = wr.astype(BF16)
    wrl = (wr - wrh.astype(F32)).astype(BF16)
    br = jnp.concatenate([b_router, jnp.zeros((LANES - N_EXPERTS,), F32)])[None, :]
    half = CMP_STRIDE // 2
    w1f = jnp.concatenate([w1p[:, :half], w1p[:, half:]], axis=-1)
    return dict(wm=wm, ws=ws, bs=bs, w1p=w1p, w1f=w1f, wout=w_out.astype(BF16), wrh=wrh, wrl=wrl, br=br)


def prompt_mixer(x, sc, sh, g_pre, W, cmpw, *, tm, tq, tk):
    b, t, _ = x.shape
    pb, kvcs, win, kvf, small = inproj_call(x, g_pre, sc, sh, W["wm"], W["ws"], W["bs"], tm)
    kvc = compress_prompt_call(kvcs, *cmpw)
    ocmp, selmask = cmpsel_prompt_call(pb, kvc, tq=tq)
    osel = flash_call("sel", pb, tq=tq, tk=tk, extra=(selmask,))
    owin = flash_call("win", pb, tq=tq, tk=tk)
    logf = small[:, :, 3 * H_NSA:3 * H_NSA + H_FOX]
    cum = cumsum_call(jnp.swapaxes(logf, 1, 2))
    c8 = jnp.pad(cum.reshape(b, KV_FOX, HPG_FOX, t), ((0, 0), (0, 0), (0, 8 - HPG_FOX), (0, 0)))
    ofox = flash_call("fox", pb, tq=tq, tk=tk, extra=(c8,))
    return ocmp, osel, owin, ofox, small, kvcs, win, kvf, logf


PAGE = 128
CMP_PAGES = 16
CMP_PITCH = 136
FOX_PAGES = 16
NS_PAD = 384


def _head_rows(qf, g):
    rows = [qf[:, (g * HPG_NSA + h) * HEAD_DIM:(g * HPG_NSA + h + 1) * HEAD_DIM] for h in range(HPG_NSA)]
    return jnp.concatenate(rows + [jnp.zeros((8 - HPG_NSA, HEAD_DIM), F32)], axis=0).astype(BF16)


def _slope_col(g):
    row = lax.broadcasted_iota(I32, (8, 1), 0)
    col = jnp.zeros((8, 1), F32)
    for h in range(HPG_NSA):
        col = jnp.where(row == h, 2.0 ** -(g * HPG_NSA + h + 1), col)
    return col


def _masked_softmax(s, mask):
    s = jnp.where(mask, s, -1e30)
    e = jnp.where(mask, jnp.exp(s - jnp.max(s, axis=1, keepdims=True)), 0.0)
    return e / jnp.maximum(jnp.sum(e, axis=1, keepdims=True), 1e-30)


def _store_heads(o_ref, o, g):
    for h in range(HPG_NSA):
        c = (g * HPG_NSA + h) * HEAD_DIM
        o_ref[0, :, c:c + HEAD_DIM] = o[h:h + 1].astype(o_ref.dtype)


def _compress_sample_kernel(pt_ref, *refs, npg):
    pages = refs[:npg + 1]
    w1_ref, hb_ref, w2_ref, b2_ref, o_ref, x_scr = refs[npg + 1:]
    per_page = PAGE // CMP_STRIDE
    n = (npg + 1) * per_page
    chunk_rows = CMP_STRIDE * 8
    for p in range(npg + 1):
        for c in range(per_page):
            x_scr[pl.ds((p * per_page + c) * CMP_PITCH, chunk_rows), :] = pages[p][0, c * chunk_rows:(c + 1) * chunk_rows, :]
    load = lambda cb, s: x_scr[pl.ds(s * 8 + cb, n, stride=CMP_PITCH), :]
    r = _compress_rows(load, n, w1_ref, hb_ref, w2_ref, b2_ref)
    o_ref[0] = r[:npg * per_page].astype(o_ref.dtype)


def compress_sample_call(page_table, cache3, w1f, hb, w2, b2):
    bsz, n_pages = page_table.shape
    npg = CMP_PAGES
    assert n_pages % npg == 0
    page_specs = [pl.BlockSpec((1, PAGE * 8, HEAD_DIM),
                               lambda bb, gi, pt, p=p: (pt[bb, jnp.minimum(gi * npg + p, n_pages - 1)], 0, 0))
                  for p in range(npg + 1)]
    rows_out = npg * PAGE // CMP_STRIDE
    grid_spec = pltpu.PrefetchScalarGridSpec(
        num_scalar_prefetch=1,
        grid=(bsz, n_pages // npg),
        in_specs=page_specs + _cmp_weight_specs(),
        out_specs=pl.BlockSpec((1, rows_out, 512), lambda bb, gi, pt: (bb, gi, 0)),
        scratch_shapes=[pltpu.VMEM(((npg + 1) * (PAGE // CMP_STRIDE) * CMP_PITCH, HEAD_DIM), F32)])
    return pl.pallas_call(
        functools.partial(_compress_sample_kernel, npg=npg),
        grid_spec=grid_spec,
        out_shape=jax.ShapeDtypeStruct((bsz, n_pages * PAGE // CMP_STRIDE, 512), BF16),
        compiler_params=_cparams(("arbitrary", "arbitrary")),
        name="compress_sample",
    )(page_table, *([cache3] * (npg + 1)), w1f, hb, w2, b2)


def _cmpsel_sample_kernel(q_ref, kv_ref, o_ref, idx_ref, *, past, n_cmp, n_keep):
    nc = kv_ref.shape[1]
    qf = q_ref[0].astype(F32)
    cidx = lax.broadcasted_iota(I32, (1, nc), 1)
    mask = jnp.where(cidx < n_cmp, cidx * CMP_STRIDE + CMP_LEN - 1, 2 ** 30) <= past
    center = (cidx * CMP_STRIDE - past).astype(F32) + 0.5 * (CMP_LEN - 1)
    overlap = _overlap_matrix(nc, NS_PAD)
    blk = lax.broadcasted_iota(I32, (8, NS_PAD), 1)
    lag = past // SEL_BLOCK - blk
    ii =lax.broadcasted_iota(I32, (NS_PAD, NS_PAD), 0)
    jj = lax.broadcasted_iota(I32, (NS_PAD, NS_PAD), 1)
    slot = lax.broadcasted_iota(I32, (NS_PAD, LANES), 1).astype(F32)
    rows = []
    for g in range(G_NSA):
        k = kv_ref[0, :, g * HEAD_DIM:(g + 1) * HEAD_DIM]
        v = kv_ref[0, :, (G_NSA + g) * HEAD_DIM:(G_NSA + g + 1) * HEAD_DIM]
        s = _nt_dot(_head_rows(qf, g), k) * LN2 + _slope_col(g) * center
        p = _masked_softmax(s, jnp.broadcast_to(mask, s.shape))
        _store_heads(o_ref, jnp.dot(p.astype(BF16), v, preferred_element_type=F32), g)
        imp = jnp.sum(p[0:HPG_NSA], axis=0, keepdims=True)
        score = _split_dot(jnp.broadcast_to(imp, (8, nc)), overlap)
        score = _force_scores(score, blk, lag)
        row = score[0:1, :]
        col = score.T[:, 0:1]
        ge = jnp.where(col >= row, 1.0, 0.0)
        gt = jnp.where(col > row, 1.0, 0.0)
        rank_row = jnp.sum(jnp.where(ii < jj, ge, gt), axis=0, keepdims=True)
        rank_col = jnp.sum(jnp.where(jj < ii, 1.0 - gt, 1.0 - ge), axis=1, keepdims=True)
        sel_row = jnp.where(rank_row < n_keep, 1.0, 0.0)
        sel_col = jnp.where(rank_col < n_keep, 1.0, 0.0)
        before = jnp.sum(jnp.where(jj < ii, sel_row, 0.0), axis=1, keepdims=True)
        pick = jnp.where(before == slot, sel_col * ii[:, 0:1].astype(F32), 0.0)
        rows.append(jnp.sum(pick, axis=0, keepdims=True))
    idx_ref[0] = jnp.concatenate(rows + [jnp.zeros((8 - G_NSA, LANES), F32)], axis=0).astype(I32)


def cmpsel_sample_call(qn, kvc, *, past):
    bsz = qn.shape[0]
    nc = kvc.shape[1]
    n_cmp = (past + 1) // CMP_STRIDE - 1
    n_sel = -(-(past + 1) // SEL_BLOCK)
    assert n_sel <= NS_PAD and n_cmp <= nc
    n_keep = min(N_SEL, n_sel)
    return pl.pallas_call(
        functools.partial(_cmpsel_sample_kernel, past=past, n_cmp=n_cmp, n_keep=n_keep),
        grid=(bsz,),
        in_specs=[pl.BlockSpec((1, 1, W_NSA), lambda bb: (bb, 0, 0)),
                  pl.BlockSpec((1, nc, 512), lambda bb: (bb, 0, 0))],
        out_specs=[pl.BlockSpec((1, 1, W_NSA), lambda bb: (bb, 0, 0)),
                   pl.BlockSpec((1, 8, LANES), lambda bb: (bb, 0, 0))],
        out_shape=[jax.ShapeDtypeStruct((bsz, 1, W_NSA), BF16), jax.ShapeDtypeStruct((bsz, 8, LANES), I32)],
        compiler_params=_cparams(("arbitrary",)),
        name="cmpsel_sample",
    )(qn, kvc)


def _sel_sample_kernel(idx_ref, pt_ref, q_ref, new_ref, cache_ref, o_ref, kvbuf, sem, *, past, n_pages, n_keep):
    bb = pl.program_id(0)
    per_page = PAGE // SEL_BLOCK
    blk_rows = SEL_BLOCK * 8
    copies = []
    for g in range(G_NSA):
        for kx in range(n_keep):
            j = idx_ref[(bb * G_NSA + g) * n_keep + kx]
            page = pt_ref[bb, jnp.minimum(j // per_page, n_pages - 1)]
            r0 = pl.multiple_of((j % per_page) * blk_rows, blk_rows)
            cp = pltpu.make_async_copy(cache_ref.at[page, pl.ds(r0, blk_rows), :], kvbuf.at[g, kx], sem.at[0])
            cp.start()
            copies.append(cp)
    for cp in copies:
        cp.wait()
    qf = q_ref[0].astype(F32)
    nkeys = n_keep * SEL_BLOCK
    lane = lax.broadcasted_iota(I32, (1, nkeys), 1)
    row0 = lax.broadcasted_iota(I32, (SEL_BLOCK, HEAD_DIM), 0) == 0
    for g in range(G_NSA):
        kpos = lane % SEL_BLOCK
        ks, vs = [], []
        for kx in range(n_keep):
            j = idx_ref[(bb * G_NSA + g) * n_keep + kx]
            is_new = j * SEL_BLOCK >= past
            fresh = jnp.logical_and(is_new, row0)
            ks.append(jnp.where(fresh, new_ref[0, 4 + g:5 + g, :], kvbuf[g, kx, pl.ds(4 + g, SEL_BLOCK, stride=8), :]))
            vs.append(jnp.where(fresh, new_ref[0, 6 + g:7 + g, :], kvbuf[g, kx, pl.ds(6 + g, SEL_BLOCK, stride=8), :]))
            kpos = kpos + jnp.where(lane // SEL_BLOCK == kx, j * SEL_BLOCK, 0)
        k = jnp.concatenate(ks, axis=0).astype(BF16)
        v = jnp.concatenate(vs, axis=0).astype(BF16)
        s = _nt_dot(_head_rows(qf, g), k) * LN2 + _slope_col(g) * (kpos - past).astype(F32)
        p = _masked_softmax(s, jnp.broadcast_to(kpos <= past, s.shape))
        _store_heads(o_ref, jnp.dot(p.astype(BF16), v, preferred_element_type=F32), g)


def sel_sample_call(idx_flat, page_table, qn, kvcs_new, cache3, *, past, n_keep):
    bsz, n_pages = page_table.shape
    assert past % SEL_BLOCK == 0 and past == n_pages * PAGE
    grid_spec = pltpu.PrefetchScalarGridSpec(
        num_scalar_prefetch=2,
        grid=(bsz,),
        in_specs=[pl.BlockSpec((1, 1, W_NSA), lambda bb, ix, pt: (bb, 0, 0)),
                  pl.BlockSpec((1, 8, HEAD_DIM), lambda bb, ix, pt: (bb, 0, 0)),
                  pl.BlockSpec(memory_space=pl.ANY)],
        out_specs=pl.BlockSpec((1, 1, W_NSA), lambda bb, ix, pt: (bb, 0, 0)),
        scratch_shapes=[pltpu.VMEM((G_NSA, n_keep, SEL_BLOCK * 8, HEAD_DIM), F32),
                        pltpu.SemaphoreType.DMA((1,))])
    return pl.pallas_call(
        functools.partial(_sel_sample_kernel, past=past, n_pages=n_pages, n_keep=n_keep),
        grid_spec=grid_spec,
        out_shape=jax.ShapeDtypeStruct((bsz, 1, W_NSA), BF16),
        compiler_params=_cparams(("arbitrary",)),
        name="sel_sample",
    )(idx_flat, page_table, qn, kvcs_new, cache3)


def _win_sample_kernel(q_ref, buf_ref, new_ref, o_ref, nb_ref):
    rows = buf_ref.shape[1]
    wb = rows // 4
    nb_ref[0] = pltpu.roll(buf_ref[0], rows - 4, axis=0)
    nb_ref[0, rows - 4:rows, :] = new_ref[0]
    qf = q_ref[0].astype(F32)
    krel = (lax.broadcasted_iota(I32, (1, wb), 1) - (wb - 1)).astype(F32)
    for g in range(G_NSA):
        k = nb_ref[0, pl.ds(g, wb, stride=4), :].astype(BF16)
        v = nb_ref[0, pl.ds(G_NSA + g, wb, stride=4), :].astype(BF16)
        s = _nt_dot(_head_rows(qf, g), k) * LN2 + _slope_col(g) * krel
        p = _masked_softmax(s, jnp.full(s.shape, True))
        _store_heads(o_ref, jnp.dot(p.astype(BF16), v, preferred_element_type=F32), g)


def win_sample_call(qn, win_buf, win_new):
    bsz, rows, _ = win_buf.shape
    assert rows == WINDOW * 4
    return pl.pallas_call(
        _win_sample_kernel,
        grid=(bsz,),
        in_specs=[pl.BlockSpec((1, 1, W_NSA), lambda bb: (bb, 0, 0)),
                  pl.BlockSpec((1, rows, HEAD_DIM), lambda bb: (bb, 0, 0)),
                  pl.BlockSpec((1, 4, HEAD_DIM), lambda bb: (bb, 0, 0))],
        out_specs=[pl.BlockSpec((1, 1, W_NSA), lambda bb: (bb, 0, 0)),
                   pl.BlockSpec((1, rows, HEAD_DIM), lambda bb: (bb, 0, 0))],
        out_shape=[jax.ShapeDtypeStruct((bsz, 1, W_NSA), BF16), jax.ShapeDtypeStruct((bsz, rows, HEAD_DIM), F32)],
        compiler_params=_cparams(("arbitrary",)),
        name="win_sample",
    )(qn, win_buf, win_new)


def _fox_sample_kernel(pt_ref, *refs, npg, n_steps):
    kv_pages = refs[:npg]
    lf_pages = refs[npg:2 * npg]
    q_ref, new_ref, lfn_ref, o_ref, qbd_scr, m_scr, l_scr, acc_scr, carry_scr = refs[2 * npg:]
    st = pl.program_id(1)
    kvw = KV_FOX * HEAD_DIM

    @pl.when(st == 0)
    def _():
        qf = q_ref[0].astype(F32)
        zero = jnp.zeros((1, HEAD_DIM), F32)
        rows = [jnp.concatenate([qf[:, h * HEAD_DIM:(h + 1) * HEAD_DIM] if c == h // HPG_FOX else zero
                                 for c in range(KV_FOX)], axis=1) for h in range(H_FOX)]
        qbd_scr[...] = jnp.concatenate(rows, axis=0)
        m_scr[...] = jnp.full(m_scr.shape, M_FLOOR, F32)
        l_scr[...] = jnp.zeros(l_scr.shape, F32)
        acc_scr[...] = jnp.zeros(acc_scr.shape, F32)
        carry_scr[...] = lfn_ref[0]

    qbd = qbd_scr[...].astype(BF16)
    tri = jnp.where(lax.broadcasted_iota(I32, (PAGE, PAGE), 0) <= lax.broadcasted_iota(I32, (PAGE, PAGE), 1),
                    1.0, 0.0).astype(BF16)
    carry = carry_scr[...]
    parts = []
    def heads(ref, base):
        n = ref.shape[1] // 8
        return jnp.concatenate([ref[0, pl.ds(base + h, n, stride=8), :] for h in range(KV_FOX)], axis=1)

    for p in range(npg):
        k = heads(kv_pages[p], 0).astype(BF16)
        incl = _split_dot(lf_pages[p][0], tri)
        tot = incl[:, PAGE - 1:PAGE]
        parts.append(_nt_dot(qbd, k) * LN2 + (carry + (tot - incl)))
        carry = carry + tot
    carry_scr[...] = carry
    s = jnp.concatenate(parts, axis=1)
    m_prev = m_scr[...]
    m_new = jnp.maximum(m_prev, jnp.max(s, axis=1, keepdims=True))
    alpha = jnp.exp(m_prev - m_new)
    pr = jnp.exp(s - m_new)
    l_scr[...] = alpha * l_scr[...] + jnp.sum(pr, axis=1, keepdims=True)
    acc = alpha * acc_scr[...]
    for p in range(npg):
        v = heads(kv_pages[p], KV_FOX).astype(BF16)
        acc = acc + jnp.dot(pr[:, p * PAGE:(p + 1) * PAGE].astype(BF16), v, preferred_element_type=F32)
    acc_scr[...] = acc
    m_scr[...] = m_new

    @pl.when(st == n_steps - 1)
    def _():
        kn = heads(new_ref, 0)
        vn = heads(new_ref, KV_FOX)
        s_new = jnp.sum(qbd_scr[...] * kn, axis=1, keepdims=True) * LN2
        m_fin = jnp.maximum(m_scr[...], s_new)
        a2 = jnp.exp(m_scr[...] - m_fin)
        p_new = jnp.exp(s_new - m_fin)
        o = (a2 * acc_scr[...] + p_new * vn) / jnp.maximum(a2 * l_scr[...] + p_new, 1e-30)
        for h in range(H_FOX):
            c = (h // HPG_FOX) * HEAD_DIM
            o_ref[0, :, h * HEAD_DIM:(h + 1) * HEAD_DIM] = o[h:h + 1, c:c + HEAD_DIM].astype(o_ref.dtype)


def fox_sample_call(page_table, qf, kvf_new, lf_new, cache3, logf_t):
    bsz, n_pages = page_table.shape
    npg = FOX_PAGES
    assert n_pages % npg == 0
    n_steps = n_pages // npg
    page_of = lambda bb, st, pt, p: pt[bb, n_pages - 1 - (st * npg + p)]
    kv_specs = [pl.BlockSpec((1, PAGE * 8, HEAD_DIM), lambda bb, st, pt, p=p: (page_of(bb, st, pt, p), 0, 0))
                for p in range(npg)]
    lf_specs = [pl.BlockSpec((1, H_FOX, PAGE), lambda bb, st, pt, p=p: (page_of(bb, st, pt, p), 0, 0))
                for p in range(npg)]
    grid_spec = pltpu.PrefetchScalarGridSpec(
        num_scalar_prefetch=1,
        grid=(bsz, n_steps),
        in_specs=kv_specs + lf_specs + [pl.BlockSpec((1, 1, W_FOX), lambda bb, st, pt: (bb, 0, 0)),
                                        pl.BlockSpec((1, 8, HEAD_DIM), lambda bb, st, pt: (bb, 0, 0)),
                                        pl.BlockSpec((1, H_FOX, 1), lambda bb, st, pt: (bb, 0, 0))],
        out_specs=pl.BlockSpec((1, 1, W_FOX), lambda bb, st, pt: (bb, 0, 0)),
        scratch_shapes=[pltpu.VMEM((H_FOX, KV_FOX * HEAD_DIM), F32), pltpu.VMEM((H_FOX, 1), F32),
                        pltpu.VMEM((H_FOX, 1), F32), pltpu.VMEM((H_FOX, KV_FOX * HEAD_DIM), F32),
                        pltpu.VMEM((H_FOX, 1), F32)])
    return pl.pallas_call(
        functools.partial(_fox_sample_kernel, npg=npg, n_steps=n_steps),
        grid_spec=grid_spec,
        out_shape=jax.ShapeDtypeStruct((bsz, 1, W_FOX), BF16),
        compiler_params=_cparams(("arbitrary", "arbitrary")),
        name="fox_sample",
    )(page_table, *([cache3] * npg), *([logf_t] * npg), qf, kvf_new, lf_new)


def sample_mixer(x, sc, sh, g_pre, W, cmpw, cache_nsa3, win_buf, cache_fox3, logf_t, page_table):
    bsz = x.shape[1]
    n_pages = page_table.shape[1]
    past = n_pages * PAGE
    pb, kvcs, win, kvf, small = inproj_call(x, g_pre, sc, sh, W["wm"], W["ws"], W["bs"], bsz)
    per_seq = lambda a: a.reshape(bsz, -1, a.shape[-1])
    pb, kvcs, win, kvf, small = map(per_seq, (pb, kvcs, win, kvf, small))
    qn = pb[:, :, COL_QN:COL_QN + W_NSA]
    qf = pb[:, :, COL_QF:COL_QF + W_FOX]
    kvc = compress_sample_call(page_table, cache_nsa3, *cmpw)
    ocmp, idx = cmpsel_sample_call(qn, kvc, past=past)
    n_keep = min(N_SEL, -(-(past + 1) // SEL_BLOCK))
    idx_flat = idx[:, :G_NSA, :n_keep].reshape(-1)
    osel = sel_sample_call(idx_flat, page_table, qn, kvcs, cache_nsa3, past=past, n_keep=n_keep)
    owin, win_new = win_sample_call(qn, win_buf, win)
    logf = small[:, :, 3 * H_NSA:3 * H_NSA + H_FOX]
    ofox = fox_sample_call(page_table, qf, kvf, logf.reshape(bsz, H_FOX, 1), cache_fox3, logf_t)
    to_rows = lambda a: a.reshape(1, bsz, a.shape[-1])
    return tuple(map(to_rows, (ocmp, osel, owin, ofox, small))) + (kvcs, win_new, kvf, logf)


PROMPT_TM = 512
ATTN_TQ = 512
ATTN_TK = 512


def kernel(x_prompt, x_sample, c_prompt, c_sample, cache_nsa_kv, state_nsa_win, cache_fox_kv, cache_fox_logf,
           page_table, w_ada, b_ada, g_pre_mix, g_post_mix, g_pre_ffn, g_post_ffn, w_in, b_nsa_gate, b_fox_forget,
           w_cmp1, b_cmp1, w_cmp2, b_cmp2, pe_cmp, g_grp, w_out, w_router, b_router, w_up, b_up, w_down, b_down):
    assert w_ada.shape[0] == 1 and x_sample.shape[1] == 1
    bp, t, d = x_prompt.shape
    bs = x_sample.shape[0]
    n_phys = cache_nsa_kv.shape[1]
    W = prep_weights(w_in[0], b_nsa_gate[0], b_fox_forget[0], w_cmp1[0], w_out[0], w_router[0], b_router[0])
    hb = cmp_bias_call(pe_cmp[0].reshape(2, CMP_LEN // 2, 2 * HEAD_DIM), W["w1p"], b_cmp1[0])
    cmpw = (W["w1f"], hb, w_cmp2[0].astype(BF16), b_cmp2[0])
    row = lambda a: a[0][None, :]

    ada = ada_call(jnp.concatenate([c_prompt, c_sample], axis=0), w_ada[0], row(b_ada))
    mods_p = [m[:, None, :] for m in jnp.split(ada[:bp], 6, axis=-1)]
    mods_s = [m[None] for m in jnp.split(ada[bp:], 6, axis=-1)]
    xs = x_sample.reshape(1, bs, d)

    ocmp, osel, owin, ofox, small, kvcs_p, win_p, kvf_p, logf_p = prompt_mixer(
        x_prompt, mods_p[1], mods_p[0], row(g_pre_mix), W, cmpw,
        tm=PROMPT_TM, tq=ATTN_TQ, tk=ATTN_TK)
    merge_args = (row(g_grp), W["wout"], row(g_post_mix))
    ffn_args = (W["wrh"], W["wrl"], W["br"])
    x1_p, h2_p, route_p, cnt_p = merge_call(ocmp, osel, owin, ofox, small, x_prompt, *merge_args, mods_p[2],
                                            row(g_pre_ffn), mods_p[4], mods_p[3], *ffn_args,
                                            jnp.zeros((8, LANES), F32), PROMPT_TM)

    cache_nsa3 = cache_nsa_kv.reshape(n_phys, PAGE * 8, HEAD_DIM)
    cache_fox3 = cache_fox_kv.reshape(n_phys, PAGE * 8, HEAD_DIM)
    logf_t = jnp.swapaxes(cache_fox_logf[0], 1, 2)
    ocmp_s, osel_s, owin_s, ofox_s, small_s, kvcs_s, win_s, kvf_s, logf_s = sample_mixer(
        xs, mods_s[1], mods_s[0], row(g_pre_mix), W, cmpw,
        cache_nsa3, state_nsa_win.reshape(bs, WINDOW * 4, HEAD_DIM), cache_fox3, logf_t, page_table)
    x1_s, h2_s, route_s, cnt_all = merge_call(ocmp_s, osel_s, owin_s, ofox_s, small_s, xs, *merge_args, mods_s[2],
                                              row(g_pre_ffn), mods_s[4], mods_s[3], *ffn_args, cnt_p, bs)

    n_p = bp * t
    h2_all = jnp.concatenate([h2_p.reshape(n_p, d), h2_s.reshape(bs, d)], axis=0)
    route_all = jnp.concatenate([route_p.reshape(n_p, LANES), route_s.reshape(bs, LANES)], axis=0)[:, :3 * TOP_K]
    slot_tok, slot_w, dest, blk_e, n_used = moe_dispatch(route_all, cnt_all[0, :N_EXPERTS].astype(I32))
    n_blk = blk_e.shape[0]
    bounds = [(c * n_blk // MOE_CHUNKS, (c + 1) * n_blk // MOE_CHUNKS) for c in range(MOE_CHUNKS)]
    a_chunks = [moe_up_call(blk_e, n_used, h2_all[slot_tok[b0 * MOE_TM:b1 * MOE_TM]], w_up[0], b_up[0][:, None, :], b0)
                for b0, b1 in bounds]
    ncol = d // MOE_TN
    ys = [moe_down_call(blk_e, n_used, a_chunks, bounds, w_down[0], b_down[0][:, None, :], slot_w, c, 1)
          for c in range(ncol)]
    yg_p = [y[dest[:, :n_p]].reshape(TOP_K, bp, t, MOE_TN) for y in ys]
    yg_s = [y[dest[:, n_p:]].reshape(TOP_K, 1, bs, MOE_TN) for y in ys]
    y_p = final_call(yg_p, x1_p, mods_p[5], row(g_post_ffn), PROMPT_TM)
    y_s = final_call(yg_s, x1_s, mods_s[5], row(g_post_ffn), bs)

    wlen = min(WINDOW, t)
    return (y_p, y_s.reshape(bs, 1, d),
            kvcs_p.reshape(1, bp, t, 2, 2, G_NSA, HEAD_DIM),
            win_p[:, (t - wlen) * 4:].reshape(1, bp, wlen, 2, G_NSA, HEAD_DIM),
            kvf_p.reshape(1, bp, t, 2, KV_FOX, HEAD_DIM),
            logf_p[None],
            kvcs_s.reshape(1, bs, 1, 2, 2, G_NSA, HEAD_DIM),
            win_s.reshape(1, bs, WINDOW, 2, G_NSA, HEAD_DIM),
            kvf_s.reshape(1, bs, 1, 2, KV_FOX, HEAD_DIM),
            logf_s.reshape(1, bs, 1, H_FOX))
```
